```python
import math
import jax
import jax.numpy as jnp
from jax import lax
import numpy as np

D_MODEL = 1024
BATCH = 8
SEQ = 2048
DEPTH = 4

N_MIXERS = 4
N_META = 16
CHUNK = 64
PAD = CHUNK - N_META
D_FF = 4 * D_MODEL
NORM_EPS = 1e-5
N_RWKV_LAYERS = (DEPTH + 3) // N_MIXERS
N_SSD_LAYERS = (DEPTH + 2) // N_MIXERS
N_GLA_LAYERS = (DEPTH + 1) // N_MIXERS
N_RET_LAYERS = DEPTH // N_MIXERS

RWKV_HEAD = 64
RWKV_HEADS = D_MODEL // RWKV_HEAD
RWKV_DECAY_LORA = 64
RWKV_AAA_LORA = 64
RWKV_GATE_LORA = 160
RWKV_GN_EPS = 64e-5

M2_D_INNER = 2 * D_MODEL
M2_HEAD = 64
M2_HEADS = M2_D_INNER // M2_HEAD
M2_GROUPS = 8
M2_HPG = M2_HEADS // M2_GROUPS
M2_STATE = 128
M2_CONV = 4
M2_CONV_DIM = M2_D_INNER + 2 * M2_GROUPS * M2_STATE
M2_PROJ = M2_D_INNER + M2_CONV_DIM + M2_HEADS

GLA_HEADS = 4
GLA_DK = (D_MODEL // 2) // GLA_HEADS
GLA_DV = D_MODEL // GLA_HEADS
GLA_GATE_LORA = 16
GLA_TAU = 16.0
GLA_PROJ = 2 * GLA_HEADS * GLA_DK + 2 * GLA_HEADS * GLA_DV + GLA_GATE_LORA

RET_HEADS = 4
RET_DK = D_MODEL // RET_HEADS
RET_DV = 2 * D_MODEL // RET_HEADS
RET_PROJ = 2 * RET_HEADS * RET_DK + 2 * RET_HEADS * RET_DV
ROPE_BASE = 10000.0

kernel_name = 'hybrid_rwkv7_ssd_gla_retnet_trunk'


def _rms(x, eps=NORM_EPS):
    xf = x.astype(jnp.float32)
    return xf * lax.rsqrt(jnp.mean(xf * xf, axis=-1, keepdims=True) + eps)


def rmsnorm(x, w):
    return (_rms(x) * w.astype(jnp.float32)).astype(x.dtype)


def _layernorm(x, eps):
    xf = x.astype(jnp.float32)
    mu = jnp.mean(xf, axis=-1, keepdims=True)
    var = jnp.mean(jnp.square(xf - mu), axis=-1, keepdims=True)
    return (xf - mu) * lax.rsqrt(var + eps)


def rotary(x, pos):
    half = x.shape[-1] // 2
    inv_freq = 1.0 / (ROPE_BASE ** jnp.linspace(0.0, 1.0, half, dtype=jnp.float32))
    ang = pos[:, None] * inv_freq[None, :]
    cos = jnp.cos(ang)[:, None, :].astype(x.dtype)
    sin = jnp.sin(ang)[:, None, :].astype(x.dtype)
    x1, x2 = x[..., :half], x[..., half:]
    return jnp.concatenate([x1 * cos - x2 * sin, x1 * sin + x2 * cos], axis=-1)


def _to_chunks(a, t_axis):
    pad = [(0, 0)] * a.ndim
    pad[t_axis] = (PAD, 0)
    a = jnp.pad(a, pad)
    n = a.shape[t_axis] // CHUNK
    a = a.reshape(a.shape[:t_axis] + (n, CHUNK) + a.shape[t_axis + 1:])
    return jnp.moveaxis(a, t_axis, 0)


def _from_chunks(y, t_axis):
    y = jnp.moveaxis(y, 0, t_axis)
    y = y.reshape(y.shape[:t_axis] + (-1,) + y.shape[t_axis + 2:])
    return lax.slice_in_dim(y, PAD, y.shape[t_axis], axis=t_axis)


def chunked_scalar_decay(q, k, v, log_a):
    B, G, R, _, dv = v.shape
    dk = q.shape[-1]
    causal = jnp.tril(jnp.ones((CHUNK, CHUNK), dtype=bool))

    def step(S, inp):
        qc, kc, vc, ac = inp
        cum = jnp.cumsum(ac, axis=-1)
        diff = cum[..., :, None] - cum[..., None, :]
        decay = jnp.exp(jnp.where(causal, diff, -jnp.inf))
        scores = jnp.einsum('bgik,bgjk->bgij', qc, kc)[:, :, None] * decay
        y = jnp.einsum('bgrij,bgrjv->bgriv', scores, vc)
        y = y + jnp.exp(cum)[..., None] * jnp.einsum('bgik,bgrkv->bgriv', qc, S)
        last = cum[..., -1:]
        S = jnp.exp(last)[..., None] * S + jnp.einsum('bgjk,bgrj,bgrjv->bgrkv', kc, jnp.exp(last - cum), vc)
        return S, y

    S0 = jnp.zeros((B, G, R, dk, dv), v.dtype)
    _, ys = lax.scan(step, S0, (_to_chunks(q, 2), _to_chunks(k, 2), _to_chunks(v, 3), _to_chunks(log_a, 3)))
    return _from_chunks(ys, 3)


def chunked_vector_decay(q, k, v, log_g):
    B, H, _, dk = q.shape
    dv = v.shape[-1]
    causal = jnp.tril(jnp.ones((CHUNK, CHUNK), dtype=bool))[:, :, None]

    def step(S, inp):
        qc, kc, vc, gc = inp
        cum = jnp.cumsum(gc, axis=2)
        diff = cum[:, :, :, None, :] - cum[:, :, None, :, :]
        decay = jnp.exp(jnp.where(causal, diff, -jnp.inf))
        scores = jnp.einsum('bhik,bhijk,bhjk->bhij', qc, decay, kc)
        y = jnp.einsum('bhij,bhjv->bhiv', scores, vc)
        y = y + jnp.einsum('bhik,bhkv->bhiv', qc * jnp.exp(cum), S)
        last = cum[:, :, -1:, :]
        S = jnp.exp(last[:, :, 0])[..., None] * S + jnp.einsum('bhjk,bhjv->bhkv', kc * jnp.exp(last - cum), vc)
        return S, y

    S0 = jnp.zeros((B, H, dk, dv), v.dtype)
    _, ys = lax.scan(step, S0, (_to_chunks(q, 2), _to_chunks(k, 2), _to_chunks(v, 2), _to_chunks(log_g, 2)))
    return _from_chunks(ys, 2)


def _rwkv7_scan(r, w, k, v, kk, a):
    B, L, H, N = r.shape

    def step(S, inp):
        rt, wt, kt, vt, kkt, at = inp
        sa = jnp.einsum('bhvk,bhk->bhv', S, -kkt)
        S = S * wt[:, :, None, :] + sa[..., None] * (kkt * at)[:, :, None, :] + vt[..., None] * kt[:, :, None, :]
        return S, jnp.einsum('bhvk,bhk->bhv', S, rt)

    S0 = jnp.zeros((B, H, N, N), jnp.float32)
    seq = tuple(jnp.moveaxis(t.astype(jnp.float32), 1, 0) for t in (r, w, k, v, kk, a))
    _, ys = lax.scan(step, S0, seq)
    return jnp.moveaxis(ys, 0, 1)


def rwkv7_mix(u, mu, w_r, w_k, w_v, w0, w_lora_a, w_lora_b, a0, a_lora_a, a_lora_b,
              g_lora_a, g_lora_b, k_k, k_a, r_k, ln_w, ln_b, w_o):
    B, L, D = u.shape
    xx = jnp.pad(u, ((0, 0), (1, 0), (0, 0)))[:, :-1] - u
    xr, xw, xk, xv, xa, xg = [u + xx * mu[i] for i in range(6)]
    r = xr @ w_r
    k = xk @ w_k
    v = xv @ w_v
    w = -jax.nn.softplus(-(w0 + jnp.tanh(xw @ w_lora_a) @ w_lora_b)) - 0.5
    a = jax.nn.sigmoid(a0 + (xa @ a_lora_a) @ a_lora_b)
    g = jax.nn.sigmoid(xg @ g_lora_a) @ g_lora_b
    heads = lambda t: t.reshape(B, L, RWKV_HEADS, RWKV_HEAD)
    kkf = heads(k * k_k).astype(jnp.float32)
    kk = (kkf / jnp.maximum(jnp.sqrt(jnp.sum(kkf * kkf, axis=-1, keepdims=True)), 1e-12)).astype(u.dtype)
    k = k * (1.0 + (a - 1.0) * k_a)
    decay = jnp.exp(-jnp.exp(w))
    y = _rwkv7_scan(heads(r), heads(decay), heads(k), heads(v), kk, heads(a))
    y = _layernorm(y, RWKV_GN_EPS).reshape(B, L, D) * ln_w + ln_b
    bonus = (jnp.sum(heads(r) * heads(k) * r_k, axis=-1, keepdims=True) * heads(v)).reshape(B, L, D)
    return ((y.astype(u.dtype) + bonus) * g) @ w_o


def mamba2_mix(u, in_proj, conv_w, conv_b, dt_bias, a_log, d_skip, norm_w, out_proj):
    B, L, _ = u.shape
    z, xbc, dt = jnp.split(u @ in_proj, [M2_D_INNER, M2_D_INNER + M2_CONV_DIM], axis=-1)
    xbc = lax.conv_general_dilated(xbc, conv_w[:, None, :], window_strides=(1,), padding=[(M2_CONV - 1, 0)],
                                   dimension_numbers=('NWC', 'WIO', 'NWC'),
                                   feature_group_count=M2_CONV_DIM) + conv_b
    xbc = jax.nn.silu(xbc)
    xs, b_in, c_in = jnp.split(xbc, [M2_D_INNER, M2_D_INNER + M2_GROUPS * M2_STATE], axis=-1)
    dt = jax.nn.softplus(dt + dt_bias).reshape(B, L, M2_GROUPS, M2_HPG)
    a = -jnp.exp(a_log).reshape(M2_GROUPS, M2_HPG)
    xs = xs.reshape(B, L, M2_GROUPS, M2_HPG, M2_HEAD)
    q = c_in.reshape(B, L, M2_GROUPS, M2_STATE).transpose(0, 2, 1, 3)
    k = b_in.reshape(B, L, M2_GROUPS, M2_STATE).transpose(0, 2, 1, 3)
    v = (xs * dt[..., None]).transpose(0, 2, 3, 1, 4)
    log_a = (dt * a).transpose(0, 2, 3, 1)
    y = chunked_scalar_decay(q, k, v, log_a).transpose(0, 3, 1, 2, 4)
    y = y + d_skip.reshape(M2_GROUPS, M2_HPG)[..., None] * xs
    y = y.reshape(B, L, M2_D_INNER) * jax.nn.silu(z)
    y = _rms(y.reshape(B, L, M2_GROUPS, M2_D_INNER // M2_GROUPS)).reshape(B, L, M2_D_INNER)
    return (y * norm_w).astype(u.dtype) @ out_proj


def gla_mix(u, in_proj, gate_up, gate_bias, norm_w, out_proj):
    B, L, _ = u.shape
    qk = GLA_HEADS * GLA_DK
    vd = GLA_HEADS * GLA_DV
    q, k, v, r, glr = jnp.split(u @ in_proj, [qk, 2 * qk, 2 * qk + vd, 2 * qk + 2 * vd], axis=-1)
    gk = (jax.nn.log_sigmoid((glr @ gate_up + gate_bias).astype(jnp.float32)) / GLA_TAU).astype(u.dtype)
    heads = lambda t, d: t.reshape(B, L, GLA_HEADS, d).transpose(0, 2, 1, 3)
    o = chunked_vector_decay(heads(q, GLA_DK) * GLA_DK ** -0.5, heads(k, GLA_DK), heads(v, GLA_DV), heads(gk, GLA_DK))
    o = (_rms(o) * norm_w).astype(u.dtype).transpose(0, 2, 1, 3).reshape(B, L, vd)
    return (o * jax.nn.silu(r)) @ out_proj


def retnet_mix(u, pos, in_proj, out_proj):
    B, L, _ = u.shape
    qd = RET_HEADS * RET_DK
    vd = RET_HEADS * RET_DV
    q, k, v, g = jnp.split(u @ in_proj, [qd, 2 * qd, 2 * qd + vd], axis=-1)
    q = rotary(q.reshape(B, L, RET_HEADS, RET_DK), pos).transpose(0, 2, 1, 3)
    k = (rotary(k.reshape(B, L, RET_HEADS, RET_DK), pos) * RET_DK ** -0.5).transpose(0, 2, 1, 3)
    v = v.reshape(B, L, RET_HEADS, RET_DV).transpose(0, 2, 1, 3)[:, :, None]
    log_gamma = jnp.log1p(-jnp.exp2(-5.0 - jnp.arange(RET_HEADS, dtype=jnp.float32))).astype(u.dtype)
    log_a = jnp.broadcast_to(log_gamma[None, :, None, None], (B, RET_HEADS, 1, L))
    y = chunked_scalar_decay(q, k, v, log_a)[:, :, 0]
    y = _layernorm(y, NORM_EPS).astype(u.dtype).transpose(0, 2, 1, 3).reshape(B, L, vd)
    return (jax.nn.silu(g) * y) @ out_proj


def sqrelu_mlp(u, w_in, w_out):
    return jnp.square(jax.nn.relu(u @ w_in)) @ w_out


def setup_inputs(seed: int = 0) -> dict:
    key = jax.random.key(seed)
    keys = iter(jax.random.split(key, 64))
    f32 = jnp.float32
    D = D_MODEL

    def nrm(shape, fan_in):
        return jax.random.normal(next(keys), shape, f32) * fan_in ** -0.5

    def noise(shape, s):
        return s * jax.random.normal(next(keys), shape, f32)

    def gain(shape):
        return 1.0 + noise(shape, 0.02)

    NA, NB, NC, ND = N_RWKV_LAYERS, N_SSD_LAYERS, N_GLA_LAYERS, N_RET_LAYERS
    dt = jnp.exp(jax.random.uniform(next(keys), (NB, M2_HEADS), f32) * (math.log(0.1) - math.log(1e-3)) + math.log(1e-3))
    inp = {
        'x': jax.random.normal(next(keys), (BATCH, SEQ, D), f32),
        'meta_tokens': noise((N_META, D), 1.0),
        'norm_mix': gain((DEPTH, D)),
        'norm_mlp': gain((DEPTH, D)),
        'norm_final': gain((D,)),
        'mlp_w_in': nrm((DEPTH, D, D_FF), D),
        'mlp_w_out': nrm((DEPTH, D_FF, D), D_FF),
        'rwkv_mu': jax.random.uniform(next(keys), (NA, 6, D), f32),
        'rwkv_w_r': nrm((NA, D, D), D),
        'rwkv_w_k': nrm((NA, D, D), D),
        'rwkv_w_v': nrm((NA, D, D), D),
        'rwkv_w0': jnp.linspace(-6.0, -1.0, D, dtype=f32)[None, :] + noise((NA, D), 0.1),
        'rwkv_w_lora_a': nrm((NA, D, RWKV_DECAY_LORA), D),
        'rwkv_w_lora_b': nrm((NA, RWKV_DECAY_LORA, D), RWKV_DECAY_LORA),
        'rwkv_a0': noise((NA, D), 0.1),
        'rwkv_a_lora_a': nrm((NA, D, RWKV_AAA_LORA), D),
        'rwkv_a_lora_b': nrm((NA, RWKV_AAA_LORA, D), RWKV_AAA_LORA),
        'rwkv_g_lora_a': nrm((NA, D, RWKV_GATE_LORA), D),
        'rwkv_g_lora_b': nrm((NA, RWKV_GATE_LORA, D), RWKV_GATE_LORA),
        'rwkv_k_k': 0.85 + noise((NA, D), 0.05),
        'rwkv_k_a': 1.0 + noise((NA, D), 0.05),
        'rwkv_r_k': noise((NA, RWKV_HEADS, RWKV_HEAD), 0.1),
        'rwkv_ln_w': gain((NA, D)),
        'rwkv_ln_b': noise((NA, D), 0.01),
        'rwkv_w_o': nrm((NA, D, D), D),
        'm2_in_proj': nrm((NB, D, M2_PROJ), D),
        'm2_conv_w': nrm((NB, M2_CONV, M2_CONV_DIM), M2_CONV),
        'm2_conv_b': noise((NB, M2_CONV_DIM), 0.01),
        'm2_dt_bias': dt + jnp.log(-jnp.expm1(-dt)),
        'm2_a_log': jnp.log(jax.random.uniform(next(keys), (NB, M2_HEADS), f32, minval=1.0, maxval=16.0)),
        'm2_d': gain((NB, M2_HEADS)),
        'm2_norm_w': gain((NB, M2_D_INNER)),
        'm2_out_proj': nrm((NB, M2_D_INNER, D), M2_D_INNER),
        'gla_in_proj': nrm((NC, D, GLA_PROJ), D),
        'gla_gate_up': nrm((NC, GLA_GATE_LORA, GLA_HEADS * GLA_DK), GLA_GATE_LORA),
        'gla_gate_bias': noise((NC, GLA_HEADS * GLA_DK), 0.5),
        'gla_norm_w': gain((NC, GLA_DV)),
        'gla_out_proj': nrm((NC, GLA_HEADS * GLA_DV, D), GLA_HEADS * GLA_DV),
        'ret_in_proj': nrm((ND, D, RET_PROJ), D),
        'ret_out_proj': nrm((ND, RET_HEADS * RET_DV, D), RET_HEADS * RET_DV),
    }
    return inp


def reference(x, meta_tokens, norm_mix, norm_mlp, norm_final, mlp_w_in, mlp_w_out,
              rwkv_mu, rwkv_w_r, rwkv_w_k, rwkv_w_v, rwkv_w0, rwkv_w_lora_a, rwkv_w_lora_b,
              rwkv_a0, rwkv_a_lora_a, rwkv_a_lora_b, rwkv_g_lora_a, rwkv_g_lora_b,
              rwkv_k_k, rwkv_k_a, rwkv_r_k, rwkv_ln_w, rwkv_ln_b, rwkv_w_o,
              m2_in_proj, m2_conv_w, m2_conv_b, m2_dt_bias, m2_a_log, m2_d, m2_norm_w, m2_out_proj,
              gla_in_proj, gla_gate_up, gla_gate_bias, gla_norm_w, gla_out_proj,
              ret_in_proj, ret_out_proj):
    B, S, D = x.shape
    meta = jnp.broadcast_to(meta_tokens[None].astype(x.dtype), (B, N_META, D))
    h = jnp.concatenate([meta, x], axis=1)
    pos = jnp.arange(N_META + S, dtype=jnp.float32)
    for i in range(DEPTH):
        m, j = i % N_MIXERS, i // N_MIXERS
        u = rmsnorm(h, norm_mix[i])
        if m == 0:
            y = rwkv7_mix(u, rwkv_mu[j], rwkv_w_r[j], rwkv_w_k[j], rwkv_w_v[j], rwkv_w0[j],
                          rwkv_w_lora_a[j], rwkv_w_lora_b[j], rwkv_a0[j], rwkv_a_lora_a[j], rwkv_a_lora_b[j],
                          rwkv_g_lora_a[j], rwkv_g_lora_b[j], rwkv_k_k[j], rwkv_k_a[j], rwkv_r_k[j],
                          rwkv_ln_w[j], rwkv_ln_b[j], rwkv_w_o[j])
        elif m == 1:
            y = mamba2_mix(u, m2_in_proj[j], m2_conv_w[j], m2_conv_b[j], m2_dt_bias[j], m2_a_log[j],
                           m2_d[j], m2_norm_w[j], m2_out_proj[j])
        elif m == 2:
            y = gla_mix(u, gla_in_proj[j], gla_gate_up[j], gla_gate_bias[j], gla_norm_w[j], gla_out_proj[j])
        else:
            y = retnet_mix(u, pos, ret_in_proj[j], ret_out_proj[j])
        h = h + y
        h = h + sqrelu_mlp(rmsnorm(h, norm_mlp[i]), mlp_w_in[i], mlp_w_out[i])
    h = rmsnorm(h, norm_final)
    return h[:, N_META:]
```

```python
import functools
import math

import jax
import jax.numpy as jnp
from jax import lax
from jax.experimental import pallas as pl
from jax.experimental.pallas import tpu as pltpu

F32 = jnp.float32
BF16 = jnp.bfloat16

N_META = 16
NORM_EPS = 1e-5
SEQ_ALIGN = 128
ROW_TILE = 512
VMEM_LIMIT = 56 * 1024 * 1024

RWKV_HEAD = 64
RWKV_GN_EPS = 64e-5
RWKV_CHUNK = 64

M2_HEAD = 64
M2_GROUPS = 8
M2_STATE = 128
M2_CONV = 4
M2_CHUNK = 128

GLA_HEADS = 4
GLA_TAU = 16.0
GLA_CHUNK = 64
GLA_SUB = 16

RET_HEADS = 4
RET_CHUNK = 128
ROPE_BASE = 10000.0


def _cparams(*sem):
    return pltpu.CompilerParams(dimension_semantics=sem, vmem_limit_bytes=VMEM_LIMIT)


def _dot(a, b):
    return jnp.dot(a.astype(BF16), b.astype(BF16), preferred_element_type=F32)


def _dot_nt(a, b):
    return lax.dot_general(a.astype(BF16), b.astype(BF16), (((1,), (1,)), ((), ())),
                           preferred_element_type=F32)


def _dot_tn(a, b):
    return lax.dot_general(a.astype(BF16), b.astype(BF16), (((0,), (0,)), ((), ())),
                           preferred_element_type=F32)


def _dot_f32(a, b):
    return jnp.dot(a, b, precision=lax.Precision.HIGHEST, preferred_element_type=F32)


def _rms_rows(x, w):
    return x * lax.rsqrt(jnp.mean(x * x, axis=-1, keepdims=True) + NORM_EPS) * w


def _tri(n, strict=False):
    r = lax.broadcasted_iota(jnp.int32, (n, n), 0)
    c = lax.broadcasted_iota(jnp.int32, (n, n), 1)
    return (r > c) if strict else (r >= c)


def _norm_linear_kernel(h_ref, nw_ref, w_ref, o_ref, u_ref):
    @pl.when(pl.program_id(1) == 0)
    def _():
        u_ref[...] = _rms_rows(h_ref[...], nw_ref[...]).astype(BF16)

    o_ref[...] = jnp.dot(u_ref[...], w_ref[...], preferred_element_type=F32).astype(o_ref.dtype)


def norm_linear(h, nw, w, tn, out_dtype=F32):
    M, D = h.shape
    N = w.shape[1]
    return pl.pallas_call(
        _norm_linear_kernel,
        grid=(M // ROW_TILE, N // tn),
        in_specs=[pl.BlockSpec((ROW_TILE, D), lambda i, j: (i, 0)),
                  pl.BlockSpec((1, D), lambda i, j: (0, 0)),
                  pl.BlockSpec((D, tn), lambda i, j: (0, j))],
        out_specs=pl.BlockSpec((ROW_TILE, tn), lambda i, j: (i, j)),
        out_shape=jax.ShapeDtypeStruct((M, N), out_dtype),
        scratch_shapes=[pltpu.VMEM((ROW_TILE, D), BF16)],
        compiler_params=_cparams("parallel", "arbitrary"),
        name="norm_linear",
    )(h, nw.reshape(1, D), w)


def _linear_res_kernel(a_ref, w_ref, h_ref, o_ref):
    o_ref[...] = h_ref[...] + _dot(a_ref[...], w_ref[...])


def linear_res(a, w, h):
    M, K = a.shape
    D = w.shape[1]
    return pl.pallas_call(
        _linear_res_kernel,
        grid=(M // ROW_TILE,),
        in_specs=[pl.BlockSpec((ROW_TILE, K), lambda i: (i, 0)),
                  pl.BlockSpec((K, D), lambda i: (0, 0)),
                  pl.BlockSpec((ROW_TILE, D), lambda i: (i, 0))],
        out_specs=pl.BlockSpec((ROW_TILE, D), lambda i: (i, 0)),
        out_shape=jax.ShapeDtypeStruct((M, D), F32),
        compiler_params=_cparams("parallel"),
        name="linear_res",
    )(a, w, h)


def _mlp_kernel(h_ref, nw_ref, w1_ref, w2_ref, o_ref, u_ref):
    f = pl.program_id(1)

    @pl.when(f == 0)
    def _():
        h = h_ref[...]
        u_ref[...] = _rms_rows(h, nw_ref[...]).astype(BF16)
        o_ref[...] = h

    hid = jnp.dot(u_ref[...], w1_ref[...], preferred_element_type=F32)
    hid = jnp.square(jnp.maximum(hid, 0.0))
    o_ref[...] += _dot(hid, w2_ref[...])


def mlp_res(h, nw, w1, w2, tf=512):
    M, D = h.shape
    F = w1.shape[1]
    return pl.pallas_call(
        _mlp_kernel,
        grid=(M // ROW_TILE, F // tf),
        in_specs=[pl.BlockSpec((ROW_TILE, D), lambda i, f: (i, 0)),
                  pl.BlockSpec((1, D), lambda i, f: (0, 0)),
                  pl.BlockSpec((D, tf), lambda i, f: (0, f)),
                  pl.BlockSpec((tf, D), lambda i, f: (f, 0))],
        out_specs=pl.BlockSpec((ROW_TILE, D), lambda i, f: (i, 0)),
        out_shape=jax.ShapeDtypeStruct((M, D), F32),
        scratch_shapes=[pltpu.VMEM((ROW_TILE, D), BF16)],
        compiler_params=_cparams("parallel", "arbitrary"),
        name="mlp_res",
    )(h, nw.reshape(1, D), w1, w2)


def _final_norm_kernel(h_ref, nw_ref, o_ref):
    o_ref[...] = _rms_rows(h_ref[...], nw_ref[...])


def final_norm(h, nw):
    M, D = h.shape
    return pl.pallas_call(
        _final_norm_kernel,
        grid=(M // ROW_TILE,),
        in_specs=[pl.BlockSpec((ROW_TILE, D), lambda i: (i, 0)),
                  pl.BlockSpec((1, D), lambda i: (0, 0))],
        out_specs=pl.BlockSpec((ROW_TILE, D), lambda i: (i, 0)),
        out_shape=jax.ShapeDtypeStruct((M, D), F32),
        compiler_params=_cparams("parallel"),
        name="final_norm",
    )(h, nw.reshape(1, D))


def _rwkv_proj_kernel(h_ref, hp_ref, nw_ref, mu_ref, wr_ref, wk_ref, wv_ref,
                      wla_ref, wlb_ref, ala_ref, alb_ref, gla_ref, glb_ref,
                      w0_ref, a0_ref,
                      r_ref, k_ref, v_ref, lw_ref, a_ref, g_ref):
    nw = nw_ref[...]
    u = _rms_rows(h_ref[...], nw)
    xx = _rms_rows(hp_ref[...], nw) - u
    mu = mu_ref[...]
    mix = lambda i: (u + xx * mu[i:i + 1, :]).astype(BF16)
    r_ref[...] = jnp.dot(mix(0), wr_ref[...], preferred_element_type=F32)
    k_ref[...] = jnp.dot(mix(2), wk_ref[...], preferred_element_type=F32)
    v_ref[...] = jnp.dot(mix(3), wv_ref[...], preferred_element_type=F32)
    hw = jnp.tanh(jnp.dot(mix(1), wla_ref[...], preferred_element_type=F32))
    w = -jax.nn.softplus(-(w0_ref[...] + _dot(hw, wlb_ref[...]))) - 0.5
    lw_ref[...] = -jnp.exp(w)
    ha = jnp.dot(mix(4), ala_ref[...], preferred_element_type=F32)
    a_ref[...] = jax.nn.sigmoid(a0_ref[...] + _dot(ha, alb_ref[...]))
    hg = jax.nn.sigmoid(jnp.dot(mix(5), gla_ref[...], preferred_element_type=F32))
    g_ref[...] = _dot(hg, glb_ref[...])


def rwkv_proj(h, hp, nw, mu, wr, wk, wv, wla, wlb, ala, alb, gla, glb, w0, a0):
    M, D = h.shape
    row = pl.BlockSpec((ROW_TILE, D), lambda i: (i, 0))
    full = lambda x: pl.BlockSpec(x.shape, lambda i: (0, 0))
    consts = (nw.reshape(1, D), mu, wr, wk, wv, wla, wlb, ala, alb, gla, glb,
              w0.reshape(1, D), a0.reshape(1, D))
    return pl.pallas_call(
        _rwkv_proj_kernel,
        grid=(M // ROW_TILE,),
        in_specs=[row, row] + [full(c) for c in consts],
        out_specs=[row] * 6,
        out_shape=[jax.ShapeDtypeStruct((M, D), F32)] * 6,
        compiler_params=_cparams("parallel"),
        name="rwkv_proj",
    )(h, hp, *consts)


def _rwkv_scan_kernel(r_ref, k_ref, v_ref, lw_ref, a_ref, g_ref,
                      kk_ref, ka_ref, rk_ref, lnw_ref, lnb_ref, o_ref, s_ref):
    C = RWKV_CHUNK
    N = RWKV_HEAD

    @pl.when(pl.program_id(2) == 0)
    def _():
        s_ref[...] = jnp.zeros_like(s_ref)

    lw_all = lw_ref[...]
    cum_all = _dot_f32(_tri(C).astype(F32), lw_all)
    incl = _tri(C)
    strict = _tri(C, strict=True)
    eye = (lax.broadcasted_iota(jnp.int32, (C, C), 0) ==
           lax.broadcasted_iota(jnp.int32, (C, C), 1)).astype(F32)
    outs = []
    for hd in range(r_ref.shape[-1] // N):
        sl = slice(hd * N, (hd + 1) * N)
        r = r_ref[:, sl]
        k = k_ref[:, sl]
        v = v_ref[:, sl]
        a = a_ref[:, sl]
        lw = lw_all[:, sl]
        cum = cum_all[:, sl]
        kkr = k * kk_ref[:, sl]
        kk = kkr / jnp.maximum(jnp.sqrt(jnp.sum(kkr * kkr, axis=-1, keepdims=True)), 1e-12)
        km = k * (1.0 + (a - 1.0) * ka_ref[:, sl])
        beta = kk * a
        e_pos = jnp.exp(cum)
        e_neg = jnp.exp(-cum)
        al_t = -kk * jnp.exp(cum - lw)
        r_t = r * e_pos
        b_t = beta * e_neg
        k_t = km * e_neg
        x = _dot_nt(jnp.concatenate([al_t, r_t], axis=0), jnp.concatenate([b_t, k_t], axis=0))
        a_ab = jnp.where(strict, x[:C, :C], 0.0)
        a_ak = jnp.where(strict, x[:C, C:], 0.0)
        a_rb = jnp.where(incl, x[C:, :C], 0.0)
        a_rk = jnp.where(incl, x[C:, C:], 0.0)
        p = a_ab
        t = eye + p
        for _ in range(int(math.log2(C)) - 1):
            p = _dot_f32(p, p)
            t = t + _dot_f32(t, p)
        s0 = s_ref[hd]
        u = _dot_f32(t, _dot_nt(al_t, s0) + _dot(a_ak, v))
        y = _dot_nt(r_t, s0) + _dot(a_rk, v) + _dot(a_rb, u)
        last = cum[C - 1:C, :]
        e_last = jnp.exp(last - cum)
        s_ref[hd] = s0 * jnp.exp(last) + _dot_tn(v, km * e_last) + _dot_tn(u, beta * e_last)
        mu_y = jnp.mean(y, axis=-1, keepdims=True)
        var = jnp.mean(jnp.square(y - mu_y), axis=-1, keepdims=True)
        yn = (y - mu_y) * lax.rsqrt(var + RWKV_GN_EPS) * lnw_ref[:, sl] + lnb_ref[:, sl]
        bonus = jnp.sum(r * km * rk_ref[:, sl], axis=-1, keepdims=True) * v
        outs.append((yn + bonus) * g_ref[:, sl])
    o_ref[...] = jnp.concatenate(outs, axis=-1)


def rwkv_scan(r, k, v, lw, a, g, k_k, k_a, r_k, ln_w, ln_b):
    B, L, D = r.shape
    C = RWKV_CHUNK
    W = 2 * RWKV_HEAD
    seq = pl.BlockSpec((None, C, W), lambda b, p, c: (b, c, p))
    par = pl.BlockSpec((1, W), lambda b, p, c: (0, p))
    return pl.pallas_call(
        _rwkv_scan_kernel,
        grid=(B, D // W, L // C),
        in_specs=[seq] * 6 + [par] * 5,
        out_specs=seq,
        out_shape=jax.ShapeDtypeStruct((B, L, D), F32),
        scratch_shapes=[pltpu.VMEM((W // RWKV_HEAD, RWKV_HEAD, RWKV_HEAD), F32)],
        compiler_params=_cparams("parallel", "parallel", "arbitrary"),
        name="rwkv_scan",
    )(r, k, v, lw, a, g, *(p.reshape(1, D) for p in (k_k, k_a, r_k, ln_w, ln_b)))


def rwkv_layer(h, nw, p):
    B, L, D = h.shape
    hp = jnp.pad(h, ((0, 0), (1, 0), (0, 0)))[:, :-1]
    bf = lambda x: x.astype(BF16)
    outs = rwkv_proj(h.reshape(B * L, D), hp.reshape(B * L, D), nw, p['mu'],
                     bf(p['w_r']), bf(p['w_k']), bf(p['w_v']),
                     bf(p['w_lora_a']), bf(p['w_lora_b']), bf(p['a_lora_a']), bf(p['a_lora_b']),
                     bf(p['g_lora_a']), bf(p['g_lora_b']), p['w0'], p['a0'])
    r, k, v, lw, a, g = (o.reshape(B, L, D) for o in outs)
    o = rwkv_scan(r, k, v, lw, a, g, p['k_k'], p['k_a'], p['r_k'].reshape(D), p['ln_w'], p['ln_b'])
    return linear_res(o.reshape(B * L, D), bf(p['w_o']), h.reshape(B * L, D)).reshape(B, L, D)


def _sdecay_kernel(q_ref, k_ref, v_ref, cc_ref, cr_ref, o_ref, s_ref, *, R, P, C):
    @pl.when(pl.program_id(2) == 0)
    def _():
        s_ref[...] = jnp.zeros_like(s_ref)

    q = q_ref[...]
    k = k_ref[...]
    causal = _tri(C)
    scores = _dot_nt(q, k)
    s0 = s_ref[...]
    qs = _dot(q, s0)
    cc = cc_ref[...]
    cr = cr_ref[...]
    for r in range(R):
        sl = slice(r * P, (r + 1) * P)
        ccr = cc[:, r:r + 1]
        dec = jnp.exp(jnp.where(causal, ccr - cr[r:r + 1, :], -jnp.inf))
        vr = v_ref[:, sl]
        o_ref[:, sl] = _dot(scores * dec, vr) + jnp.exp(ccr) * qs[:, sl]
        last = cc[C - 1:C, r:r + 1]
        s_ref[:, sl] = jnp.exp(last) * s0[:, sl] + _dot_tn(k, vr * jnp.exp(last - ccr))


def sdecay_scan(q, k, v, cum, *, G, R, C):
    B, L, GN = q.shape
    N = GN // G
    P = v.shape[-1] // (G * R)
    nc = L // C
    cum5 = cum.reshape(B, nc, C, G, R)
    cc = cum5.transpose(0, 3, 1, 2, 4)
    cr = cum5.transpose(0, 3, 1, 4, 2)
    qk_spec = pl.BlockSpec((None, C, N), lambda b, g, c: (b, c, g))
    v_spec = pl.BlockSpec((None, C, R * P), lambda b, g, c: (b, c, g))
    return pl.pallas_call(
        functools.partial(_sdecay_kernel, R=R, P=P, C=C),
        grid=(B, G, nc),
        in_specs=[qk_spec, qk_spec, v_spec,
                  pl.BlockSpec((None, None, None, C, R), lambda b, g, c: (b, g, c, 0, 0)),
                  pl.BlockSpec((None, None, None, R, C), lambda b, g, c: (b, g, c, 0, 0))],
        out_specs=v_spec,
        out_shape=jax.ShapeDtypeStruct(v.shape, F32),
        scratch_shapes=[pltpu.VMEM((N, R * P), F32)],
        compiler_params=_cparams("parallel", "parallel", "arbitrary"),
        name="sdecay_scan",
    )(q, k, v, cc, cr)


def _chunk_cumsum(x, C):
    B, L, H = x.shape
    return jnp.cumsum(x.reshape(B, L // C, C, H), axis=2).reshape(B, L, H)


def mamba2_layer(h, nw, p):
    B, L, D = h.shape
    d_inner = p['norm_w'].shape[0]
    heads = p['a_log'].shape[0]
    gn = M2_GROUPS * M2_STATE
    hpg = heads // M2_GROUPS
    w_in = p['in_proj']
    n_main = d_inner + d_inner + 2 * gn
    w_dt = jnp.pad(w_in[:, n_main:], ((0, 0), (0, 128 - heads)))
    h2 = h.reshape(B * L, D)
    zx = norm_linear(h2, nw, w_in[:, :n_main].astype(BF16), tn=1024).reshape(B, L, n_main)
    dt = norm_linear(h2, nw, w_dt.astype(BF16), tn=128)[:, :heads].reshape(B, L, heads)
    z, xbc = zx[..., :d_inner], zx[..., d_inner:]
    xp = jnp.pad(xbc, ((0, 0), (M2_CONV - 1, 0), (0, 0)))
    conv = sum(xp[:, i:i + L] * p['conv_w'][i] for i in range(M2_CONV)) + p['conv_b']
    xbc = jax.nn.silu(conv)
    xs, b_in, c_in = xbc[..., :d_inner], xbc[..., d_inner:d_inner + gn], xbc[..., d_inner + gn:]
    dt = jax.nn.softplus(dt + p['dt_bias'])
    log_a = dt * (-jnp.exp(p['a_log']))
    v = (xs.reshape(B, L, heads, M2_HEAD) * dt[..., None]).reshape(B, L, d_inner)
    y = sdecay_scan(c_in, b_in, v, _chunk_cumsum(log_a, M2_CHUNK), G=M2_GROUPS, R=hpg, C=M2_CHUNK)
    y = y + (xs.reshape(B, L, heads, M2_HEAD) * p['d'][:, None]).reshape(B, L, d_inner)
    y = y * jax.nn.silu(z)
    yg = y.reshape(B, L, M2_GROUPS, d_inner // M2_GROUPS)
    yg = yg * lax.rsqrt(jnp.mean(yg * yg, axis=-1, keepdims=True) + NORM_EPS)
    y = yg.reshape(B, L, d_inner) * p['norm_w']
    return linear_res(y.reshape(B * L, d_inner), p['out_proj'].astype(BF16), h2).reshape(B, L, D)


def _rotary(x, pos):
    half = x.shape[-1] // 2
    inv_freq = 1.0 / (ROPE_BASE ** jnp.linspace(0.0, 1.0, half, dtype=F32))
    ang = pos[:, None] * inv_freq[None, :]
    cos = jnp.cos(ang)[:, None, :]
    sin = jnp.sin(ang)[:, None, :]
    x1, x2 = x[..., :half], x[..., half:]
    return jnp.concatenate([x1 * cos - x2 * sin, x1 * sin + x2 * cos], axis=-1)


def retnet_layer(h, nw, p):
    B, L, D = h.shape
    w_in = p['in_proj']
    vd = p['out_proj'].shape[0]
    qd = (w_in.shape[1] - 2 * vd) // 2
    dk = qd // RET_HEADS
    h2 = h.reshape(B * L, D)
    proj = norm_linear(h2, nw, w_in.astype(BF16), tn=1024).reshape(B, L, -1)
    q, k, v, g = (proj[..., :qd], proj[..., qd:2 * qd], proj[..., 2 * qd:2 * qd + vd],
                  proj[..., 2 * qd + vd:])
    pos = jnp.arange(L, dtype=F32)
    q = _rotary(q.reshape(B, L, RET_HEADS, dk), pos).reshape(B, L, qd)
    k = (_rotary(k.reshape(B, L, RET_HEADS, dk), pos) * dk ** -0.5).reshape(B, L, qd)
    log_gamma = jnp.log1p(-jnp.exp2(-5.0 - jnp.arange(RET_HEADS, dtype=F32)))
    cum = jnp.broadcast_to(log_gamma[None, None, :], (B, L, RET_HEADS))
    y = sdecay_scan(q, k, v, _chunk_cumsum(cum, RET_CHUNK), G=RET_HEADS, R=1, C=RET_CHUNK)
    yh = y.reshape(B, L, RET_HEADS, vd // RET_HEADS)
    mu = jnp.mean(yh, axis=-1, keepdims=True)
    var = jnp.mean(jnp.square(yh - mu), axis=-1, keepdims=True)
    yn = ((yh - mu) * lax.rsqrt(var + NORM_EPS)).reshape(B, L, vd)
    return linear_res((jax.nn.silu(g) * yn).reshape(B * L, vd), p['out_proj'].astype(BF16),
                      h2).reshape(B, L, D)


def _gla_kernel(q_ref, k_ref, g_ref, v_ref, o_ref, s_ref, *, scale):
    C = GLA_CHUNK
    SB = GLA_SUB

    @pl.when(pl.program_id(2) == 0)
    def _():
        s_ref[...] = jnp.zeros_like(s_ref)

    q = q_ref[...] * scale
    k = k_ref[...]
    v = v_ref[...]
    gc = _dot_f32(_tri(C).astype(F32), g_ref[...])
    row_id = lax.broadcasted_iota(jnp.int32, (C, 1), 0)
    sub_row = lax.broadcasted_iota(jnp.int32, (SB, 1), 0)
    col_id = lax.broadcasted_iota(jnp.int32, (1, C), 1)
    blocks = []
    for i in range(C // SB):
        lo = i * SB
        qi = q[lo:lo + SB]
        gi = gc[lo:lo + SB]
        ki = k[lo:lo + SB]
        blk = jnp.zeros((SB, C), F32)
        for j in range(SB):
            d = jnp.where(sub_row >= j, gi - gi[j:j + 1], -jnp.inf)
            col = jnp.sum(qi * ki[j:j + 1] * jnp.exp(d), axis=-1, keepdims=True)
            blk = blk + col * (col_id == lo + j).astype(F32)
        if i > 0:
            ref = gc[lo - 1:lo]
            qt = qi * jnp.exp(gi - ref)
            kt = k * jnp.exp(jnp.where(row_id < lo, ref - gc, -jnp.inf))
            blk = blk + jnp.where(col_id < lo, _dot_nt(qt, kt), 0.0)
        blocks.append(blk)
    a = jnp.concatenate(blocks, axis=0)
    s0 = s_ref[...]
    o_ref[...] = _dot(a, v) + _dot_nt(q * jnp.exp(gc), s0)
    last = gc[C - 1:C]
    s_ref[...] = s0 * jnp.exp(last) + _dot_tn(v, k * jnp.exp(last - gc))


def gla_scan(q, k, gk, v):
    B, L, QK = q.shape
    dk = QK // GLA_HEADS
    dv = v.shape[-1] // GLA_HEADS
    C = GLA_CHUNK
    qk_spec = pl.BlockSpec((None, C, dk), lambda b, hd, c: (b, c, hd))
    v_spec = pl.BlockSpec((None, C, dv), lambda b, hd, c: (b, c, hd))
    return pl.pallas_call(
        functools.partial(_gla_kernel, scale=dk ** -0.5),
        grid=(B, GLA_HEADS, L // C),
        in_specs=[qk_spec, qk_spec, qk_spec, v_spec],
        out_specs=v_spec,
        out_shape=jax.ShapeDtypeStruct(v.shape, F32),
        scratch_shapes=[pltpu.VMEM((dv, dk), F32)],
        compiler_params=_cparams("parallel", "parallel", "arbitrary"),
        name="gla_scan",
    )(q, k, gk, v)


def gla_layer(h, nw, p):
    B, L, D = h.shape
    w_in = p['in_proj']
    lora = p['gate_up'].shape[0]
    qk = p['gate_up'].shape[1]
    vd = p['out_proj'].shape[0]
    n_main = 2 * qk + 2 * vd
    h2 = h.reshape(B * L, D)
    proj = norm_linear(h2, nw, w_in[:, :n_main].astype(BF16), tn=1024)
    w_g = jnp.pad(w_in[:, n_main:], ((0, 0), (0, 128 - lora)))
    glr = norm_linear(h2, nw, w_g.astype(BF16), tn=128)
    gate_up = jnp.pad(p['gate_up'], ((0, 128 - lora), (0, 0)))
    gpre = linear_res(glr, gate_up.astype(BF16),
                      jnp.broadcast_to(p['gate_bias'][None, :], (B * L, qk)))
    gk = (jax.nn.log_sigmoid(gpre) / GLA_TAU).reshape(B, L, qk)
    proj = proj.reshape(B, L, n_main)
    q, k, v, r = (proj[..., :qk], proj[..., qk:2 * qk], proj[..., 2 * qk:2 * qk + vd],
                  proj[..., 2 * qk + vd:])
    o = gla_scan(q, k, gk, v).reshape(B, L, GLA_HEADS, vd // GLA_HEADS)
    o = o * lax.rsqrt(jnp.mean(o * o, axis=-1, keepdims=True) + NORM_EPS) * p['norm_w']
    o = o.reshape(B, L, vd) * jax.nn.silu(r)
    return linear_res(o.reshape(B * L, vd), p['out_proj'].astype(BF16), h2).reshape(B, L, D)


def kernel(x, meta_tokens, norm_mix, norm_mlp, norm_final, mlp_w_in, mlp_w_out, rwkv_mu, rwkv_w_r, rwkv_w_k, rwkv_w_v, rwkv_w0, rwkv_w_lora_a, rwkv_w_lora_b, rwkv_a0, rwkv_a_lora_a, rwkv_a_lora_b, rwkv_g_lora_a, rwkv_g_lora_b, rwkv_k_k, rwkv_k_a, rwkv_r_k, rwkv_ln_w, rwkv_ln_b, rwkv_w_o, m2_in_proj, m2_conv_w, m2_conv_b, m2_dt_bias, m2_a_log, m2_d, m2_norm_w, m2_out_proj, gla_in_proj, gla_gate_up, gla_gate_bias, gla_norm_w, gla_out_proj, ret_in_proj, ret_out_proj):
    B, S, D = x.shape
    depth = norm_mix.shape[0]
    n_tok = N_META + S
    L = -(-n_tok // SEQ_ALIGN) * SEQ_ALIGN
    assert (B * L) % ROW_TILE == 0
    meta = jnp.broadcast_to(meta_tokens[None].astype(x.dtype), (B, N_META, D))
    h = jnp.concatenate([meta, x, jnp.zeros((B, L - n_tok, D), x.dtype)], axis=1)
    for i in range(depth):
        m, j = i % 4, i // 4
        if m == 0:
            h = rwkv_layer(h, norm_mix[i], dict(
                mu=rwkv_mu[j], w_r=rwkv_w_r[j], w_k=rwkv_w_k[j], w_v=rwkv_w_v[j], w0=rwkv_w0[j],
                w_lora_a=rwkv_w_lora_a[j], w_lora_b=rwkv_w_lora_b[j], a0=rwkv_a0[j],
                a_lora_a=rwkv_a_lora_a[j], a_lora_b=rwkv_a_lora_b[j],
                g_lora_a=rwkv_g_lora_a[j], g_lora_b=rwkv_g_lora_b[j],
                k_k=rwkv_k_k[j], k_a=rwkv_k_a[j], r_k=rwkv_r_k[j],
                ln_w=rwkv_ln_w[j], ln_b=rwkv_ln_b[j], w_o=rwkv_w_o[j]))
        elif m == 1:
            h = mamba2_layer(h, norm_mix[i], dict(
                in_proj=m2_in_proj[j], conv_w=m2_conv_w[j], conv_b=m2_conv_b[j],
                dt_bias=m2_dt_bias[j], a_log=m2_a_log[j], d=m2_d[j], norm_w=m2_norm_w[j],
                out_proj=m2_out_proj[j]))
        elif m == 2:
            h = gla_layer(h, norm_mix[i], dict(
                in_proj=gla_in_proj[j], gate_up=gla_gate_up[j], gate_bias=gla_gate_bias[j],
                norm_w=gla_norm_w[j], out_proj=gla_out_proj[j]))
        else:
            h = retnet_layer(h, norm_mix[i], dict(in_proj=ret_in_proj[j], out_proj=ret_out_proj[j]))
        h = mlp_res(h.reshape(B * L, D), norm_mlp[i], mlp_w_in[i].astype(BF16),
                    mlp_w_out[i].astype(BF16)).reshape(B, L, D)
    out = final_norm(h.reshape(B * L, D), norm_final).reshape(B, L, D)
    return out[:, N_META:n_tok]
```

```python
import functools
import math

import jax
import jax.numpy as jnp
from jax import lax
from jax.experimental import pallas as pl
from jax.experimental.pallas import tpu as pltpu

F32 = jnp.float32
BF16 = jnp.bfloat16

N_META = 16
NORM_EPS = 1e-5
SEQ_ALIGN = 128
MAX_ROW_TILE = 640
BF16_SUBLANES = 16
F32_SUBLANES = 8
VMEM_LIMIT = 56 * 1024 * 1024

RWKV_HEAD = 64
RWKV_GN_EPS = 64e-5
RWKV_CHUNK = 64

M2_HEAD = 64
M2_GROUPS = 8
M2_STATE = 128
M2_CONV = 4
M2_CHUNK = 128

GLA_HEADS = 4
GLA_TAU = 16.0
GLA_CHUNK = 64
GLA_SUB = 16

RET_HEADS = 4
RET_CHUNK = 128
ROPE_BASE = 10000.0


def _cparams(*sem):
    return pltpu.CompilerParams(dimension_semantics=sem, vmem_limit_bytes=VMEM_LIMIT)


def _row_tile(rows_per_batch):
    for t in range(min(rows_per_batch, MAX_ROW_TILE) // BF16_SUBLANES * BF16_SUBLANES, 0, -BF16_SUBLANES):
        if rows_per_batch % t == 0:
            return t
    raise ValueError(rows_per_batch)


def _dot(a, b):
    return jnp.dot(a.astype(BF16), b.astype(BF16), preferred_element_type=F32)


def _dot_nt(a, b):
    return lax.dot_general(a.astype(BF16), b.astype(BF16), (((1,), (1,)), ((), ())),
                           preferred_element_type=F32)


def _dot_tn(a, b):
    return lax.dot_general(a.astype(BF16), b.astype(BF16), (((0,), (0,)), ((), ())),
                           preferred_element_type=F32)


def _dot_f32(a, b):
    return jnp.dot(a, b, precision=lax.Precision.HIGHEST, preferred_element_type=F32)


def _rms_rows(x, w):
    return x * lax.rsqrt(jnp.mean(x * x, axis=-1, keepdims=True) + NORM_EPS) * w


def _tri(n, strict=False, upper=False):
    r = lax.broadcasted_iota(jnp.int32, (n, n), 0)
    c = lax.broadcasted_iota(jnp.int32, (n, n), 1)
    if upper:
        r, c = c, r
    return (r > c) if strict else (r >= c)


def _silu(x):
    return x * jax.nn.sigmoid(x)


def _norm_linear_kernel(h_ref, nw_ref, w_ref, o_ref, u_ref):
    @pl.when(pl.program_id(1) == 0)
    def _():
        u_ref[...] = _rms_rows(h_ref[...], nw_ref[...]).astype(BF16)

    o_ref[...] = jnp.dot(u_ref[...], w_ref[...], preferred_element_type=F32).astype(o_ref.dtype)


def norm_linear(h, nw, w, tm, tn, out_dtype):
    M, D = h.shape
    N = w.shape[1]
    return pl.pallas_call(
        _norm_linear_kernel,
        grid=(M // tm, N // tn),
        in_specs=[pl.BlockSpec((tm, D), lambda i, j: (i, 0)),
                  pl.BlockSpec((1, D), lambda i, j: (0, 0)),
                  pl.BlockSpec((D, tn), lambda i, j: (0, j))],
        out_specs=pl.BlockSpec((tm, tn), lambda i, j: (i, j)),
        out_shape=jax.ShapeDtypeStruct((M, N), out_dtype),
        scratch_shapes=[pltpu.VMEM((tm, D), BF16)],
        compiler_params=_cparams("parallel", "arbitrary"),
        name="norm_linear",
    )(h, nw.reshape(1, D), w)


def _linear_res_kernel(a_ref, w_ref, h_ref, o_ref):
    o_ref[...] = h_ref[...] + jnp.dot(a_ref[...], w_ref[...], preferred_element_type=F32)


def linear_res(a, w, h, tm):
    M, K = a.shape
    D = w.shape[1]
    return pl.pallas_call(
        _linear_res_kernel,
        grid=(M // tm,),
        in_specs=[pl.BlockSpec((tm, K), lambda i: (i, 0)),
                  pl.BlockSpec((K, D), lambda i: (0, 0)),
                  pl.BlockSpec((tm, D), lambda i: (i, 0))],
        out_specs=pl.BlockSpec((tm, D), lambda i: (i, 0)),
        out_shape=jax.ShapeDtypeStruct((M, D), F32),
        compiler_params=_cparams("parallel"),
        name="linear_res",
    )(a, w, h)


def _mlp_kernel(h_ref, nw_ref, w1_ref, w2_ref, o_ref, u_ref):
    f = pl.program_id(1)

    @pl.when(f == 0)
    def _():
        h = h_ref[...]
        u_ref[...] = _rms_rows(h, nw_ref[...]).astype(BF16)
        o_ref[...] = h

    hid = jnp.dot(u_ref[...], w1_ref[...], preferred_element_type=F32)
    hid = jnp.square(jnp.maximum(hid, 0.0))
    o_ref[...] += _dot(hid, w2_ref[...])


def mlp_res(h, nw, w1, w2, tm, tf=512):
    M, D = h.shape
    F = w1.shape[1]
    return pl.pallas_call(
        _mlp_kernel,
        grid=(M // tm, F // tf),
        in_specs=[pl.BlockSpec((tm, D), lambda i, f: (i, 0)),
                  pl.BlockSpec((1, D), lambda i, f: (0, 0)),
                  pl.BlockSpec((D, tf), lambda i, f: (0, f)),
                  pl.BlockSpec((tf, D), lambda i, f: (f, 0))],
        out_specs=pl.BlockSpec((tm, D), lambda i, f: (i, 0)),
        out_shape=jax.ShapeDtypeStruct((M, D), F32),
        scratch_shapes=[pltpu.VMEM((tm, D), BF16)],
        compiler_params=_cparams("parallel", "arbitrary"),
        name="mlp_res",
    )(h, nw.reshape(1, D), w1, w2)


def _final_norm_kernel(h_ref, nw_ref, o_ref):
    o_ref[...] = _rms_rows(h_ref[...], nw_ref[...])


def final_norm(h, nw, tm):
    M, D = h.shape
    return pl.pallas_call(
        _final_norm_kernel,
        grid=(M // tm,),
        in_specs=[pl.BlockSpec((tm, D), lambda i: (i, 0)),
                  pl.BlockSpec((1, D), lambda i: (0, 0))],
        out_specs=pl.BlockSpec((tm, D), lambda i: (i, 0)),
        out_shape=jax.ShapeDtypeStruct((M, D), F32),
        compiler_params=_cparams("parallel"),
        name="final_norm",
    )(h, nw.reshape(1, D))


def _rwkv_proj_kernel(h_ref, halo_ref, nw_ref, mu_ref, wr_ref, wk_ref, wv_ref,
                      wla_ref, wlb_ref, ala_ref, alb_ref, gla_ref, glb_ref,
                      w0_ref, a0_ref,
                      r_ref, k_ref, v_ref, lw_ref, a_ref, g_ref, *, tiles_per_batch):
    nw = nw_ref[...]
    u = _rms_rows(h_ref[...], nw)
    first = pl.program_id(0) % tiles_per_batch == 0
    prev_last = _rms_rows(halo_ref[F32_SUBLANES - 1:F32_SUBLANES, :], nw)
    prev_last = jnp.where(first, 0.0, prev_last)
    row = lax.broadcasted_iota(jnp.int32, (u.shape[0], 1), 0)
    xx = jnp.where(row == 0, prev_last, pltpu.roll(u, 1, axis=0)) - u
    mu = mu_ref[...]
    mix = lambda i: (u + xx * mu[i:i + 1, :]).astype(BF16)
    r_ref[...] = jnp.dot(mix(0), wr_ref[...], preferred_element_type=F32).astype(r_ref.dtype)
    k_ref[...] = jnp.dot(mix(2), wk_ref[...], preferred_element_type=F32).astype(k_ref.dtype)
    v_ref[...] = jnp.dot(mix(3), wv_ref[...], preferred_element_type=F32).astype(v_ref.dtype)
    hw = jnp.tanh(jnp.dot(mix(1), wla_ref[...], preferred_element_type=F32))
    w = -jax.nn.softplus(-(w0_ref[...] + _dot(hw, wlb_ref[...]))) - 0.5
    lw_ref[...] = -jnp.exp(w)
    ha = jnp.dot(mix(4), ala_ref[...], preferred_element_type=F32)
    a_ref[...] = jax.nn.sigmoid(a0_ref[...] + _dot(ha, alb_ref[...]))
    hg = jax.nn.sigmoid(jnp.dot(mix(5), gla_ref[...], preferred_element_type=F32))
    g_ref[...] = _dot(hg, glb_ref[...]).astype(g_ref.dtype)


def rwkv_proj(h, tm, tiles_per_batch, nw, mu, wr, wk, wv, wla, wlb, ala, alb, gla, glb, w0, a0):
    M, D = h.shape
    row = pl.BlockSpec((tm, D), lambda i: (i, 0))
    halo = pl.BlockSpec((F32_SUBLANES, D),
                        lambda i: (jnp.maximum(i * (tm // F32_SUBLANES) - 1, 0), 0))
    full = lambda x: pl.BlockSpec(x.shape, lambda i: (0, 0))
    consts = (nw.reshape(1, D), mu, wr, wk, wv, wla, wlb, ala, alb, gla, glb,
              w0.reshape(1, D), a0.reshape(1, D))
    sds = lambda dt: jax.ShapeDtypeStruct((M, D), dt)
    return pl.pallas_call(
        functools.partial(_rwkv_proj_kernel, tiles_per_batch=tiles_per_batch),
        grid=(M // tm,),
        in_specs=[row, halo] + [full(c) for c in consts],
        out_specs=[row] * 6,
        out_shape=[sds(BF16), sds(BF16), sds(BF16), sds(F32), sds(F32), sds(BF16)],
        compiler_params=_cparams("parallel"),
        name="rwkv_proj",
    )(h, h, *consts)


def _rwkv_scan_kernel(r_ref, k_ref, v_ref, lw_ref, a_ref, g_ref,
                      kk_ref, ka_ref, rk_ref, lnw_ref, lnb_ref, o_ref, s_ref):
    C = RWKV_CHUNK
    N = RWKV_HEAD

    @pl.when(pl.program_id(1) == 0)
    def _():
        s_ref[...] = jnp.zeros_like(s_ref)

    lw_all = lw_ref[...]
    cum_all = _dot_f32(_tri(C).astype(F32), lw_all)
    incl = _tri(C)
    strict = _tri(C, strict=True)
    mask2 = jnp.concatenate([strict, incl], axis=0)
    eye = (lax.broadcasted_iota(jnp.int32, (C, C), 0) ==
           lax.broadcasted_iota(jnp.int32, (C, C), 1)).astype(F32)
    heads = range(r_ref.shape[-1] // N)
    sls = [slice(hd * N, (hd + 1) * N) for hd in heads]
    s0s = [s_ref[hd] for hd in heads]
    pre = []
    for sl in sls:
        r = r_ref[:, sl].astype(F32)
        k = k_ref[:, sl].astype(F32)
        a = a_ref[:, sl]
        cum = cum_all[:, sl]
        kkr = k * kk_ref[:, sl]
        kk = kkr / jnp.maximum(jnp.sqrt(jnp.sum(kkr * kkr, axis=-1, keepdims=True)), 1e-12)
        km = k * (1.0 + (a - 1.0) * ka_ref[:, sl])
        beta = kk * a
        e_neg = jnp.exp(-cum)
        lhs = jnp.concatenate([-kk * jnp.exp(cum - lw_all[:, sl]), r * jnp.exp(cum)], axis=0)
        pre.append((r, km, beta, cum, lhs, beta * e_neg, km * e_neg))
    x_bs = [_dot_nt(p[4], p[5]) for p in pre]
    x_ks = [_dot_nt(p[4], p[6]) for p in pre]
    ps = [jnp.where(strict, x[:C], 0.0) for x in x_bs]
    ts = [eye + p for p in ps]
    for _ in range(int(math.log2(C)) - 1):
        ps = [_dot(p, p) for p in ps]
        ts = [t + _dot(t, p) for t, p in zip(ts, ps)]
    vs = [v_ref[:, sl] for sl in sls]
    hss = [_dot_nt(p[4], s0) + _dot(jnp.where(mask2, x, 0.0), v)
           for p, s0, x, v in zip(pre, s0s, x_ks, vs)]
    us = [_dot(t, hs[:C]) for t, hs in zip(ts, hss)]
    ys = [hs[C:] + _dot(jnp.where(incl, x[C:], 0.0), u) for hs, x, u in zip(hss, x_bs, us)]
    for hd in heads:
        r, km, beta, cum, _, _, _ = pre[hd]
        last = cum[C - 1:C, :]
        e_last = jnp.exp(last - cum)
        s_ref[hd] = s0s[hd] * jnp.exp(last) + _dot_tn(
            jnp.concatenate([vs[hd].astype(F32), us[hd]], axis=0),
            jnp.concatenate([km * e_last, beta * e_last], axis=0))
    outs = []
    for hd in heads:
        sl = sls[hd]
        r, km = pre[hd][0], pre[hd][1]
        y = ys[hd]
        mu_y = jnp.mean(y, axis=-1, keepdims=True)
        var = jnp.mean(jnp.square(y - mu_y), axis=-1, keepdims=True)
        yn = (y - mu_y) * lax.rsqrt(var + RWKV_GN_EPS) * lnw_ref[:, sl] + lnb_ref[:, sl]
        bonus = jnp.sum(r * km * rk_ref[:, sl], axis=-1, keepdims=True) * vs[hd].astype(F32)
        outs.append((yn + bonus) * g_ref[:, sl].astype(F32))
    o_ref[...] = jnp.concatenate(outs, axis=-1).astype(o_ref.dtype)


def rwkv_scan(r, k, v, lw, a, g, k_k, k_a, r_k, ln_w, ln_b):
    B, L, D = r.shape
    C = RWKV_CHUNK
    seq = pl.BlockSpec((None, C, D), lambda b, c: (b, c, 0))
    par = pl.BlockSpec((1, D), lambda b, c: (0, 0))
    return pl.pallas_call(
        _rwkv_scan_kernel,
        grid=(B, L // C),
        in_specs=[seq] * 6 + [par] * 5,
        out_specs=seq,
        out_shape=jax.ShapeDtypeStruct((B, L, D), BF16),
        scratch_shapes=[pltpu.VMEM((D // RWKV_HEAD, RWKV_HEAD, RWKV_HEAD), F32)],
        compiler_params=_cparams("parallel", "arbitrary"),
        name="rwkv_scan",
    )(r, k, v, lw, a, g, *(p.reshape(1, D) for p in (k_k, k_a, r_k, ln_w, ln_b)))


def rwkv_layer(h, nw, p):
    B, L, D = h.shape
    tm = _row_tile(L)
    bf = lambda x: x.astype(BF16)
    h2 = h.reshape(B * L, D)
    outs = rwkv_proj(h2, tm, L // tm, nw, p['mu'],
                     bf(p['w_r']), bf(p['w_k']), bf(p['w_v']),
                     bf(p['w_lora_a']), bf(p['w_lora_b']), bf(p['a_lora_a']), bf(p['a_lora_b']),
                     bf(p['g_lora_a']), bf(p['g_lora_b']), p['w0'], p['a0'])
    r, k, v, lw, a, g = (o.reshape(B, L, D) for o in outs)
    o = rwkv_scan(r, k, v, lw, a, g, p['k_k'], p['k_a'], p['r_k'].reshape(D), p['ln_w'], p['ln_b'])
    return linear_res(o.reshape(B * L, D), bf(p['w_o']), h2, tm).reshape(B, L, D)


def _sdecay_chunk(q, k, vs, ccs, crs, s_ref, P):
    C = q.shape[0]
    causal = _tri(C)
    scores = _dot_nt(q, k)
    s0 = s_ref[...]
    qs = _dot(q, s0)
    ys = []
    for r, (v, cc, cr) in enumerate(zip(vs, ccs, crs)):
        sl = slice(r * P, (r + 1) * P)
        dec = jnp.exp(jnp.where(causal, cc - cr, -jnp.inf))
        ys.append(_dot(scores * dec, v) + jnp.exp(cc) * qs[:, sl])
        last = cc[C - 1:C, :]
        s_ref[:, sl] = jnp.exp(last) * s0[:, sl] + _dot_tn(k, v * jnp.exp(last - cc))
    return ys


def _m2_scan_kernel(z_ref, xs_ref, xsh_ref, b_ref, bh_ref, c_ref, ch_ref,
                    wx_ref, wb_ref, wc_ref, bx_ref, bb_ref, bc_ref,
                    dtc_ref, dtr_ref, pc_ref, pr_ref, nw_ref, o_ref, s_ref, *, R, P, C):
    first = pl.program_id(2) == 0

    @pl.when(first)
    def _():
        s_ref[...] = jnp.zeros_like(s_ref)

    def conv_silu(x_ref, halo_ref, w_ref, bias_ref):
        x = x_ref[...].astype(F32)
        halo = jnp.where(first, 0.0, halo_ref[...].astype(F32))
        xf = jnp.concatenate([halo, x], axis=0)
        w = w_ref[...]
        acc = bias_ref[...] + w[M2_CONV - 1:M2_CONV] * x
        for i in range(M2_CONV - 1):
            off = BF16_SUBLANES - (M2_CONV - 1) + i
            acc = acc + w[i:i + 1] * xf[off:off + C]
        return _silu(acc)

    xs = conv_silu(xs_ref, xsh_ref, wx_ref, bx_ref)
    k = conv_silu(b_ref, bh_ref, wb_ref, bb_ref)
    q = conv_silu(c_ref, ch_ref, wc_ref, bc_ref)
    pc = pc_ref[...]
    pr = pr_ref[...]
    dt_c = jax.nn.softplus(dtc_ref[...] + pc[0:1])
    dt_r = jax.nn.softplus(dtr_ref[...] + pr[:, 0:1])
    cc = _dot_f32(_tri(C).astype(F32), dt_c * -jnp.exp(pc[1:2]))
    cr = _dot_f32(dt_r * -jnp.exp(pr[:, 1:2]), _tri(C, upper=True).astype(F32))
    xr = [xs[:, r * P:(r + 1) * P] for r in range(R)]
    ys = _sdecay_chunk(q, k, [xr[r] * dt_c[:, r:r + 1] for r in range(R)],
                       [cc[:, r:r + 1] for r in range(R)], [cr[r:r + 1] for r in range(R)], s_ref, P)
    y = jnp.concatenate([ys[r] + pc[2:3, r:r + 1] * xr[r] for r in range(R)], axis=-1)
    y = y * _silu(z_ref[...].astype(F32))
    y = y * lax.rsqrt(jnp.mean(y * y, axis=-1, keepdims=True) + NORM_EPS) * nw_ref[...]
    o_ref[...] = y.astype(o_ref.dtype)


def m2_scan(zx, dt, conv_w, conv_b, dt_bias, a_log, d_skip, norm_w, *, d_inner, heads):
    B, L, _ = zx.shape
    G, N, C = M2_GROUPS, M2_STATE, M2_CHUNK
    R = heads // G
    GP = d_inner // G
    HB = BF16_SUBLANES
    x0 = d_inner // GP
    b0 = 2 * d_inner // N
    c0 = b0 + G
    wb0 = d_inner // N
    wc0 = wb0 + G
    cur = lambda w, off: pl.BlockSpec((None, C, w), lambda b, g, c: (b, c, off + g))
    halo = lambda w, off: pl.BlockSpec(
        (None, HB, w), lambda b, g, c: (b, jnp.maximum(c * (C // HB) - 1, 0), off + g))
    par = lambda rows, w, off: pl.BlockSpec((rows, w), lambda b, g, c: (0, off + g))
    dt4 = dt.reshape(B, L, G, R)
    pcol = jnp.stack([dt_bias, a_log, d_skip]).reshape(3, G, R).transpose(1, 0, 2)
    prow = pcol.transpose(0, 2, 1)
    cb = conv_b.reshape(1, -1)
    return pl.pallas_call(
        functools.partial(_m2_scan_kernel, R=R, P=GP // R, C=C),
        grid=(B, G, L // C),
        in_specs=[cur(GP, 0), cur(GP, x0), halo(GP, x0), cur(N, b0), halo(N, b0), cur(N, c0), halo(N, c0),
                  par(M2_CONV, GP, 0), par(M2_CONV, N, wb0), par(M2_CONV, N, wc0),
                  par(1, GP, 0), par(1, N, wb0), par(1, N, wc0),
                  pl.BlockSpec((None, None, C, R), lambda b, g, c: (b, g, c, 0)),
                  pl.BlockSpec((None, None, R, C), lambda b, g, c: (b, g, 0, c)),
                  pl.BlockSpec((None, 3, R), lambda b, g, c: (g, 0, 0)),
                  pl.BlockSpec((None, R, 3), lambda b, g, c: (g, 0, 0)),
                  par(1, GP, 0)],
        out_specs=cur(GP, 0),
        out_shape=jax.ShapeDtypeStruct((B, L, d_inner), BF16),
        scratch_shapes=[pltpu.VMEM((N, GP), F32)],
        compiler_params=_cparams("parallel", "parallel", "arbitrary"),
        name="m2_scan",
    )(zx, zx, zx, zx, zx, zx, zx, conv_w, conv_w, conv_w, cb, cb, cb,
      dt4.transpose(0, 2, 1, 3), dt4.transpose(0, 2, 3, 1), pcol, prow, norm_w.reshape(1, -1))


def mamba2_layer(h, nw, p):
    B, L, D = h.shape
    tm = _row_tile(L)
    d_inner = p['norm_w'].shape[0]
    heads = p['a_log'].shape[0]
    w_in = p['in_proj']
    n_main = w_in.shape[1] - heads
    w_dt = jnp.pad(w_in[:, n_main:], ((0, 0), (0, 128 - heads)))
    h2 = h.reshape(B * L, D)
    zx = norm_linear(h2, nw, w_in[:, :n_main].astype(BF16), tm, 1024, BF16).reshape(B, L, n_main)
    dt = norm_linear(h2, nw, w_dt.astype(BF16), tm, 128, F32)[:, :heads].reshape(B, L, heads)
    y = m2_scan(zx, dt, p['conv_w'], p['conv_b'], p['dt_bias'], p['a_log'], p['d'], p['norm_w'],
                d_inner=d_inner, heads=heads)
    return linear_res(y.reshape(B * L, d_inner), p['out_proj'].astype(BF16), h2, tm).reshape(B, L, D)


def _ret_scan_kernel(q_ref, k_ref, v_ref, g_ref, cos_ref, sin_ref, lg_ref, o_ref, s_ref, *, C, scale):
    @pl.when(pl.program_id(2) == 0)
    def _():
        s_ref[...] = jnp.zeros_like(s_ref)

    cos = cos_ref[...]
    sin = sin_ref[...]
    half = cos.shape[-1]

    def rotary(x_ref):
        x1 = x_ref[:, :half].astype(F32)
        x2 = x_ref[:, half:].astype(F32)
        return jnp.concatenate([x1 * cos - x2 * sin, x1 * sin + x2 * cos], axis=-1)

    q = rotary(q_ref)
    k = rotary(k_ref) * scale
    lg = lg_ref[...]
    cc = (lax.broadcasted_iota(jnp.int32, (C, 1), 0) + 1).astype(F32) * lg
    cr = (lax.broadcasted_iota(jnp.int32, (1, C), 1) + 1).astype(F32) * lg
    y, = _sdecay_chunk(q, k, [v_ref[...].astype(F32)], [cc], [cr], s_ref, v_ref.shape[-1])
    mu = jnp.mean(y, axis=-1, keepdims=True)
    var = jnp.mean(jnp.square(y - mu), axis=-1, keepdims=True)
    yn = (y - mu) * lax.rsqrt(var + NORM_EPS)
    o_ref[...] = (_silu(g_ref[...].astype(F32)) * yn).astype(o_ref.dtype)


def ret_scan(proj, cos, sin, log_gamma, *, dk, dv):
    B, L, _ = proj.shape
    H, C = RET_HEADS, RET_CHUNK
    v0 = 2 * H * dk // dv
    qk = lambda off: pl.BlockSpec((None, C, dk), lambda b, hd, c: (b, c, off + hd))
    vg = lambda off: pl.BlockSpec((None, C, dv), lambda b, hd, c: (b, c, off + hd))
    tab = pl.BlockSpec((C, dk // 2), lambda b, hd, c: (c, 0))
    return pl.pallas_call(
        functools.partial(_ret_scan_kernel, C=C, scale=dk ** -0.5),
        grid=(B, H, L // C),
        in_specs=[qk(0), qk(H), vg(v0), vg(v0 + H), tab, tab,
                  pl.BlockSpec((None, 1, 1), lambda b, hd, c: (hd, 0, 0))],
        out_specs=vg(0),
        out_shape=jax.ShapeDtypeStruct((B, L, H * dv), BF16),
        scratch_shapes=[pltpu.VMEM((dk, dv), F32)],
        compiler_params=_cparams("parallel", "parallel", "arbitrary"),
        name="ret_scan",
    )(proj, proj, proj, proj, cos, sin, log_gamma.reshape(H, 1, 1))


def retnet_layer(h, nw, p):
    B, L, D = h.shape
    tm = _row_tile(L)
    w_in = p['in_proj']
    vd = p['out_proj'].shape[0]
    dv = vd // RET_HEADS
    dk = (w_in.shape[1] - 2 * vd) // (2 * RET_HEADS)
    h2 = h.reshape(B * L, D)
    proj = norm_linear(h2, nw, w_in.astype(BF16), tm, 1024, BF16).reshape(B, L, -1)
    half = dk // 2
    inv_freq = 1.0 / (ROPE_BASE ** jnp.linspace(0.0, 1.0, half, dtype=F32))
    ang = jnp.arange(L, dtype=F32)[:, None] * inv_freq[None, :]
    log_gamma = jnp.log1p(-jnp.exp2(-5.0 - jnp.arange(RET_HEADS, dtype=F32)))
    y = ret_scan(proj, jnp.cos(ang), jnp.sin(ang), log_gamma, dk=dk, dv=dv)
    return linear_res(y.reshape(B * L, vd), p['out_proj'].astype(BF16), h2, tm).reshape(B, L, D)


def _gla_kernel(q_ref, k_ref, v_ref, r_ref, glr_ref, gup_ref, gb_ref, nw_ref, o_ref, s_ref, *, scale):
    C = GLA_CHUNK
    SB = GLA_SUB

    @pl.when(pl.program_id(2) == 0)
    def _():
        s_ref[...] = jnp.zeros_like(s_ref)

    q = q_ref[...].astype(F32) * scale
    k = k_ref[...].astype(F32)
    v = v_ref[...]
    gate = jax.nn.log_sigmoid(_dot(glr_ref[...], gup_ref[...]) + gb_ref[...]) / GLA_TAU
    gc = _dot_f32(_tri(C).astype(F32), gate)
    row_id = lax.broadcasted_iota(jnp.int32, (C, 1), 0)
    sub_row = lax.broadcasted_iota(jnp.int32, (SB, 1), 0)
    col_id = lax.broadcasted_iota(jnp.int32, (1, C), 1)
    blocks = []
    for i in range(C // SB):
        lo = i * SB
        qi = q[lo:lo + SB]
        gi = gc[lo:lo + SB]
        ki = k[lo:lo + SB]
        blk = jnp.zeros((SB, C), F32)
        for j in range(SB):
            d = jnp.where(sub_row >= j, gi - gi[j:j + 1], -jnp.inf)
            col = jnp.sum(qi * ki[j:j + 1] * jnp.exp(d), axis=-1, keepdims=True)
            blk = blk + col * (col_id == lo + j).astype(F32)
        if i > 0:
            ref = gc[lo - 1:lo]
            qt = qi * jnp.exp(gi - ref)
            kt = k * jnp.exp(jnp.where(row_id < lo, ref - gc, -jnp.inf))
            blk = blk + jnp.where(col_id < lo, _dot_nt(qt, kt), 0.0)
        blocks.append(blk)
    a = jnp.concatenate(blocks, axis=0)
    s0 = s_ref[...]
    o = _dot(a, v) + _dot_nt(q * jnp.exp(gc), s0)
    last = gc[C - 1:C]
    s_ref[...] = s0 * jnp.exp(last) + _dot_tn(v, k * jnp.exp(last - gc))
    o = o * lax.rsqrt(jnp.mean(o * o, axis=-1, keepdims=True) + NORM_EPS) * nw_ref[...]
    o_ref[...] = (o * _silu(r_ref[...].astype(F32))).astype(o_ref.dtype)


def gla_scan(proj, gate_up, gate_bias, norm_w, *, dk, dv, lora_pad):
    B, L, _ = proj.shape
    H, C = GLA_HEADS, GLA_CHUNK
    v0 = 2 * H * dk // dv
    g0 = (2 * H * dk + 2 * H * dv) // lora_pad
    qk = lambda off: pl.BlockSpec((None, C, dk), lambda b, hd, c: (b, c, off + hd))
    vr = lambda off: pl.BlockSpec((None, C, dv), lambda b, hd, c: (b, c, off + hd))
    return pl.pallas_call(
        functools.partial(_gla_kernel, scale=dk ** -0.5),
        grid=(B, H, L // C),
        in_specs=[qk(0), qk(H), vr(v0), vr(v0 + H),
                  pl.BlockSpec((None, C, lora_pad), lambda b, hd, c: (b, c, g0)),
                  pl.BlockSpec((lora_pad, dk), lambda b, hd, c: (0, hd)),
                  pl.BlockSpec((1, dk), lambda b, hd, c: (0, hd)),
                  pl.BlockSpec((1, dv), lambda b, hd, c: (0, 0))],
        out_specs=vr(0),
        out_shape=jax.ShapeDtypeStruct((B, L, H * dv), BF16),
        scratch_shapes=[pltpu.VMEM((dv, dk), F32)],
        compiler_params=_cparams("parallel", "parallel", "arbitrary"),
        name="gla_scan",
    )(proj, proj, proj, proj, proj, gate_up, gate_bias.reshape(1, -1), norm_w.reshape(1, -1))


def gla_layer(h, nw, p):
    B, L, D = h.shape
    tm = _row_tile(L)
    lora, qk = p['gate_up'].shape
    vd = p['out_proj'].shape[0]
    lora_pad = 128
    w_in = jnp.pad(p['in_proj'], ((0, 0), (0, lora_pad - lora)))
    n_all = w_in.shape[1]
    gate_up = jnp.pad(p['gate_up'], ((0, lora_pad - lora), (0, 0))).astype(BF16)
    h2 = h.reshape(B * L, D)
    proj = norm_linear(h2, nw, w_in.astype(BF16), tm, n_all // 5, BF16).reshape(B, L, n_all)
    o = gla_scan(proj, gate_up, p['gate_bias'], p['norm_w'],
                 dk=qk // GLA_HEADS, dv=vd // GLA_HEADS, lora_pad=lora_pad)
    return linear_res(o.reshape(B * L, vd), p['out_proj'].astype(BF16), h2, tm).reshape(B, L, D)


def kernel(x, meta_tokens, norm_mix, norm_mlp, norm_final, mlp_w_in, mlp_w_out, rwkv_mu, rwkv_w_r, rwkv_w_k, rwkv_w_v, rwkv_w0, rwkv_w_lora_a, rwkv_w_lora_b, rwkv_a0, rwkv_a_lora_a, rwkv_a_lora_b, rwkv_g_lora_a, rwkv_g_lora_b, rwkv_k_k, rwkv_k_a, rwkv_r_k, rwkv_ln_w, rwkv_ln_b, rwkv_w_o, m2_in_proj, m2_conv_w, m2_conv_b, m2_dt_bias, m2_a_log, m2_d, m2_norm_w, m2_out_proj, gla_in_proj, gla_gate_up, gla_gate_bias, gla_norm_w, gla_out_proj, ret_in_proj, ret_out_proj):
    B, S, D = x.shape
    depth = norm_mix.shape[0]
    n_tok = N_META + S
    L = -(-n_tok // SEQ_ALIGN) * SEQ_ALIGN
    tm = _row_tile(L)
    meta = jnp.broadcast_to(meta_tokens[None].astype(x.dtype), (B, N_META, D))
    h = jnp.concatenate([meta, x, jnp.zeros((B, L - n_tok, D), x.dtype)], axis=1)
    for i in range(depth):
        m, j = i % 4, i // 4
        if m == 0:
            h = rwkv_layer(h, norm_mix[i], dict(
                mu=rwkv_mu[j], w_r=rwkv_w_r[j], w_k=rwkv_w_k[j], w_v=rwkv_w_v[j], w0=rwkv_w0[j],
                w_lora_a=rwkv_w_lora_a[j], w_lora_b=rwkv_w_lora_b[j], a0=rwkv_a0[j],
                a_lora_a=rwkv_a_lora_a[j], a_lora_b=rwkv_a_lora_b[j],
                g_lora_a=rwkv_g_lora_a[j], g_lora_b=rwkv_g_lora_b[j],
                k_k=rwkv_k_k[j], k_a=rwkv_k_a[j], r_k=rwkv_r_k[j],
                ln_w=rwkv_ln_w[j], ln_b=rwkv_ln_b[j], w_o=rwkv_w_o[j]))
        elif m == 1:
            h = mamba2_layer(h, norm_mix[i], dict(
                in_proj=m2_in_proj[j], conv_w=m2_conv_w[j], conv_b=m2_conv_b[j],
                dt_bias=m2_dt_bias[j], a_log=m2_a_log[j], d=m2_d[j], norm_w=m2_norm_w[j],
                out_proj=m2_out_proj[j]))
        elif m == 2:
            h = gla_layer(h, norm_mix[i], dict(
                in_proj=gla_in_proj[j], gate_up=gla_gate_up[j], gate_bias=gla_gate_bias[j],
                norm_w=gla_norm_w[j], out_proj=gla_out_proj[j]))
        else:
            h = retnet_layer(h, norm_mix[i], dict(in_proj=ret_in_proj[j], out_proj=ret_out_proj[j]))
        h = mlp_res(h.reshape(B * L, D), norm_mlp[i], mlp_w_in[i].astype(BF16),
                    mlp_w_out[i].astype(BF16), tm).reshape(B, L, D)
    out = final_norm(h.reshape(B * L, D), norm_final, tm).reshape(B, L, D)
    return out[:, N_META:n_tok]
```

```python
import functools
import math

import jax
import jax.numpy as jnp
from jax import lax
from jax.experimental import pallas as pl
from jax.experimental.pallas import tpu as pltpu

F32 = jnp.float32
BF16 = jnp.bfloat16

N_META = 16
NORM_EPS = 1e-5
SEQ_ALIGN = 128
MAX_ROW_TILE = 640
MAX_WIDE_ROW_TILE = 1152
BF16_SUBLANES = 16
F32_SUBLANES = 8
VMEM_LIMIT = 56 * 1024 * 1024

RWKV_HEAD = 64
RWKV_GN_EPS = 64e-5
RWKV_CHUNK = 64

M2_HEAD = 64
M2_GROUPS = 8
M2_STATE = 128
M2_CONV = 4
M2_CHUNK = 128
M2_GROUPS_PER_STEP = 4

GLA_HEADS = 4
GLA_TAU = 16.0
GLA_CHUNK = 64
GLA_SUB = 16

RET_HEADS = 4
RET_CHUNK = 128
ROPE_BASE = 10000.0


def _cparams(*sem):
    return pltpu.CompilerParams(dimension_semantics=sem, vmem_limit_bytes=VMEM_LIMIT)


def _row_tile(rows, limit=MAX_ROW_TILE):
    for t in range(min(rows, limit) // BF16_SUBLANES * BF16_SUBLANES, 0, -BF16_SUBLANES):
        if rows % t == 0:
            return t
    raise ValueError(rows)


def _dot(a, b):
    return jnp.dot(a.astype(BF16), b.astype(BF16), preferred_element_type=F32)


def _dot_nt(a, b):
    return lax.dot_general(a.astype(BF16), b.astype(BF16), (((1,), (1,)), ((), ())),
                           preferred_element_type=F32)


def _dot_tn(a, b):
    return lax.dot_general(a.astype(BF16), b.astype(BF16), (((0,), (0,)), ((), ())),
                           preferred_element_type=F32)


def _dot_f32(a, b):
    return jnp.dot(a, b, precision=lax.Precision.HIGHEST, preferred_element_type=F32)


def _rms_rows(x, w):
    return x * lax.rsqrt(jnp.mean(x * x, axis=-1, keepdims=True) + NORM_EPS) * w


def _tri(n, strict=False, upper=False):
    r = lax.broadcasted_iota(jnp.int32, (n, n), 0)
    c = lax.broadcasted_iota(jnp.int32, (n, n), 1)
    if upper:
        r, c = c, r
    return (r > c) if strict else (r >= c)


def _silu(x):
    return x * jax.nn.sigmoid(x)


def _tree_sum(xs):
    while len(xs) > 1:
        xs = [xs[i] + xs[i + 1] if i + 1 < len(xs) else xs[i] for i in range(0, len(xs), 2)]
    return xs[0]


def _norm_linear_kernel(h_ref, nw_ref, w_ref, o_ref, u_ref):
    @pl.when(pl.program_id(1) == 0)
    def _():
        u_ref[...] = _rms_rows(h_ref[...], nw_ref[...]).astype(BF16)

    o_ref[...] = jnp.dot(u_ref[...], w_ref[...], preferred_element_type=F32).astype(o_ref.dtype)


def norm_linear(h, nw, w, tn, out_dtype):
    M, D = h.shape
    N = w.shape[1]
    tm = _row_tile(M, MAX_WIDE_ROW_TILE)
    return pl.pallas_call(
        _norm_linear_kernel,
        grid=(M // tm, N // tn),
        in_specs=[pl.BlockSpec((tm, D), lambda i, j: (i, 0)),
                  pl.BlockSpec((1, D), lambda i, j: (0, 0)),
                  pl.BlockSpec((D, tn), lambda i, j: (0, j))],
        out_specs=pl.BlockSpec((tm, tn), lambda i, j: (i, j)),
        out_shape=jax.ShapeDtypeStruct((M, N), out_dtype),
        scratch_shapes=[pltpu.VMEM((tm, D), BF16)],
        compiler_params=_cparams("parallel", "arbitrary"),
        name="norm_linear",
    )(h, nw.reshape(1, D), w)


def _linear_res_kernel(a_ref, w_ref, h_ref, o_ref):
    o_ref[...] = h_ref[...] + jnp.dot(a_ref[...], w_ref[...], preferred_element_type=F32)


def linear_res(a, w, h):
    M, K = a.shape
    D = w.shape[1]
    tm = _row_tile(M, MAX_WIDE_ROW_TILE)
    return pl.pallas_call(
        _linear_res_kernel,
        grid=(M // tm,),
        in_specs=[pl.BlockSpec((tm, K), lambda i: (i, 0)),
                  pl.BlockSpec((K, D), lambda i: (0, 0)),
                  pl.BlockSpec((tm, D), lambda i: (i, 0))],
        out_specs=pl.BlockSpec((tm, D), lambda i: (i, 0)),
        out_shape=jax.ShapeDtypeStruct((M, D), F32),
        compiler_params=_cparams("parallel"),
        name="linear_res",
    )(a, w, h)


def _mlp_kernel(h_ref, nw_ref, w1_ref, w2_ref, o_ref, u_ref):
    f = pl.program_id(1)

    @pl.when(f == 0)
    def _():
        h = h_ref[...]
        u_ref[...] = _rms_rows(h, nw_ref[...]).astype(BF16)
        o_ref[...] = h

    hid = jnp.dot(u_ref[...], w1_ref[...], preferred_element_type=F32)
    hid = jnp.square(jnp.maximum(hid, 0.0))
    o_ref[...] += _dot(hid, w2_ref[...])


def mlp_res(h, nw, w1, w2, tf=1024):
    M, D = h.shape
    F = w1.shape[1]
    tm = _row_tile(M, MAX_WIDE_ROW_TILE)
    return pl.pallas_call(
        _mlp_kernel,
        grid=(M // tm, F // tf),
        in_specs=[pl.BlockSpec((tm, D), lambda i, f: (i, 0)),
                  pl.BlockSpec((1, D), lambda i, f: (0, 0)),
                  pl.BlockSpec((D, tf), lambda i, f: (0, f)),
                  pl.BlockSpec((tf, D), lambda i, f: (f, 0))],
        out_specs=pl.BlockSpec((tm, D), lambda i, f: (i, 0)),
        out_shape=jax.ShapeDtypeStruct((M, D), F32),
        scratch_shapes=[pltpu.VMEM((tm, D), BF16)],
        compiler_params=_cparams("parallel", "arbitrary"),
        name="mlp_res",
    )(h, nw.reshape(1, D), w1, w2)


def _final_norm_kernel(h_ref, nw_ref, o_ref):
    o_ref[...] = _rms_rows(h_ref[...], nw_ref[...])


def final_norm(h, nw):
    M, D = h.shape
    tm = _row_tile(M, MAX_WIDE_ROW_TILE)
    return pl.pallas_call(
        _final_norm_kernel,
        grid=(M // tm,),
        in_specs=[pl.BlockSpec((tm, D), lambda i: (i, 0)),
                  pl.BlockSpec((1, D), lambda i: (0, 0))],
        out_specs=pl.BlockSpec((tm, D), lambda i: (i, 0)),
        out_shape=jax.ShapeDtypeStruct((M, D), F32),
        compiler_params=_cparams("parallel"),
        name="final_norm",
    )(h, nw.reshape(1, D))


def _rwkv_proj_kernel(h_ref, halo_ref, nw_ref, mu_ref, wr_ref, wk_ref, wv_ref,
                      wla_ref, wlb_ref, ala_ref, alb_ref, gla_ref, glb_ref,
                      w0_ref, a0_ref,
                      r_ref, k_ref, v_ref, lw_ref, a_ref, g_ref, *, tiles_per_batch):
    nw = nw_ref[...]
    u = _rms_rows(h_ref[...], nw)
    first = pl.program_id(0) % tiles_per_batch == 0
    prev_last = _rms_rows(halo_ref[F32_SUBLANES - 1:F32_SUBLANES, :], nw)
    prev_last = jnp.where(first, 0.0, prev_last)
    row = lax.broadcasted_iota(jnp.int32, (u.shape[0], 1), 0)
    xx = jnp.where(row == 0, prev_last, pltpu.roll(u, 1, axis=0)) - u
    mu = mu_ref[...]
    mix = lambda i: (u + xx * mu[i:i + 1, :]).astype(BF16)
    r_ref[...] = jnp.dot(mix(0), wr_ref[...], preferred_element_type=F32).astype(r_ref.dtype)
    k_ref[...] = jnp.dot(mix(2), wk_ref[...], preferred_element_type=F32).astype(k_ref.dtype)
    v_ref[...] = jnp.dot(mix(3), wv_ref[...], preferred_element_type=F32).astype(v_ref.dtype)
    hw = jnp.tanh(jnp.dot(mix(1), wla_ref[...], preferred_element_type=F32))
    w = -jax.nn.softplus(-(w0_ref[...] + _dot(hw, wlb_ref[...]))) - 0.5
    lw_ref[...] = -jnp.exp(w)
    ha = jnp.dot(mix(4), ala_ref[...], preferred_element_type=F32)
    a_ref[...] = jax.nn.sigmoid(a0_ref[...] + _dot(ha, alb_ref[...]))
    hg = jax.nn.sigmoid(jnp.dot(mix(5), gla_ref[...], preferred_element_type=F32))
    g_ref[...] = _dot(hg, glb_ref[...]).astype(g_ref.dtype)


def rwkv_proj(h, rows_per_batch, nw, mu, wr, wk, wv, wla, wlb, ala, alb, gla, glb, w0, a0):
    M, D = h.shape
    tm = _row_tile(rows_per_batch)
    row = pl.BlockSpec((tm, D), lambda i: (i, 0))
    halo = pl.BlockSpec((F32_SUBLANES, D),
                        lambda i: (jnp.maximum(i * (tm // F32_SUBLANES) - 1, 0), 0))
    full = lambda x: pl.BlockSpec(x.shape, lambda i: (0, 0))
    consts = (nw.reshape(1, D), mu, wr, wk, wv, wla, wlb, ala, alb, gla, glb,
              w0.reshape(1, D), a0.reshape(1, D))
    sds = lambda dt: jax.ShapeDtypeStruct((M, D), dt)
    return pl.pallas_call(
        functools.partial(_rwkv_proj_kernel, tiles_per_batch=rows_per_batch // tm),
        grid=(M // tm,),
        in_specs=[row, halo] + [full(c) for c in consts],
        out_specs=[row] * 6,
        out_shape=[sds(BF16), sds(BF16), sds(BF16), sds(F32), sds(F32), sds(BF16)],
        compiler_params=_cparams("parallel"),
        name="rwkv_proj",
    )(h, h, *consts)


def _rwkv_scan_kernel(r_ref, k_ref, v_ref, lw_ref, a_ref, g_ref,
                      kk_ref, ka_ref, rk_ref, lnw_ref, lnb_ref, o_ref, s_ref):
    C = RWKV_CHUNK
    N = RWKV_HEAD
    assert C == N

    @pl.when(pl.program_id(1) == 0)
    def _():
        s_ref[...] = jnp.zeros_like(s_ref)

    W = 2 * N
    lw_all = lw_ref[...]
    cum_all = _dot_f32(_tri(C).astype(F32), lw_all)
    lane = lax.broadcasted_iota(jnp.int32, (1, W), 1)
    left = lane < N
    row_c = lax.broadcasted_iota(jnp.int32, (C, 1), 0)
    strict = row_c > (lane & (N - 1))
    incl = row_c >= (lane & (N - 1))
    mask2 = jnp.concatenate([strict, incl], axis=0)
    row_w = lax.broadcasted_iota(jnp.int32, (W, 1), 0)
    bdiag = (row_w < N) == left
    eye = (row_w == lane).astype(F32)
    split = lambda x: jnp.concatenate([jnp.where(left, x, 0.0), jnp.where(left, 0.0, x)], axis=0)

    def head_sum(x):
        sa = jnp.sum(jnp.where(left, x, 0.0), axis=-1, keepdims=True)
        sb = jnp.sum(jnp.where(left, 0.0, x), axis=-1, keepdims=True)
        return jnp.where(left, sa, sb)

    pairs = range(r_ref.shape[-1] // W)
    sls = [slice(p * W, (p + 1) * W) for p in pairs]
    s0s = [s_ref[p] for p in pairs]
    pre = []
    for sl in sls:
        r = r_ref[:, sl].astype(F32)
        k = k_ref[:, sl].astype(F32)
        a = a_ref[:, sl]
        cum = cum_all[:, sl]
        kkr = k * kk_ref[:, sl]
        kk = kkr / jnp.maximum(jnp.sqrt(head_sum(kkr * kkr)), 1e-12)
        km = k * (1.0 + (a - 1.0) * ka_ref[:, sl])
        beta = kk * a
        e_neg = jnp.exp(-cum)
        lhs = jnp.concatenate([-kk * jnp.exp(cum - lw_all[:, sl]), r * jnp.exp(cum)], axis=0)
        pre.append((r, km, beta, cum, lhs, beta * e_neg, km * e_neg))
    x_as = [_dot_nt(jnp.where(left, p[4], 0.0), jnp.concatenate([p[5], p[6]], axis=0)) for p in pre]
    x_bs = [_dot_nt(jnp.where(left, 0.0, p[4]), jnp.concatenate([p[6], p[5]], axis=0)) for p in pre]
    xbeta = [jnp.where(left, xa, xb) for xa, xb in zip(x_as, x_bs)]
    xkey = [jnp.where(mask2, jnp.where(left, xb, xa), 0.0) for xa, xb in zip(x_as, x_bs)]
    ps = [split(jnp.where(strict, x[:C], 0.0)) for x in xbeta]
    ts = [eye + p for p in ps]
    n_fac = int(math.log2(C))
    for i in range(1, n_fac):
        if i == 1:
            ps = [_dot(p, p) for p in ps]
        zs = [_dot(p, jnp.concatenate([p, t], axis=-1)) if i < n_fac - 1 else _dot(p, t)
              for p, t in zip(ps, ts)]
        ts = [t + z[:, -W:] for t, z in zip(ts, zs)]
        ps = [z[:, :W] for z in zs]
    vs = [v_ref[:, sl].astype(F32) for sl in sls]
    vx = [jnp.concatenate([jnp.where(left, 0.0, v), jnp.where(left, v, 0.0)], axis=0) for v in vs]
    hss = [_dot_nt(p[4], s0) + _dot(xk, v) for p, s0, xk, v in zip(pre, s0s, xkey, vx)]
    ubd = [_dot(t, split(hs[:C])) for t, hs in zip(ts, hss)]
    ys = [hs[C:] + _dot(jnp.where(incl, x[C:], 0.0), u) for hs, x, u in zip(hss, xbeta, ubd)]
    for p in pairs:
        r, km, beta, cum, _, _, _ = pre[p]
        last = cum[C - 1:C, :]
        e_last = jnp.exp(last - cum)
        upd = _dot_tn(jnp.concatenate([vs[p], ubd[p][:C] + ubd[p][C:]], axis=0),
                      jnp.concatenate([km * e_last, beta * e_last], axis=0))
        s_ref[p] = s0s[p] * jnp.exp(last) + jnp.where(bdiag, upd, 0.0)
    outs = []
    for p in pairs:
        sl = sls[p]
        r, km = pre[p][0], pre[p][1]
        y = ys[p]
        mu_y = head_sum(y) * (1.0 / N)
        var = head_sum(jnp.square(y - mu_y)) * (1.0 / N)
        yn = (y - mu_y) * lax.rsqrt(var + RWKV_GN_EPS) * lnw_ref[:, sl] + lnb_ref[:, sl]
        bonus = head_sum(r * km * rk_ref[:, sl]) * vs[p]
        outs.append((yn + bonus) * g_ref[:, sl].astype(F32))
    o_ref[...] = jnp.concatenate(outs, axis=-1).astype(o_ref.dtype)


def rwkv_scan(r, k, v, lw, a, g, k_k, k_a, r_k, ln_w, ln_b):
    B, L, D = r.shape
    C = RWKV_CHUNK
    seq = pl.BlockSpec((None, C, D), lambda b, c: (b, c, 0))
    par = pl.BlockSpec((1, D), lambda b, c: (0, 0))
    return pl.pallas_call(
        _rwkv_scan_kernel,
        grid=(B, L // C),
        in_specs=[seq] * 6 + [par] * 5,
        out_specs=seq,
        out_shape=jax.ShapeDtypeStruct((B, L, D), BF16),
        scratch_shapes=[pltpu.VMEM((D // (2 * RWKV_HEAD), 2 * RWKV_HEAD, 2 * RWKV_HEAD), F32)],
        compiler_params=_cparams("parallel", "arbitrary"),
        name="rwkv_scan",
    )(r, k, v, lw, a, g, *(p.reshape(1, D) for p in (k_k, k_a, r_k, ln_w, ln_b)))


def rwkv_layer(h, nw, p):
    B, L, D = h.shape
    bf = lambda x: x.astype(BF16)
    h2 = h.reshape(B * L, D)
    outs = rwkv_proj(h2, L, nw, p['mu'],
                     bf(p['w_r']), bf(p['w_k']), bf(p['w_v']),
                     bf(p['w_lora_a']), bf(p['w_lora_b']), bf(p['a_lora_a']), bf(p['a_lora_b']),
                     bf(p['g_lora_a']), bf(p['g_lora_b']), p['w0'], p['a0'])
    r, k, v, lw, a, g = (o.reshape(B, L, D) for o in outs)
    o = rwkv_scan(r, k, v, lw, a, g, p['k_k'], p['k_a'], p['r_k'].reshape(D), p['ln_w'], p['ln_b'])
    return linear_res(o.reshape(B * L, D), bf(p['w_o']), h2).reshape(B, L, D)


def _sdecay_groups(qs, ks, vss, ccss, crss, s_ref, P):
    C = qs[0].shape[0]
    causal = _tri(C)
    groups = range(len(qs))
    s0s = [s_ref[g] for g in groups]
    scores = [_dot_nt(qs[g], ks[g]) for g in groups]
    qss = [_dot(qs[g], s0s[g]) for g in groups]
    yss = []
    for g in groups:
        ys = []
        for r, (v, cc, cr) in enumerate(zip(vss[g], ccss[g], crss[g])):
            dec = jnp.exp(jnp.where(causal, cc - cr, -jnp.inf))
            ys.append(_dot(scores[g] * dec, v) + jnp.exp(cc) * qss[g][:, r * P:(r + 1) * P])
        yss.append(ys)
    for g in groups:
        lasts = [cc[C - 1:C, :] for cc in ccss[g]]
        vw = jnp.concatenate([v * jnp.exp(last - cc)
                              for v, cc, last in zip(vss[g], ccss[g], lasts)], axis=-1)
        keep = jnp.concatenate([jnp.broadcast_to(jnp.exp(last), (1, P)) for last in lasts], axis=-1)
        s_ref[g] = keep * s0s[g] + _dot_tn(ks[g], vw)
    return yss


def _m2_scan_kernel(shift_ref, expand_ref, z_ref, xs_ref, xsh_ref, b_ref, bh_ref, c_ref, ch_ref,
                    wx_ref, wb_ref, wc_ref, bx_ref, bb_ref, bc_ref,
                    dtc_ref, dtr_ref, pc_ref, pr_ref, dx_ref, nw_ref, o_ref, s_ref, *, GS, R, P, N, C):
    first = pl.program_id(2) == 0

    @pl.when(first)
    def _():
        s_ref[...] = jnp.zeros_like(s_ref)

    shift = shift_ref[...]

    def conv_silu(x_ref, prev_ref, w_ref, bias_ref):
        x = x_ref[...]
        prev = prev_ref[...]
        xf = jnp.concatenate([jnp.where(first, jnp.zeros_like(prev), prev), x], axis=0)
        taps = jnp.dot(shift, xf, preferred_element_type=F32)
        w = w_ref[...]
        acc = bias_ref[...] + w[M2_CONV - 1:M2_CONV] * x.astype(F32)
        for i in range(M2_CONV - 1):
            acc = acc + w[i:i + 1] * taps[i * C:(i + 1) * C]
        return _silu(acc)

    GP = R * P
    xs = conv_silu(xs_ref, xsh_ref, wx_ref, bx_ref)
    k = conv_silu(b_ref, bh_ref, wb_ref, bb_ref)
    q = conv_silu(c_ref, ch_ref, wc_ref, bc_ref)
    pc = pc_ref[...]
    pr = pr_ref[...]
    dt_c = jax.nn.softplus(dtc_ref[...] + pc[0:1])
    dt_r = jax.nn.softplus(dtr_ref[...] + pr[:, 0:1])
    cc = _dot_f32(_tri(C).astype(F32), dt_c * -jnp.exp(pc[1:2]))
    cr = _dot_f32(dt_r * -jnp.exp(pr[:, 1:2]), _tri(C, upper=True).astype(F32))
    expand = expand_ref[...]

    def widen(x):
        hi = x.astype(BF16)
        r1 = x - hi.astype(F32)
        mid = r1.astype(BF16)
        lo = (r1 - mid.astype(F32)).astype(BF16)
        return jnp.dot(jnp.concatenate([hi, mid, lo], axis=-1), expand, preferred_element_type=F32)

    last = cc[C - 1:C]
    e_cum = widen(jnp.exp(cc))
    v = xs * widen(dt_c)
    vw = v * widen(jnp.exp(last - cc))
    keep = e_cum[C - 1:C]
    causal = _tri(C)
    head_of_lane = lax.broadcasted_iota(jnp.int32, (1, GP), 1) // P
    groups = range(GS)
    gsl = [slice(g * GP, (g + 1) * GP) for g in groups]
    qg = [q[:, g * N:(g + 1) * N] for g in groups]
    kg = [k[:, g * N:(g + 1) * N] for g in groups]
    s0s = [s_ref[g] for g in groups]
    scores = [_dot_nt(qg[g], kg[g]) for g in groups]
    qss = [_dot(qg[g], s0s[g]) for g in groups]
    pvs = []
    for g in groups:
        ps = [scores[g] * jnp.exp(jnp.where(causal, cc[:, j:j + 1] - cr[j:j + 1], -jnp.inf))
              for j in range(g * R, (g + 1) * R)]
        vbd = jnp.concatenate([jnp.where(head_of_lane == r, v[:, gsl[g]], 0.0) for r in range(R)], axis=0)
        pvs.append(_dot(jnp.concatenate(ps, axis=-1), vbd))
    for g in groups:
        s_ref[g] = keep[:, gsl[g]] * s0s[g] + _dot_tn(kg[g], vw[:, gsl[g]])
    z = z_ref[...].astype(F32)
    dx = dx_ref[...]
    outs = []
    for g in groups:
        y = pvs[g] + e_cum[:, gsl[g]] * qss[g] + dx[:, gsl[g]] * xs[:, gsl[g]]
        y = y * _silu(z[:, gsl[g]])
        outs.append(y * lax.rsqrt(jnp.mean(y * y, axis=-1, keepdims=True) + NORM_EPS))
    o_ref[...] = (jnp.concatenate(outs, axis=-1) * nw_ref[...]).astype(o_ref.dtype)


def m2_scan(zx, dt, conv_w, conv_b, dt_bias, a_log, d_skip, norm_w, *, d_inner, heads):
    B, L, _ = zx.shape
    G, N, C, GS = M2_GROUPS, M2_STATE, M2_CHUNK, M2_GROUPS_PER_STEP
    R = heads // G
    GP = d_inner // G
    NS = G // GS
    WX, WN, WR = GS * GP, GS * N, GS * R
    x0 = d_inner // WX
    b0 = 2 * d_inner // WN
    c0 = b0 + NS
    wb0 = d_inner // WN
    wc0 = wb0 + NS
    cur = lambda w, off: pl.BlockSpec((None, C, w), lambda b, g, c: (b, c, off + g))
    halo = lambda w, off: pl.BlockSpec((None, C, w), lambda b, g, c: (b, jnp.maximum(c - 1, 0), off + g))
    par = lambda rows, w, off: pl.BlockSpec((rows, w), lambda b, g, c: (0, off + g))
    tap_row = jnp.arange((M2_CONV - 1) * C)
    shift = (jnp.arange(2 * C)[None, :] ==
             (C + tap_row % C - (M2_CONV - 1) + tap_row // C)[:, None]).astype(BF16)
    dt4 = dt.reshape(B, L, NS, WR)
    pcol = jnp.stack([dt_bias, a_log]).reshape(2, NS, WR).transpose(1, 0, 2)
    prow = pcol.transpose(0, 2, 1)
    expand = jnp.tile(jnp.arange(WX)[None, :] // (GP // R) == jnp.arange(WR)[:, None], (3, 1)).astype(BF16)
    cb = conv_b.reshape(1, -1)
    const = lambda x: pl.BlockSpec(x.shape, lambda b, g, c: (0, 0))
    return pl.pallas_call(
        functools.partial(_m2_scan_kernel, GS=GS, R=R, P=GP // R, N=N, C=C),
        grid=(B, NS, L // C),
        in_specs=[const(shift), const(expand),
                  cur(WX, 0), cur(WX, x0), halo(WX, x0), cur(WN, b0), halo(WN, b0), cur(WN, c0), halo(WN, c0),
                  par(M2_CONV, WX, 0), par(M2_CONV, WN, wb0), par(M2_CONV, WN, wc0),
                  par(1, WX, 0), par(1, WN, wb0), par(1, WN, wc0),
                  pl.BlockSpec((None, None, C, WR), lambda b, g, c: (b, g, c, 0)),
                  pl.BlockSpec((None, None, WR, C), lambda b, g, c: (b, g, 0, c)),
                  pl.BlockSpec((None, 2, WR), lambda b, g, c: (g, 0, 0)),
                  pl.BlockSpec((None, WR, 2), lambda b, g, c: (g, 0, 0)),
                  par(1, WX, 0), par(1, WX, 0)],
        out_specs=cur(WX, 0),
        out_shape=jax.ShapeDtypeStruct((B, L, d_inner), BF16),
        scratch_shapes=[pltpu.VMEM((GS, N, GP), F32)],
        compiler_params=_cparams("parallel", "parallel", "arbitrary"),
        name="m2_scan",
    )(shift, expand, zx, zx, zx, zx, zx, zx, zx, conv_w, conv_w, conv_w, cb, cb, cb,
      dt4.transpose(0, 2, 1, 3), dt4.transpose(0, 2, 3, 1), pcol, prow,
      jnp.repeat(d_skip, GP // R).reshape(1, -1), norm_w.reshape(1, -1))


def mamba2_layer(h, nw, p):
    B, L, D = h.shape
    d_inner = p['norm_w'].shape[0]
    heads = p['a_log'].shape[0]
    w_in = p['in_proj']
    n_main = w_in.shape[1] - heads
    w_dt = jnp.pad(w_in[:, n_main:], ((0, 0), (0, 128 - heads)))
    h2 = h.reshape(B * L, D)
    zx = norm_linear(h2, nw, w_in[:, :n_main].astype(BF16), 1024, BF16).reshape(B, L, n_main)
    dt = norm_linear(h2, nw, w_dt.astype(BF16), 128, F32)[:, :heads].reshape(B, L, heads)
    y = m2_scan(zx, dt, p['conv_w'], p['conv_b'], p['dt_bias'], p['a_log'], p['d'], p['norm_w'],
                d_inner=d_inner, heads=heads)
    return linear_res(y.reshape(B * L, d_inner), p['out_proj'].astype(BF16), h2).reshape(B, L, D)


def _ret_scan_kernel(q_ref, k_ref, v_ref, g_ref, cos_ref, sin_ref, lg_ref, o_ref, s_ref,
                     *, H, dk, dv, C, scale):
    @pl.when(pl.program_id(1) == 0)
    def _():
        s_ref[...] = jnp.zeros_like(s_ref)

    cos = cos_ref[...]
    sin = sin_ref[...]
    half = dk // 2

    def rotary(x_ref, hd):
        x1 = x_ref[:, hd * dk:hd * dk + half].astype(F32)
        x2 = x_ref[:, hd * dk + half:(hd + 1) * dk].astype(F32)
        return jnp.concatenate([x1 * cos - x2 * sin, x1 * sin + x2 * cos], axis=-1)

    lg = lg_ref[...]
    steps_c = (lax.broadcasted_iota(jnp.int32, (C, 1), 0) + 1).astype(F32)
    steps_r = (lax.broadcasted_iota(jnp.int32, (1, C), 1) + 1).astype(F32)
    yss = _sdecay_groups(
        [rotary(q_ref, hd) for hd in range(H)], [rotary(k_ref, hd) * scale for hd in range(H)],
        [[v_ref[:, hd * dv:(hd + 1) * dv]] for hd in range(H)],
        [[steps_c * lg[:, hd:hd + 1]] for hd in range(H)],
        [[steps_r * lg[:, hd:hd + 1]] for hd in range(H)], s_ref, dv)
    outs = []
    for hd in range(H):
        y = yss[hd][0]
        mu = jnp.mean(y, axis=-1, keepdims=True)
        var = jnp.mean(jnp.square(y - mu), axis=-1, keepdims=True)
        outs.append(_silu(g_ref[:, hd * dv:(hd + 1) * dv].astype(F32)) * ((y - mu) * lax.rsqrt(var + NORM_EPS)))
    o_ref[...] = jnp.concatenate(outs, axis=-1).astype(o_ref.dtype)


def ret_scan(proj, cos, sin, log_gamma, *, dk, dv):
    B, L, _ = proj.shape
    H, C = RET_HEADS, RET_CHUNK
    qk = lambda j: pl.BlockSpec((None, C, H * dk), lambda b, c: (b, c, j))
    vg = lambda j: pl.BlockSpec((None, C, H * dv), lambda b, c: (b, c, j))
    tab = pl.BlockSpec((C, dk // 2), lambda b, c: (c, 0))
    return pl.pallas_call(
        functools.partial(_ret_scan_kernel, H=H, dk=dk, dv=dv, C=C, scale=dk ** -0.5),
        grid=(B, L // C),
        in_specs=[qk(0), qk(1), vg(1), vg(2), tab, tab, pl.BlockSpec((1, H), lambda b, c: (0, 0))],
        out_specs=vg(0),
        out_shape=jax.ShapeDtypeStruct((B, L, H * dv), BF16),
        scratch_shapes=[pltpu.VMEM((H, dk, dv), F32)],
        compiler_params=_cparams("parallel", "arbitrary"),
        name="ret_scan",
    )(proj, proj, proj, proj, cos, sin, log_gamma.reshape(1, H))


def retnet_layer(h, nw, p):
    B, L, D = h.shape
    w_in = p['in_proj']
    vd = p['out_proj'].shape[0]
    dv = vd // RET_HEADS
    dk = (w_in.shape[1] - 2 * vd) // (2 * RET_HEADS)
    assert 2 * RET_HEADS * dk == vd
    h2 = h.reshape(B * L, D)
    proj = norm_linear(h2, nw, w_in.astype(BF16), 1024, BF16).reshape(B, L, -1)
    half = dk // 2
    inv_freq = 1.0 / (ROPE_BASE ** jnp.linspace(0.0, 1.0, half, dtype=F32))
    ang = jnp.arange(L, dtype=F32)[:, None] * inv_freq[None, :]
    log_gamma = jnp.log1p(-jnp.exp2(-5.0 - jnp.arange(RET_HEADS, dtype=F32)))
    y = ret_scan(proj, jnp.cos(ang), jnp.sin(ang), log_gamma, dk=dk, dv=dv)
    return linear_res(y.reshape(B * L, vd), p['out_proj'].astype(BF16), h2).reshape(B, L, D)


def _gla_kernel(q_ref, k_ref, v_ref, r_ref, glr_ref, gup_ref, gb_ref, nw_ref, o_ref, s_ref,
                *, H, dk, dv, scale):
    C = GLA_CHUNK
    SB = GLA_SUB

    @pl.when(pl.program_id(1) == 0)
    def _():
        s_ref[...] = jnp.zeros_like(s_ref)

    gate = jax.nn.log_sigmoid(_dot(glr_ref[...], gup_ref[...]) + gb_ref[...]) / GLA_TAU
    gc_all = _dot_f32(_tri(C).astype(F32), gate)
    row_id = lax.broadcasted_iota(jnp.int32, (C, 1), 0)
    sub_row = lax.broadcasted_iota(jnp.int32, (SB, 1), 0)
    col_id = lax.broadcasted_iota(jnp.int32, (1, C), 1)
    heads = range(H)
    qs = [q_ref[:, hd * dk:(hd + 1) * dk].astype(F32) * scale for hd in heads]
    ks = [k_ref[:, hd * dk:(hd + 1) * dk].astype(F32) for hd in heads]
    gcs = [gc_all[:, hd * dk:(hd + 1) * dk] for hd in heads]
    vs = [v_ref[:, hd * dv:(hd + 1) * dv] for hd in heads]
    s0s = [s_ref[hd] for hd in heads]
    a_rows = [[] for _ in heads]
    for i in range(C // SB):
        lo = i * SB
        for hd in heads:
            qi = qs[hd][lo:lo + SB]
            gi = gcs[hd][lo:lo + SB]
            ki = ks[hd][lo:lo + SB]
            cols = []
            for j in range(SB):
                d = jnp.where(sub_row >= j, gi - gi[j:j + 1], -jnp.inf)
                col = jnp.sum(qi * ki[j:j + 1] * jnp.exp(d), axis=-1, keepdims=True)
                cols.append(jnp.where(col_id == lo + j, col, 0.0))
            blk = _tree_sum(cols)
            if i > 0:
                ref = gcs[hd][lo - 1:lo]
                qt = qi * jnp.exp(gi - ref)
                kt = ks[hd] * jnp.exp(jnp.where(row_id < lo, ref - gcs[hd], -jnp.inf))
                blk = blk + jnp.where(col_id < lo, _dot_nt(qt, kt), 0.0)
            a_rows[hd].append(blk)
    os_ = [_dot(jnp.concatenate(a_rows[hd], axis=0), vs[hd]) + _dot_nt(qs[hd] * jnp.exp(gcs[hd]), s0s[hd])
           for hd in heads]
    for hd in heads:
        last = gcs[hd][C - 1:C]
        s_ref[hd] = s0s[hd] * jnp.exp(last) + _dot_tn(vs[hd], ks[hd] * jnp.exp(last - gcs[hd]))
    nw = nw_ref[...]
    outs = []
    for hd in heads:
        o = os_[hd]
        o = o * lax.rsqrt(jnp.mean(o * o, axis=-1, keepdims=True) + NORM_EPS) * nw
        outs.append(o * _silu(r_ref[:, hd * dv:(hd + 1) * dv].astype(F32)))
    o_ref[...] = jnp.concatenate(outs, axis=-1).astype(o_ref.dtype)


def gla_scan(proj, gate_up, gate_bias, norm_w, *, dk, dv, lora_pad):
    B, L, _ = proj.shape
    H, C = GLA_HEADS, GLA_CHUNK
    qk = lambda j: pl.BlockSpec((None, C, H * dk), lambda b, c: (b, c, j))
    vr = lambda j: pl.BlockSpec((None, C, H * dv), lambda b, c: (b, c, j))
    g0 = (2 * H * dk + 2 * H * dv) // lora_pad
    full = lambda x: pl.BlockSpec(x.shape, lambda b, c: (0, 0))
    consts = (gate_up, gate_bias.reshape(1, -1), norm_w.reshape(1, -1))
    return pl.pallas_call(
        functools.partial(_gla_kernel, H=H, dk=dk, dv=dv, scale=dk ** -0.5),
        grid=(B, L // C),
        in_specs=[qk(0), qk(1), vr(1), vr(2),
                  pl.BlockSpec((None, C, lora_pad), lambda b, c: (b, c, g0))] + [full(x) for x in consts],
        out_specs=vr(0),
        out_shape=jax.ShapeDtypeStruct((B, L, H * dv), BF16),
        scratch_shapes=[pltpu.VMEM((H, dv, dk), F32)],
        compiler_params=_cparams("parallel", "arbitrary"),
        name="gla_scan",
    )(proj, proj, proj, proj, proj, *consts)


def gla_layer(h, nw, p):
    B, L, D = h.shape
    lora, qk = p['gate_up'].shape
    vd = p['out_proj'].shape[0]
    assert 2 * qk == vd
    lora_pad = 128
    w_in = jnp.pad(p['in_proj'], ((0, 0), (0, lora_pad - lora)))
    n_all = w_in.shape[1]
    gate_up = jnp.pad(p['gate_up'], ((0, lora_pad - lora), (0, 0))).astype(BF16)
    h2 = h.reshape(B * L, D)
    proj = norm_linear(h2, nw, w_in.astype(BF16), n_all // 5, BF16).reshape(B, L, n_all)
    o = gla_scan(proj, gate_up, p['gate_bias'], p['norm_w'],
                 dk=qk // GLA_HEADS, dv=vd // GLA_HEADS, lora_pad=lora_pad)
    return linear_res(o.reshape(B * L, vd), p['out_proj'].astype(BF16), h2).reshape(B, L, D)


def kernel(x, meta_tokens, norm_mix, norm_mlp, norm_final, mlp_w_in, mlp_w_out, rwkv_mu, rwkv_w_r, rwkv_w_k, rwkv_w_v, rwkv_w0, rwkv_w_lora_a, rwkv_w_lora_b, rwkv_a0, rwkv_a_lora_a, rwkv_a_lora_b, rwkv_g_lora_a, rwkv_g_lora_b, rwkv_k_k, rwkv_k_a, rwkv_r_k, rwkv_ln_w, rwkv_ln_b, rwkv_w_o, m2_in_proj, m2_conv_w, m2_conv_b, m2_dt_bias, m2_a_log, m2_d, m2_norm_w, m2_out_proj, gla_in_proj, gla_gate_up, gla_gate_bias, gla_norm_w, gla_out_proj, ret_in_proj, ret_out_proj):
    B, S, D = x.shape
    depth = norm_mix.shape[0]
    n_tok = N_META + S
    L = -(-n_tok // SEQ_ALIGN) * SEQ_ALIGN
    meta = jnp.broadcast_to(meta_tokens[None].astype(x.dtype), (B, N_META, D))
    h = jnp.concatenate([meta, x, jnp.zeros((B, L - n_tok, D), x.dtype)], axis=1)
    for i in range(depth):
        m, j = i % 4, i // 4
        if m == 0:
            h = rwkv_layer(h, norm_mix[i], dict(
                mu=rwkv_mu[j], w_r=rwkv_w_r[j], w_k=rwkv_w_k[j], w_v=rwkv_w_v[j], w0=rwkv_w0[j],
                w_lora_a=rwkv_w_lora_a[j], w_lora_b=rwkv_w_lora_b[j], a0=rwkv_a0[j],
                a_lora_a=rwkv_a_lora_a[j], a_lora_b=rwkv_a_lora_b[j],
                g_lora_a=rwkv_g_lora_a[j], g_lora_b=rwkv_g_lora_b[j],
                k_k=rwkv_k_k[j], k_a=rwkv_k_a[j], r_k=rwkv_r_k[j],
                ln_w=rwkv_ln_w[j], ln_b=rwkv_ln_b[j], w_o=rwkv_w_o[j]))
        elif m == 1:
            h = mamba2_layer(h, norm_mix[i], dict(
                in_proj=m2_in_proj[j], conv_w=m2_conv_w[j], conv_b=m2_conv_b[j],
                dt_bias=m2_dt_bias[j], a_log=m2_a_log[j], d=m2_d[j], norm_w=m2_norm_w[j],
                out_proj=m2_out_proj[j]))
        elif m == 2:
            h = gla_layer(h, norm_mix[i], dict(
                in_proj=gla_in_proj[j], gate_up=gla_gate_up[j], gate_bias=gla_gate_bias[j],
                norm_w=gla_norm_w[j], out_proj=gla_out_proj[j]))
        else:
            h = retnet_layer(h, norm_mix[i], dict(in_proj=ret_in_proj[j], out_proj=ret_out_proj[j]))
        h = mlp_res(h.reshape(B * L, D), norm_mlp[i], mlp_w_in[i].astype(BF16),
                    mlp_w_out[i].astype(BF16)).reshape(B, L, D)
    out = final_norm(h.reshape(B * L, D), norm_final).reshape(B, L, D)
    return out[:, N_META:n_tok]
```

```python
import functools
import math

import jax
import jax.numpy as jnp
from jax import lax
from jax.experimental import pallas as pl
from jax.experimental.pallas import tpu as pltpu

F32 = jnp.float32
BF16 = jnp.bfloat16

N_META = 16
NORM_EPS = 1e-5
SEQ_ALIGN = 128
MAX_ROW_TILE = 640
MAX_WIDE_ROW_TILE = 1152
BF16_SUBLANES = 16
F32_SUBLANES = 8
VMEM_LIMIT = 56 * 1024 * 1024

RWKV_HEAD = 64
RWKV_GN_EPS = 64e-5
RWKV_CHUNK = 64
RWKV_PAIRS_PER_BATCH = 8

M2_HEAD = 64
M2_GROUPS = 8
M2_STATE = 128
M2_CONV = 4
M2_CHUNK = 128
M2_GROUPS_PER_STEP = 4

GLA_HEADS = 4
GLA_TAU = 16.0
GLA_CHUNK = 64
GLA_SUB = 8

RET_HEADS = 4
RET_CHUNK = 128
ROPE_BASE = 10000.0


def _cparams(*sem):
    return pltpu.CompilerParams(dimension_semantics=sem, vmem_limit_bytes=VMEM_LIMIT)


def _row_tile(rows, limit=MAX_ROW_TILE):
    for t in range(min(rows, limit) // BF16_SUBLANES * BF16_SUBLANES, 0, -BF16_SUBLANES):
        if rows % t == 0:
            return t
    raise ValueError(rows)


def _dot(a, b):
    return jnp.dot(a.astype(BF16), b.astype(BF16), preferred_element_type=F32)


def _dot_nt(a, b):
    return lax.dot_general(a.astype(BF16), b.astype(BF16), (((1,), (1,)), ((), ())),
                           preferred_element_type=F32)


def _dot_tn(a, b):
    return lax.dot_general(a.astype(BF16), b.astype(BF16), (((0,), (0,)), ((), ())),
                           preferred_element_type=F32)


def _split3(x):
    hi = x.astype(BF16)
    r1 = x - hi.astype(F32)
    mid = r1.astype(BF16)
    return hi, mid, (r1 - mid.astype(F32)).astype(BF16)


def _cumsum_rows(x):
    C = x.shape[0]
    r = lax.broadcasted_iota(jnp.int32, (C, 3 * C), 0)
    c = lax.broadcasted_iota(jnp.int32, (C, 3 * C), 1)
    tri3 = (r >= lax.rem(c, C)).astype(BF16)
    return jnp.dot(tri3, jnp.concatenate(_split3(x), axis=0), preferred_element_type=F32)


def _cumsum_lanes(x):
    C = x.shape[1]
    r = lax.broadcasted_iota(jnp.int32, (3 * C, C), 0)
    c = lax.broadcasted_iota(jnp.int32, (3 * C, C), 1)
    tri3 = (lax.rem(r, C) <= c).astype(BF16)
    return jnp.dot(jnp.concatenate(_split3(x), axis=1), tri3, preferred_element_type=F32)


def _rms_rows(x, w):
    return x * lax.rsqrt(jnp.mean(x * x, axis=-1, keepdims=True) + NORM_EPS) * w


def _tri(n, strict=False, upper=False):
    r = lax.broadcasted_iota(jnp.int32, (n, n), 0)
    c = lax.broadcasted_iota(jnp.int32, (n, n), 1)
    if upper:
        r, c = c, r
    return (r > c) if strict else (r >= c)


def _silu(x):
    return x * jax.nn.sigmoid(x)


def _tree_sum(xs):
    while len(xs) > 1:
        xs = [xs[i] + xs[i + 1] if i + 1 < len(xs) else xs[i] for i in range(0, len(xs), 2)]
    return xs[0]


def _norm_linear_kernel(h_ref, nw_ref, w_ref, o_ref, u_ref):
    @pl.when(pl.program_id(1) == 0)
    def _():
        u_ref[...] = _rms_rows(h_ref[...], nw_ref[...]).astype(BF16)

    o_ref[...] = jnp.dot(u_ref[...], w_ref[...], preferred_element_type=F32).astype(o_ref.dtype)


def norm_linear(h, nw, w, tn, out_dtype):
    M, D = h.shape
    N = w.shape[1]
    tm = _row_tile(M, MAX_WIDE_ROW_TILE)
    return pl.pallas_call(
        _norm_linear_kernel,
        grid=(M // tm, N // tn),
        in_specs=[pl.BlockSpec((tm, D), lambda i, j: (i, 0)),
                  pl.BlockSpec((1, D), lambda i, j: (0, 0)),
                  pl.BlockSpec((D, tn), lambda i, j: (0, j))],
        out_specs=pl.BlockSpec((tm, tn), lambda i, j: (i, j)),
        out_shape=jax.ShapeDtypeStruct((M, N), out_dtype),
        scratch_shapes=[pltpu.VMEM((tm, D), BF16)],
        compiler_params=_cparams("parallel", "arbitrary"),
        name="norm_linear",
    )(h, nw.reshape(1, D), w)


def _linear_res_kernel(a_ref, w_ref, h_ref, o_ref):
    o_ref[...] = h_ref[...] + jnp.dot(a_ref[...], w_ref[...], preferred_element_type=F32)


def linear_res(a, w, h):
    M, K = a.shape
    D = w.shape[1]
    tm = _row_tile(M, MAX_WIDE_ROW_TILE)
    return pl.pallas_call(
        _linear_res_kernel,
        grid=(M // tm,),
        in_specs=[pl.BlockSpec((tm, K), lambda i: (i, 0)),
                  pl.BlockSpec((K, D), lambda i: (0, 0)),
                  pl.BlockSpec((tm, D), lambda i: (i, 0))],
        out_specs=pl.BlockSpec((tm, D), lambda i: (i, 0)),
        out_shape=jax.ShapeDtypeStruct((M, D), F32),
        compiler_params=_cparams("parallel"),
        name="linear_res",
    )(a, w, h)


def _mlp_kernel(h_ref, nw_ref, w1_ref, w2_ref, o_ref, u_ref):
    f = pl.program_id(1)

    @pl.when(f == 0)
    def _():
        h = h_ref[...]
        u_ref[...] = _rms_rows(h, nw_ref[...]).astype(BF16)
        o_ref[...] = h

    hid = jnp.dot(u_ref[...], w1_ref[...], preferred_element_type=F32)
    hid = jnp.square(jnp.maximum(hid, 0.0))
    o_ref[...] += _dot(hid, w2_ref[...])


def mlp_res(h, nw, w1, w2, tf=1024):
    M, D = h.shape
    F = w1.shape[1]
    tm = _row_tile(M, MAX_WIDE_ROW_TILE)
    return pl.pallas_call(
        _mlp_kernel,
        grid=(M // tm, F // tf),
        in_specs=[pl.BlockSpec((tm, D), lambda i, f: (i, 0)),
                  pl.BlockSpec((1, D), lambda i, f: (0, 0)),
                  pl.BlockSpec((D, tf), lambda i, f: (0, f)),
                  pl.BlockSpec((tf, D), lambda i, f: (f, 0))],
        out_specs=pl.BlockSpec((tm, D), lambda i, f: (i, 0)),
        out_shape=jax.ShapeDtypeStruct((M, D), F32),
        scratch_shapes=[pltpu.VMEM((tm, D), BF16)],
        compiler_params=_cparams("parallel", "arbitrary"),
        name="mlp_res",
    )(h, nw.reshape(1, D), w1, w2)


def _final_norm_kernel(h_ref, nw_ref, o_ref):
    o_ref[...] = _rms_rows(h_ref[...], nw_ref[...])


def final_norm(h, nw):
    M, D = h.shape
    tm = _row_tile(M, MAX_WIDE_ROW_TILE)
    return pl.pallas_call(
        _final_norm_kernel,
        grid=(M // tm,),
        in_specs=[pl.BlockSpec((tm, D), lambda i: (i, 0)),
                  pl.BlockSpec((1, D), lambda i: (0, 0))],
        out_specs=pl.BlockSpec((tm, D), lambda i: (i, 0)),
        out_shape=jax.ShapeDtypeStruct((M, D), F32),
        compiler_params=_cparams("parallel"),
        name="final_norm",
    )(h, nw.reshape(1, D))


def _rwkv_proj_kernel(h_ref, halo_ref, nw_ref, mu_ref, wr_ref, wk_ref, wv_ref,
                      wla_ref, wlb_ref, ala_ref, alb_ref, gla_ref, glb_ref,
                      w0_ref, a0_ref,
                      r_ref, k_ref, v_ref, lw_ref, a_ref, g_ref, *, tiles_per_batch):
    nw = nw_ref[...]
    u = _rms_rows(h_ref[...], nw)
    first = pl.program_id(0) % tiles_per_batch == 0
    prev_last = _rms_rows(halo_ref[F32_SUBLANES - 1:F32_SUBLANES, :], nw)
    prev_last = jnp.where(first, 0.0, prev_last)
    row = lax.broadcasted_iota(jnp.int32, (u.shape[0], 1), 0)
    xx = jnp.where(row == 0, prev_last, pltpu.roll(u, 1, axis=0)) - u
    mu = mu_ref[...]
    mix = lambda i: (u + xx * mu[i:i + 1, :]).astype(BF16)
    r_ref[...] = jnp.dot(mix(0), wr_ref[...], preferred_element_type=F32).astype(r_ref.dtype)
    k_ref[...] = jnp.dot(mix(2), wk_ref[...], preferred_element_type=F32).astype(k_ref.dtype)
    v_ref[...] = jnp.dot(mix(3), wv_ref[...], preferred_element_type=F32).astype(v_ref.dtype)
    hw = jnp.tanh(jnp.dot(mix(1), wla_ref[...], preferred_element_type=F32))
    w = -jax.nn.softplus(-(w0_ref[...] + _dot(hw, wlb_ref[...]))) - 0.5
    lw_ref[...] = -jnp.exp(w)
    ha = jnp.dot(mix(4), ala_ref[...], preferred_element_type=F32)
    a_ref[...] = jax.nn.sigmoid(a0_ref[...] + _dot(ha, alb_ref[...]))
    hg = jax.nn.sigmoid(jnp.dot(mix(5), gla_ref[...], preferred_element_type=F32))
    g_ref[...] = _dot(hg, glb_ref[...]).astype(g_ref.dtype)


def rwkv_proj(h, rows_per_batch, nw, mu, wr, wk, wv, wla, wlb, ala, alb, gla, glb, w0, a0):
    M, D = h.shape
    tm = _row_tile(rows_per_batch)
    row = pl.BlockSpec((tm, D), lambda i: (i, 0))
    halo = pl.BlockSpec((F32_SUBLANES, D),
                        lambda i: (jnp.maximum(i * (tm // F32_SUBLANES) - 1, 0), 0))
    full = lambda x: pl.BlockSpec(x.shape, lambda i: (0, 0))
    consts = (nw.reshape(1, D), mu, wr, wk, wv, wla, wlb, ala, alb, gla, glb,
              w0.reshape(1, D), a0.reshape(1, D))
    sds = lambda dt: jax.ShapeDtypeStruct((M, D), dt)
    return pl.pallas_call(
        functools.partial(_rwkv_proj_kernel, tiles_per_batch=rows_per_batch // tm),
        grid=(M // tm,),
        in_specs=[row, halo] + [full(c) for c in consts],
        out_specs=[row] * 6,
        out_shape=[sds(BF16), sds(BF16), sds(BF16), sds(F32), sds(F32), sds(BF16)],
        compiler_params=_cparams("parallel"),
        name="rwkv_proj",
    )(h, h, *consts)


def _rwkv_scan_kernel(r_ref, k_ref, v_ref, lw_ref, a_ref, g_ref,
                      kk_ref, ka_ref, rk_ref, lnw_ref, lnb_ref, o_ref, s_ref):
    C = RWKV_CHUNK
    N = RWKV_HEAD
    assert C == N

    @pl.when(pl.program_id(1) == 0)
    def _():
        s_ref[...] = jnp.zeros_like(s_ref)

    W = 2 * N
    lw_all = lw_ref[...]
    cum_all = _cumsum_rows(lw_all)
    lane = lax.broadcasted_iota(jnp.int32, (1, W), 1)
    left = lane < N
    row_c = lax.broadcasted_iota(jnp.int32, (C, 1), 0)
    strict = row_c > (lane & (N - 1))
    incl = row_c >= (lane & (N - 1))
    mask2 = jnp.concatenate([strict, incl], axis=0)
    row_w = lax.broadcasted_iota(jnp.int32, (W, 1), 0)
    bdiag = (row_w < N) == left
    eye = (row_w == lane).astype(F32)
    split = lambda x: jnp.concatenate([jnp.where(left, x, 0.0), jnp.where(left, 0.0, x)], axis=0)

    def head_sum(x):
        sa = jnp.sum(jnp.where(left, x, 0.0), axis=-1, keepdims=True)
        sb = jnp.sum(jnp.where(left, 0.0, x), axis=-1, keepdims=True)
        return jnp.where(left, sa, sb)

    def run(ids):
        pairs = range(len(ids))
        sls = [slice(p * W, (p + 1) * W) for p in ids]
        s0s = [s_ref[p] for p in ids]
        pre = []
        for sl in sls:
            r = r_ref[:, sl].astype(F32)
            k = k_ref[:, sl].astype(F32)
            a = a_ref[:, sl]
            cum = cum_all[:, sl]
            kkr = k * kk_ref[:, sl]
            kk = kkr * lax.rsqrt(jnp.maximum(head_sum(kkr * kkr), 1e-24))
            km = k * (1.0 + (a - 1.0) * ka_ref[:, sl])
            beta = kk * a
            e_neg = jnp.exp(-cum)
            lhs = jnp.concatenate([-kk * jnp.exp(cum - lw_all[:, sl]), r * jnp.exp(cum)], axis=0)
            pre.append((r, km, beta, cum, lhs, beta * e_neg, km * e_neg))
        x_as = [_dot_nt(jnp.where(left, p[4], 0.0), jnp.concatenate([p[5], p[6]], axis=0)) for p in pre]
        x_bs = [_dot_nt(jnp.where(left, 0.0, p[4]), jnp.concatenate([p[6], p[5]], axis=0)) for p in pre]
        xbeta = [jnp.where(left, xa, xb) for xa, xb in zip(x_as, x_bs)]
        xkey = [jnp.where(mask2, jnp.where(left, xb, xa), 0.0) for xa, xb in zip(x_as, x_bs)]
        ps = [split(jnp.where(strict, x[:C], 0.0)) for x in xbeta]
        ts = [eye + p for p in ps]
        n_fac = int(math.log2(C))
        for i in range(1, n_fac):
            if i == 1:
                ps = [_dot(p, p) for p in ps]
            zs = [_dot(p, jnp.concatenate([p, t], axis=-1)) if i < n_fac - 1 else _dot(p, t)
                  for p, t in zip(ps, ts)]
            ts = [t + z[:, -W:] for t, z in zip(ts, zs)]
            ps = [z[:, :W] for z in zs]
        vs = [v_ref[:, sl].astype(F32) for sl in sls]
        vx = [jnp.concatenate([jnp.where(left, 0.0, v), jnp.where(left, v, 0.0)], axis=0) for v in vs]
        hss = [_dot_nt(p[4], s0) + _dot(xk, v) for p, s0, xk, v in zip(pre, s0s, xkey, vx)]
        ubd = [_dot(t, split(hs[:C])) for t, hs in zip(ts, hss)]
        ys = [hs[C:] + _dot(jnp.where(incl, x[C:], 0.0), u) for hs, x, u in zip(hss, xbeta, ubd)]
        for p in pairs:
            r, km, beta, cum, _, _, _ = pre[p]
            last = cum[C - 1:C, :]
            e_last = jnp.exp(last - cum)
            upd = _dot_tn(jnp.concatenate([vs[p], ubd[p][:C] + ubd[p][C:]], axis=0),
                          jnp.concatenate([km * e_last, beta * e_last], axis=0))
            s_ref[ids[p]] = s0s[p] * jnp.exp(last) + jnp.where(bdiag, upd, 0.0)
        outs = []
        for p in pairs:
            sl = sls[p]
            r, km = pre[p][0], pre[p][1]
            y = ys[p]
            mu_y = head_sum(y) * (1.0 / N)
            var = head_sum(jnp.square(y - mu_y)) * (1.0 / N)
            yn = (y - mu_y) * lax.rsqrt(var + RWKV_GN_EPS) * lnw_ref[:, sl] + lnb_ref[:, sl]
            bonus = head_sum(r * km * rk_ref[:, sl]) * vs[p]
            outs.append((yn + bonus) * g_ref[:, sl].astype(F32))
        o_ref[:, ids[0] * W:(ids[-1] + 1) * W] = jnp.concatenate(outs, axis=-1).astype(o_ref.dtype)

    n_pairs = r_ref.shape[-1] // W
    for lo in range(0, n_pairs, RWKV_PAIRS_PER_BATCH):
        run(list(range(lo, min(lo + RWKV_PAIRS_PER_BATCH, n_pairs))))


def rwkv_scan(r, k, v, lw, a, g, k_k, k_a, r_k, ln_w, ln_b):
    B, L, D = r.shape
    C = RWKV_CHUNK
    seq = pl.BlockSpec((None, C, D), lambda b, c: (b, c, 0))
    par = pl.BlockSpec((1, D), lambda b, c: (0, 0))
    return pl.pallas_call(
        _rwkv_scan_kernel,
        grid=(B, L // C),
        in_specs=[seq] * 6 + [par] * 5,
        out_specs=seq,
        out_shape=jax.ShapeDtypeStruct((B, L, D), BF16),
        scratch_shapes=[pltpu.VMEM((D // (2 * RWKV_HEAD), 2 * RWKV_HEAD, 2 * RWKV_HEAD), F32)],
        compiler_params=_cparams("parallel", "arbitrary"),
        name="rwkv_scan",
    )(r, k, v, lw, a, g, *(p.reshape(1, D) for p in (k_k, k_a, r_k, ln_w, ln_b)))


def rwkv_layer(h, nw, p):
    B, L, D = h.shape
    bf = lambda x: x.astype(BF16)
    h2 = h.reshape(B * L, D)
    outs = rwkv_proj(h2, L, nw, p['mu'],
                     bf(p['w_r']), bf(p['w_k']), bf(p['w_v']),
                     bf(p['w_lora_a']), bf(p['w_lora_b']), bf(p['a_lora_a']), bf(p['a_lora_b']),
                     bf(p['g_lora_a']), bf(p['g_lora_b']), p['w0'], p['a0'])
    r, k, v, lw, a, g = (o.reshape(B, L, D) for o in outs)
    o = rwkv_scan(r, k, v, lw, a, g, p['k_k'], p['k_a'], p['r_k'].reshape(D), p['ln_w'], p['ln_b'])
    return linear_res(o.reshape(B * L, D), bf(p['w_o']), h2).reshape(B, L, D)


def _sdecay_groups(qs, ks, vss, ccss, crss, s_ref, P):
    C = qs[0].shape[0]
    causal = _tri(C)
    groups = range(len(qs))
    s0s = [s_ref[g] for g in groups]
    scores = [_dot_nt(qs[g], ks[g]) for g in groups]
    qss = [_dot(qs[g], s0s[g]) for g in groups]
    yss = []
    for g in groups:
        ys = []
        for r, (v, cc, cr) in enumerate(zip(vss[g], ccss[g], crss[g])):
            dec = jnp.exp(jnp.where(causal, cc - cr, -jnp.inf))
            ys.append(_dot(scores[g] * dec, v) + jnp.exp(cc) * qss[g][:, r * P:(r + 1) * P])
        yss.append(ys)
    for g in groups:
        lasts = [cc[C - 1:C, :] for cc in ccss[g]]
        vw = jnp.concatenate([v * jnp.exp(last - cc)
                              for v, cc, last in zip(vss[g], ccss[g], lasts)], axis=-1)
        keep = jnp.concatenate([jnp.broadcast_to(jnp.exp(last), (1, P)) for last in lasts], axis=-1)
        s_ref[g] = keep * s0s[g] + _dot_tn(ks[g], vw)
    return yss


def _m2_scan_kernel(shift_ref, expand_ref, z_ref, xs_ref, xsh_ref, b_ref, bh_ref, c_ref, ch_ref,
                    wx_ref, wb_ref, wc_ref, bx_ref, bb_ref, bc_ref,
                    dtc_ref, dtr_ref, pc_ref, pr_ref, dx_ref, nw_ref, o_ref, s_ref, *, GS, R, P, N, C):
    first = pl.program_id(2) == 0

    @pl.when(first)
    def _():
        s_ref[...] = jnp.zeros_like(s_ref)

    shift = shift_ref[...]

    def conv_silu(x_ref, prev_ref, w_ref, bias_ref):
        x = x_ref[...]
        prev = prev_ref[...]
        xf = jnp.concatenate([jnp.where(first, jnp.zeros_like(prev), prev), x], axis=0)
        taps = jnp.dot(shift, xf, preferred_element_type=F32)
        w = w_ref[...]
        acc = bias_ref[...] + w[M2_CONV - 1:M2_CONV] * x.astype(F32)
        for i in range(M2_CONV - 1):
            acc = acc + w[i:i + 1] * taps[i * C:(i + 1) * C]
        return _silu(acc)

    GP = R * P
    xs = conv_silu(xs_ref, xsh_ref, wx_ref, bx_ref)
    k = conv_silu(b_ref, bh_ref, wb_ref, bb_ref)
    q = conv_silu(c_ref, ch_ref, wc_ref, bc_ref)
    pc = pc_ref[...]
    pr = pr_ref[...]
    dt_c = jax.nn.softplus(dtc_ref[...] + pc[0:1])
    dt_r = jax.nn.softplus(dtr_ref[...] + pr[:, 0:1])
    cc = _cumsum_rows(dt_c * -jnp.exp(pc[1:2]))
    cr = _cumsum_lanes(dt_r * -jnp.exp(pr[:, 1:2]))
    expand = expand_ref[...]

    def widen(x):
        return jnp.dot(jnp.concatenate(_split3(x), axis=-1), expand, preferred_element_type=F32)

    last = cc[C - 1:C]
    e_cum = widen(jnp.exp(cc))
    v = xs * widen(dt_c)
    vw = v * widen(jnp.exp(last - cc))
    keep = e_cum[C - 1:C]
    causal = _tri(C)
    head_of_lane = lax.broadcasted_iota(jnp.int32, (1, GP), 1) // P
    groups = range(GS)
    gsl = [slice(g * GP, (g + 1) * GP) for g in groups]
    qg = [q[:, g * N:(g + 1) * N] for g in groups]
    kg = [k[:, g * N:(g + 1) * N] for g in groups]
    s0s = [s_ref[g] for g in groups]
    scores = [_dot_nt(qg[g], kg[g]) for g in groups]
    qss = [_dot(qg[g], s0s[g]) for g in groups]
    pvs = []
    for g in groups:
        ps = [scores[g] * jnp.exp(jnp.where(causal, cc[:, j:j + 1] - cr[j:j + 1], -jnp.inf))
              for j in range(g * R, (g + 1) * R)]
        vbd = jnp.concatenate([jnp.where(head_of_lane == r, v[:, gsl[g]], 0.0) for r in range(R)], axis=0)
        pvs.append(_dot(jnp.concatenate(ps, axis=-1), vbd))
    for g in groups:
        s_ref[g] = keep[:, gsl[g]] * s0s[g] + _dot_tn(kg[g], vw[:, gsl[g]])
    z = z_ref[...].astype(F32)
    dx = dx_ref[...]
    outs = []
    for g in groups:
        y = pvs[g] + e_cum[:, gsl[g]] * qss[g] + dx[:, gsl[g]] * xs[:, gsl[g]]
        y = y * _silu(z[:, gsl[g]])
        outs.append(y * lax.rsqrt(jnp.mean(y * y, axis=-1, keepdims=True) + NORM_EPS))
    o_ref[...] = (jnp.concatenate(outs, axis=-1) * nw_ref[...]).astype(o_ref.dtype)


def m2_scan(zx, dt, conv_w, conv_b, dt_bias, a_log, d_skip, norm_w, *, d_inner, heads):
    B, L, _ = zx.shape
    G, N, C, GS = M2_GROUPS, M2_STATE, M2_CHUNK, M2_GROUPS_PER_STEP
    R = heads // G
    GP = d_inner // G
    NS = G // GS
    WX, WN, WR = GS * GP, GS * N, GS * R
    x0 = d_inner // WX
    b0 = 2 * d_inner // WN
    c0 = b0 + NS
    wb0 = d_inner // WN
    wc0 = wb0 + NS
    cur = lambda w, off: pl.BlockSpec((None, C, w), lambda b, g, c: (b, c, off + g))
    halo = lambda w, off: pl.BlockSpec((None, C, w), lambda b, g, c: (b, jnp.maximum(c - 1, 0), off + g))
    par = lambda rows, w, off: pl.BlockSpec((rows, w), lambda b, g, c: (0, off + g))
    tap_row = jnp.arange((M2_CONV - 1) * C)
    shift = (jnp.arange(2 * C)[None, :] ==
             (C + tap_row % C - (M2_CONV - 1) + tap_row // C)[:, None]).astype(BF16)
    dt4 = dt.reshape(B, L, NS, WR)
    pcol = jnp.stack([dt_bias, a_log]).reshape(2, NS, WR).transpose(1, 0, 2)
    prow = pcol.transpose(0, 2, 1)
    expand = jnp.tile(jnp.arange(WX)[None, :] // (GP // R) == jnp.arange(WR)[:, None], (3, 1)).astype(BF16)
    cb = conv_b.reshape(1, -1)
    const = lambda x: pl.BlockSpec(x.shape, lambda b, g, c: (0, 0))
    return pl.pallas_call(
        functools.partial(_m2_scan_kernel, GS=GS, R=R, P=GP // R, N=N, C=C),
        grid=(B, NS, L // C),
        in_specs=[const(shift), const(expand),
                  cur(WX, 0), cur(WX, x0), halo(WX, x0), cur(WN, b0), halo(WN, b0), cur(WN, c0), halo(WN, c0),
                  par(M2_CONV, WX, 0), par(M2_CONV, WN, wb0), par(M2_CONV, WN, wc0),
                  par(1, WX, 0), par(1, WN, wb0), par(1, WN, wc0),
                  pl.BlockSpec((None, None, C, WR), lambda b, g, c: (b, g, c, 0)),
                  pl.BlockSpec((None, None, WR, C), lambda b, g, c: (b, g, 0, c)),
                  pl.BlockSpec((None, 2, WR), lambda b, g, c: (g, 0, 0)),
                  pl.BlockSpec((None, WR, 2), lambda b, g, c: (g, 0, 0)),
                  par(1, WX, 0), par(1, WX, 0)],
        out_specs=cur(WX, 0),
        out_shape=jax.ShapeDtypeStruct((B, L, d_inner), BF16),
        scratch_shapes=[pltpu.VMEM((GS, N, GP), F32)],
        compiler_params=_cparams("parallel", "parallel", "arbitrary"),
        name="m2_scan",
    )(shift, expand, zx, zx, zx, zx, zx, zx, zx, conv_w, conv_w, conv_w, cb, cb, cb,
      dt4.transpose(0, 2, 1, 3), dt4.transpose(0, 2, 3, 1), pcol, prow,
      jnp.repeat(d_skip, GP // R).reshape(1, -1), norm_w.reshape(1, -1))


def mamba2_layer(h, nw, p):
    B, L, D = h.shape
    d_inner = p['norm_w'].shape[0]
    heads = p['a_log'].shape[0]
    w_in = p['in_proj']
    n_main = w_in.shape[1] - heads
    w_dt = jnp.pad(w_in[:, n_main:], ((0, 0), (0, 128 - heads)))
    h2 = h.reshape(B * L, D)
    zx = norm_linear(h2, nw, w_in[:, :n_main].astype(BF16), 2048, BF16).reshape(B, L, n_main)
    dt = norm_linear(h2, nw, w_dt.astype(BF16), 128, F32)[:, :heads].reshape(B, L, heads)
    y = m2_scan(zx, dt, p['conv_w'], p['conv_b'], p['dt_bias'], p['a_log'], p['d'], p['norm_w'],
                d_inner=d_inner, heads=heads)
    return linear_res(y.reshape(B * L, d_inner), p['out_proj'].astype(BF16), h2).reshape(B, L, D)


def _ret_scan_kernel(q_ref, k_ref, v_ref, g_ref, cos_ref, sin_ref, lg_ref, o_ref, s_ref,
                     *, H, dk, dv, C, scale):
    @pl.when(pl.program_id(1) == 0)
    def _():
        s_ref[...] = jnp.zeros_like(s_ref)

    cos = cos_ref[...]
    sin = sin_ref[...]
    half = dk // 2

    def rotary(x_ref, hd):
        x1 = x_ref[:, hd * dk:hd * dk + half].astype(F32)
        x2 = x_ref[:, hd * dk + half:(hd + 1) * dk].astype(F32)
        return jnp.concatenate([x1 * cos - x2 * sin, x1 * sin + x2 * cos], axis=-1)

    lg = lg_ref[...]
    steps_c = (lax.broadcasted_iota(jnp.int32, (C, 1), 0) + 1).astype(F32)
    steps_r = (lax.broadcasted_iota(jnp.int32, (1, C), 1) + 1).astype(F32)
    yss = _sdecay_groups(
        [rotary(q_ref, hd) for hd in range(H)], [rotary(k_ref, hd) * scale for hd in range(H)],
        [[v_ref[:, hd * dv:(hd + 1) * dv]] for hd in range(H)],
        [[steps_c * lg[:, hd:hd + 1]] for hd in range(H)],
        [[steps_r * lg[:, hd:hd + 1]] for hd in range(H)], s_ref, dv)
    outs = []
    for hd in range(H):
        y = yss[hd][0]
        mu = jnp.mean(y, axis=-1, keepdims=True)
        var = jnp.mean(jnp.square(y - mu), axis=-1, keepdims=True)
        outs.append(_silu(g_ref[:, hd * dv:(hd + 1) * dv].astype(F32)) * ((y - mu) * lax.rsqrt(var + NORM_EPS)))
    o_ref[...] = jnp.concatenate(outs, axis=-1).astype(o_ref.dtype)


def ret_scan(proj, cos, sin, log_gamma, *, dk, dv):
    B, L, _ = proj.shape
    H, C = RET_HEADS, RET_CHUNK
    qk = lambda j: pl.BlockSpec((None, C, H * dk), lambda b, c: (b, c, j))
    vg = lambda j: pl.BlockSpec((None, C, H * dv), lambda b, c: (b, c, j))
    tab = pl.BlockSpec((C, dk // 2), lambda b, c: (c, 0))
    return pl.pallas_call(
        functools.partial(_ret_scan_kernel, H=H, dk=dk, dv=dv, C=C, scale=dk ** -0.5),
        grid=(B, L // C),
        in_specs=[qk(0), qk(1), vg(1), vg(2), tab, tab, pl.BlockSpec((1, H), lambda b, c: (0, 0))],
        out_specs=vg(0),
        out_shape=jax.ShapeDtypeStruct((B, L, H * dv), BF16),
        scratch_shapes=[pltpu.VMEM((H, dk, dv), F32)],
        compiler_params=_cparams("parallel", "arbitrary"),
        name="ret_scan",
    )(proj, proj, proj, proj, cos, sin, log_gamma.reshape(1, H))


def retnet_layer(h, nw, p):
    B, L, D = h.shape
    w_in = p['in_proj']
    vd = p['out_proj'].shape[0]
    dv = vd // RET_HEADS
    dk = (w_in.shape[1] - 2 * vd) // (2 * RET_HEADS)
    assert 2 * RET_HEADS * dk == vd
    h2 = h.reshape(B * L, D)
    proj = norm_linear(h2, nw, w_in.astype(BF16), 2048, BF16).reshape(B, L, -1)
    half = dk // 2
    inv_freq = 1.0 / (ROPE_BASE ** jnp.linspace(0.0, 1.0, half, dtype=F32))
    ang = jnp.arange(L, dtype=F32)[:, None] * inv_freq[None, :]
    log_gamma = jnp.log1p(-jnp.exp2(-5.0 - jnp.arange(RET_HEADS, dtype=F32)))
    y = ret_scan(proj, jnp.cos(ang), jnp.sin(ang), log_gamma, dk=dk, dv=dv)
    return linear_res(y.reshape(B * L, vd), p['out_proj'].astype(BF16), h2).reshape(B, L, D)


def _gla_kernel(q_ref, k_ref, v_ref, r_ref, glr_ref, gup_ref, gb_ref, nw_ref, o_ref, s_ref,
                *, H, dk, dv, scale):
    C = GLA_CHUNK
    SB = GLA_SUB

    @pl.when(pl.program_id(1) == 0)
    def _():
        s_ref[...] = jnp.zeros_like(s_ref)

    gate = jax.nn.log_sigmoid(_dot(glr_ref[...], gup_ref[...]) + gb_ref[...]) / GLA_TAU
    gc_all = _cumsum_rows(gate)
    row_id = lax.broadcasted_iota(jnp.int32, (C, 1), 0)
    sub_row = lax.broadcasted_iota(jnp.int32, (SB, 1), 0)
    col_id = lax.broadcasted_iota(jnp.int32, (1, C), 1)
    heads = range(H)
    qs = [q_ref[:, hd * dk:(hd + 1) * dk].astype(F32) * scale for hd in heads]
    ks = [k_ref[:, hd * dk:(hd + 1) * dk].astype(F32) for hd in heads]
    gcs = [gc_all[:, hd * dk:(hd + 1) * dk] for hd in heads]
    vs = [v_ref[:, hd * dv:(hd + 1) * dv] for hd in heads]
    s0s = [s_ref[hd] for hd in heads]
    a_rows = [[] for _ in heads]
    for i in range(C // SB):
        lo = i * SB
        for hd in heads:
            qi = qs[hd][lo:lo + SB]
            gi = gcs[hd][lo:lo + SB]
            ki = ks[hd][lo:lo + SB]
            cols = []
            for j in range(SB):
                d = jnp.where(sub_row >= j, gi - gi[j:j + 1], -jnp.inf)
                col = jnp.sum(qi * ki[j:j + 1] * jnp.exp(d), axis=-1, keepdims=True)
                cols.append(jnp.where(col_id == lo + j, col, 0.0))
            blk = _tree_sum(cols)
            if i > 0:
                ref = gcs[hd][lo - 1:lo]
                qt = qi * jnp.exp(gi - ref)
                kt = ks[hd] * jnp.exp(jnp.where(row_id < lo, ref - gcs[hd], -jnp.inf))
                blk = blk + jnp.where(col_id < lo, _dot_nt(qt, kt), 0.0)
            a_rows[hd].append(blk)
    os_ = [_dot(jnp.concatenate(a_rows[hd], axis=0), vs[hd]) + _dot_nt(qs[hd] * jnp.exp(gcs[hd]), s0s[hd])
           for hd in heads]
    for hd in heads:
        last = gcs[hd][C - 1:C]
        s_ref[hd] = s0s[hd] * jnp.exp(last) + _dot_tn(vs[hd], ks[hd] * jnp.exp(last - gcs[hd]))
    nw = nw_ref[...]
    outs = []
    for hd in heads:
        o = os_[hd]
        o = o * lax.rsqrt(jnp.mean(o * o, axis=-1, keepdims=True) + NORM_EPS) * nw
        outs.append(o * _silu(r_ref[:, hd * dv:(hd + 1) * dv].astype(F32)))
    o_ref[...] = jnp.concatenate(outs, axis=-1).astype(o_ref.dtype)


def gla_scan(proj, gate_up, gate_bias, norm_w, *, dk, dv, lora_pad):
    B, L, _ = proj.shape
    H, C = GLA_HEADS, GLA_CHUNK
    qk = lambda j: pl.BlockSpec((None, C, H * dk), lambda b, c: (b, c, j))
    vr = lambda j: pl.BlockSpec((None, C, H * dv), lambda b, c: (b, c, j))
    g0 = (2 * H * dk + 2 * H * dv) // lora_pad
    full = lambda x: pl.BlockSpec(x.shape, lambda b, c: (0, 0))
    consts = (gate_up, gate_bias.reshape(1, -1), norm_w.reshape(1, -1))
    return pl.pallas_call(
        functools.partial(_gla_kernel, H=H, dk=dk, dv=dv, scale=dk ** -0.5),
        grid=(B, L // C),
        in_specs=[qk(0), qk(1), vr(1), vr(2),
                  pl.BlockSpec((None, C, lora_pad), lambda b, c: (b, c, g0))] + [full(x) for x in consts],
        out_specs=vr(0),
        out_shape=jax.ShapeDtypeStruct((B, L, H * dv), BF16),
        scratch_shapes=[pltpu.VMEM((H, dv, dk), F32)],
        compiler_params=_cparams("parallel", "arbitrary"),
        name="gla_scan",
    )(proj, proj, proj, proj, proj, *consts)


def gla_layer(h, nw, p):
    B, L, D = h.shape
    lora, qk = p['gate_up'].shape
    vd = p['out_proj'].shape[0]
    assert 2 * qk == vd
    lora_pad = 128
    w_in = jnp.pad(p['in_proj'], ((0, 0), (0, lora_pad - lora)))
    n_all = w_in.shape[1]
    gate_up = jnp.pad(p['gate_up'], ((0, lora_pad - lora), (0, 0))).astype(BF16)
    h2 = h.reshape(B * L, D)
    proj = norm_linear(h2, nw, w_in.astype(BF16), n_all, BF16).reshape(B, L, n_all)
    o = gla_scan(proj, gate_up, p['gate_bias'], p['norm_w'],
                 dk=qk // GLA_HEADS, dv=vd // GLA_HEADS, lora_pad=lora_pad)
    return linear_res(o.reshape(B * L, vd), p['out_proj'].astype(BF16), h2).reshape(B, L, D)


def kernel(x, meta_tokens, norm_mix, norm_mlp, norm_final, mlp_w_in, mlp_w_out, rwkv_mu, rwkv_w_r, rwkv_w_k, rwkv_w_v, rwkv_w0, rwkv_w_lora_a, rwkv_w_lora_b, rwkv_a0, rwkv_a_lora_a, rwkv_a_lora_b, rwkv_g_lora_a, rwkv_g_lora_b, rwkv_k_k, rwkv_k_a, rwkv_r_k, rwkv_ln_w, rwkv_ln_b, rwkv_w_o, m2_in_proj, m2_conv_w, m2_conv_b, m2_dt_bias, m2_a_log, m2_d, m2_norm_w, m2_out_proj, gla_in_proj, gla_gate_up, gla_gate_bias, gla_norm_w, gla_out_proj, ret_in_proj, ret_out_proj):
    B, S, D = x.shape
    depth = norm_mix.shape[0]
    n_tok = N_META + S
    L = -(-n_tok // SEQ_ALIGN) * SEQ_ALIGN
    meta = jnp.broadcast_to(meta_tokens[None].astype(x.dtype), (B, N_META, D))
    h = jnp.concatenate([meta, x, jnp.zeros((B, L - n_tok, D), x.dtype)], axis=1)
    for i in range(depth):
        m, j = i % 4, i // 4
        if m == 0:
            h = rwkv_layer(h, norm_mix[i], dict(
                mu=rwkv_mu[j], w_r=rwkv_w_r[j], w_k=rwkv_w_k[j], w_v=rwkv_w_v[j], w0=rwkv_w0[j],
                w_lora_a=rwkv_w_lora_a[j], w_lora_b=rwkv_w_lora_b[j], a0=rwkv_a0[j],
                a_lora_a=rwkv_a_lora_a[j], a_lora_b=rwkv_a_lora_b[j],
                g_lora_a=rwkv_g_lora_a[j], g_lora_b=rwkv_g_lora_b[j],
                k_k=rwkv_k_k[j], k_a=rwkv_k_a[j], r_k=rwkv_r_k[j],
                ln_w=rwkv_ln_w[j], ln_b=rwkv_ln_b[j], w_o=rwkv_w_o[j]))
        elif m == 1:
            h = mamba2_layer(h, norm_mix[i], dict(
                in_proj=m2_in_proj[j], conv_w=m2_conv_w[j], conv_b=m2_conv_b[j],
                dt_bias=m2_dt_bias[j], a_log=m2_a_log[j], d=m2_d[j], norm_w=m2_norm_w[j],
                out_proj=m2_out_proj[j]))
        elif m == 2:
            h = gla_layer(h, norm_mix[i], dict(
                in_proj=gla_in_proj[j], gate_up=gla_gate_up[j], gate_bias=gla_gate_bias[j],
                norm_w=gla_norm_w[j], out_proj=gla_out_proj[j]))
        else:
            h = retnet_layer(h, norm_mix[i], dict(in_proj=ret_in_proj[j], out_proj=ret_out_proj[j]))
        h = mlp_res(h.reshape(B * L, D), norm_mlp[i], mlp_w_in[i].astype(BF16),
                    mlp_w_out[i].astype(BF16)).reshape(B, L, D)
    out = final_norm(h.reshape(B * L, D), norm_final).reshape(B, L, D)
    return out[:, N_META:n_tok]
```

```python
import functools
import math

import jax
import jax.numpy as jnp
from jax import lax
from jax.experimental import pallas as pl
from jax.experimental.pallas import tpu as pltpu

F32 = jnp.float32
BF16 = jnp.bfloat16

N_META = 16
NORM_EPS = 1e-5
SEQ_ALIGN = 128
MAX_ROW_TILE = 640
MAX_WIDE_ROW_TILE = 1152
BF16_SUBLANES = 16
F32_SUBLANES = 8
VMEM_LIMIT = 56 * 1024 * 1024

RWKV_HEAD = 64
RWKV_GN_EPS = 64e-5
RWKV_CHUNK = 64
RWKV_PAIRS_PER_BATCH = 8

M2_HEAD = 64
M2_GROUPS = 8
M2_STATE = 128
M2_CONV = 4
M2_CHUNK = 128
M2_GROUPS_PER_STEP = 4

GLA_HEADS = 4
GLA_TAU = 16.0
GLA_CHUNK = 64
GLA_SUB = 8

RET_HEADS = 4
RET_CHUNK = 128
ROPE_BASE = 10000.0


def _cparams(*sem):
    return pltpu.CompilerParams(dimension_semantics=sem, vmem_limit_bytes=VMEM_LIMIT)


def _row_tile(rows, limit=MAX_ROW_TILE):
    for t in range(min(rows, limit) // BF16_SUBLANES * BF16_SUBLANES, 0, -BF16_SUBLANES):
        if rows % t == 0:
            return t
    raise ValueError(rows)


def _dot(a, b):
    return jnp.dot(a.astype(BF16), b.astype(BF16), preferred_element_type=F32)


def _dot_nt(a, b):
    return lax.dot_general(a.astype(BF16), b.astype(BF16), (((1,), (1,)), ((), ())),
                           preferred_element_type=F32)


def _dot_tn(a, b):
    return lax.dot_general(a.astype(BF16), b.astype(BF16), (((0,), (0,)), ((), ())),
                           preferred_element_type=F32)


def _split3(x):
    hi = x.astype(BF16)
    r1 = x - hi.astype(F32)
    mid = r1.astype(BF16)
    return hi, mid, (r1 - mid.astype(F32)).astype(BF16)


def _cumsum_rows(x):
    C = x.shape[0]
    r = lax.broadcasted_iota(jnp.int32, (C, 3 * C), 0)
    c = lax.broadcasted_iota(jnp.int32, (C, 3 * C), 1)
    tri3 = (r >= lax.rem(c, C)).astype(BF16)
    return jnp.dot(tri3, jnp.concatenate(_split3(x), axis=0), preferred_element_type=F32)


def _cumsum_lanes(x):
    C = x.shape[1]
    r = lax.broadcasted_iota(jnp.int32, (3 * C, C), 0)
    c = lax.broadcasted_iota(jnp.int32, (3 * C, C), 1)
    tri3 = (lax.rem(r, C) <= c).astype(BF16)
    return jnp.dot(jnp.concatenate(_split3(x), axis=1), tri3, preferred_element_type=F32)


def _rms_rows(x, w):
    return x * lax.rsqrt(jnp.mean(x * x, axis=-1, keepdims=True) + NORM_EPS) * w


def _tri(n, strict=False, upper=False):
    r = lax.broadcasted_iota(jnp.int32, (n, n), 0)
    c = lax.broadcasted_iota(jnp.int32, (n, n), 1)
    if upper:
        r, c = c, r
    return (r > c) if strict else (r >= c)


def _silu(x):
    return x * jax.nn.sigmoid(x)


def _tree_sum(xs):
    while len(xs) > 1:
        xs = [xs[i] + xs[i + 1] if i + 1 < len(xs) else xs[i] for i in range(0, len(xs), 2)]
    return xs[0]


def _norm_linear_kernel(h_ref, nw_ref, w_ref, o_ref, u_ref):
    @pl.when(pl.program_id(1) == 0)
    def _():
        u_ref[...] = _rms_rows(h_ref[...], nw_ref[...]).astype(BF16)

    o_ref[...] = jnp.dot(u_ref[...], w_ref[...], preferred_element_type=F32).astype(o_ref.dtype)


def norm_linear(h, nw, w, tn, out_dtype):
    M, D = h.shape
    N = w.shape[1]
    tm = _row_tile(M, MAX_WIDE_ROW_TILE)
    return pl.pallas_call(
        _norm_linear_kernel,
        grid=(M // tm, N // tn),
        in_specs=[pl.BlockSpec((tm, D), lambda i, j: (i, 0)),
                  pl.BlockSpec((1, D), lambda i, j: (0, 0)),
                  pl.BlockSpec((D, tn), lambda i, j: (0, j))],
        out_specs=pl.BlockSpec((tm, tn), lambda i, j: (i, j)),
        out_shape=jax.ShapeDtypeStruct((M, N), out_dtype),
        scratch_shapes=[pltpu.VMEM((tm, D), BF16)],
        compiler_params=_cparams("parallel", "arbitrary"),
        name="norm_linear",
    )(h, nw.reshape(1, D), w)


def _out_mlp_kernel(a_ref, wo_ref, h_ref, nw_ref, w1_ref, w2_ref, fnw_ref, o_ref, u_ref, *, final):
    f = pl.program_id(1)

    @pl.when(f == 0)
    def _():
        h = h_ref[...] + jnp.dot(a_ref[...], wo_ref[...], preferred_element_type=F32)
        u_ref[...] = _rms_rows(h, nw_ref[...]).astype(BF16)
        o_ref[...] = h

    hid = jnp.dot(u_ref[...], w1_ref[...], preferred_element_type=F32)
    hid = jnp.square(jnp.maximum(hid, 0.0))
    o_ref[...] += _dot(hid, w2_ref[...])

    if final:
        @pl.when(f == pl.num_programs(1) - 1)
        def _():
            o_ref[...] = _rms_rows(o_ref[...], fnw_ref[...])


def out_mlp(a, wo, h, nw, w1, w2, fnw, final, tf=512):
    M, K = a.shape
    D = h.shape[1]
    F = w1.shape[1]
    tm = _row_tile(M, MAX_WIDE_ROW_TILE)
    row = lambda w: pl.BlockSpec((tm, w), lambda i, f: (i, 0))
    vec = pl.BlockSpec((1, D), lambda i, f: (0, 0))
    return pl.pallas_call(
        functools.partial(_out_mlp_kernel, final=final),
        grid=(M // tm, F // tf),
        in_specs=[row(K), pl.BlockSpec((K, D), lambda i, f: (0, 0)), row(D), vec,
                  pl.BlockSpec((D, tf), lambda i, f: (0, f)),
                  pl.BlockSpec((tf, D), lambda i, f: (f, 0)), vec],
        out_specs=row(D),
        out_shape=jax.ShapeDtypeStruct((M, D), F32),
        scratch_shapes=[pltpu.VMEM((tm, D), BF16)],
        compiler_params=_cparams("parallel", "arbitrary"),
        name="out_mlp",
    )(a, wo, h, nw.reshape(1, D), w1, w2, fnw.reshape(1, D))


def _rwkv_proj_kernel(h_ref, halo_ref, nw_ref, mu_ref, wr_ref, wk_ref, wv_ref,
                      wla_ref, wlb_ref, ala_ref, alb_ref, gla_ref, glb_ref,
                      w0_ref, a0_ref,
                      r_ref, k_ref, v_ref, lw_ref, a_ref, g_ref, *, tiles_per_batch):
    nw = nw_ref[...]
    u = _rms_rows(h_ref[...], nw)
    first = pl.program_id(0) % tiles_per_batch == 0
    prev_last = _rms_rows(halo_ref[F32_SUBLANES - 1:F32_SUBLANES, :], nw)
    prev_last = jnp.where(first, 0.0, prev_last)
    row = lax.broadcasted_iota(jnp.int32, (u.shape[0], 1), 0)
    xx = jnp.where(row == 0, prev_last, pltpu.roll(u, 1, axis=0)) - u
    mu = mu_ref[...]
    mix = lambda i: (u + xx * mu[i:i + 1, :]).astype(BF16)
    r_ref[...] = jnp.dot(mix(0), wr_ref[...], preferred_element_type=F32).astype(r_ref.dtype)
    k_ref[...] = jnp.dot(mix(2), wk_ref[...], preferred_element_type=F32).astype(k_ref.dtype)
    v_ref[...] = jnp.dot(mix(3), wv_ref[...], preferred_element_type=F32).astype(v_ref.dtype)
    hw = jnp.tanh(jnp.dot(mix(1), wla_ref[...], preferred_element_type=F32))
    w = -jax.nn.softplus(-(w0_ref[...] + _dot(hw, wlb_ref[...]))) - 0.5
    lw_ref[...] = -jnp.exp(w)
    ha = jnp.dot(mix(4), ala_ref[...], preferred_element_type=F32)
    a_ref[...] = jax.nn.sigmoid(a0_ref[...] + _dot(ha, alb_ref[...]))
    hg = jax.nn.sigmoid(jnp.dot(mix(5), gla_ref[...], preferred_element_type=F32))
    g_ref[...] = _dot(hg, glb_ref[...]).astype(g_ref.dtype)


def rwkv_proj(h, rows_per_batch, nw, mu, wr, wk, wv, wla, wlb, ala, alb, gla, glb, w0, a0):
    M, D = h.shape
    tm = _row_tile(rows_per_batch)
    row = pl.BlockSpec((tm, D), lambda i: (i, 0))
    halo = pl.BlockSpec((F32_SUBLANES, D),
                        lambda i: (jnp.maximum(i * (tm // F32_SUBLANES) - 1, 0), 0))
    full = lambda x: pl.BlockSpec(x.shape, lambda i: (0, 0))
    consts = (nw.reshape(1, D), mu, wr, wk, wv, wla, wlb, ala, alb, gla, glb,
              w0.reshape(1, D), a0.reshape(1, D))
    sds = lambda dt: jax.ShapeDtypeStruct((M, D), dt)
    return pl.pallas_call(
        functools.partial(_rwkv_proj_kernel, tiles_per_batch=rows_per_batch // tm),
        grid=(M // tm,),
        in_specs=[row, halo] + [full(c) for c in consts],
        out_specs=[row] * 6,
        out_shape=[sds(BF16), sds(BF16), sds(BF16), sds(F32), sds(F32), sds(BF16)],
        compiler_params=_cparams("parallel"),
        name="rwkv_proj",
    )(h, h, *consts)


def _rwkv_scan_kernel(r_ref, k_ref, v_ref, lw_ref, a_ref, g_ref,
                      kk_ref, ka_ref, rk_ref, lnw_ref, lnb_ref, o_ref, s_ref):
    C = RWKV_CHUNK
    N = RWKV_HEAD
    assert C == N

    @pl.when(pl.program_id(1) == 0)
    def _():
        s_ref[...] = jnp.zeros_like(s_ref)

    W = 2 * N
    lw_all = lw_ref[...]
    cum_all = _cumsum_rows(lw_all)
    lane = lax.broadcasted_iota(jnp.int32, (1, W), 1)
    left = lane < N
    row_c = lax.broadcasted_iota(jnp.int32, (C, 1), 0)
    strict = row_c > (lane & (N - 1))
    incl = row_c >= (lane & (N - 1))
    mask2 = jnp.concatenate([strict, incl], axis=0)
    row_w = lax.broadcasted_iota(jnp.int32, (W, 1), 0)
    bdiag = (row_w < N) == left
    eye = (row_w == lane).astype(F32)
    split = lambda x: jnp.concatenate([jnp.where(left, x, 0.0), jnp.where(left, 0.0, x)], axis=0)

    def head_sum(x):
        sa = jnp.sum(jnp.where(left, x, 0.0), axis=-1, keepdims=True)
        sb = jnp.sum(jnp.where(left, 0.0, x), axis=-1, keepdims=True)
        return jnp.where(left, sa, sb)

    def run(ids):
        pairs = range(len(ids))
        sls = [slice(p * W, (p + 1) * W) for p in ids]
        s0s = [s_ref[p] for p in ids]
        pre = []
        for sl in sls:
            r = r_ref[:, sl].astype(F32)
            k = k_ref[:, sl].astype(F32)
            a = a_ref[:, sl]
            cum = cum_all[:, sl]
            kkr = k * kk_ref[:, sl]
            kk = kkr * lax.rsqrt(jnp.maximum(head_sum(kkr * kkr), 1e-24))
            km = k * (1.0 + (a - 1.0) * ka_ref[:, sl])
            beta = kk * a
            e_neg = jnp.exp(-cum)
            lhs = jnp.concatenate([-kk * jnp.exp(cum - lw_all[:, sl]), r * jnp.exp(cum)], axis=0)
            pre.append((r, km, beta, cum, lhs, beta * e_neg, km * e_neg))
        x_as = [_dot_nt(jnp.where(left, p[4], 0.0), jnp.concatenate([p[5], p[6]], axis=0)) for p in pre]
        x_bs = [_dot_nt(jnp.where(left, 0.0, p[4]), jnp.concatenate([p[6], p[5]], axis=0)) for p in pre]
        xbeta = [jnp.where(left, xa, xb) for xa, xb in zip(x_as, x_bs)]
        xkey = [jnp.where(mask2, jnp.where(left, xb, xa), 0.0) for xa, xb in zip(x_as, x_bs)]
        ps = [split(jnp.where(strict, x[:C], 0.0)) for x in xbeta]
        ts = [eye + p for p in ps]
        n_fac = int(math.log2(C))
        for i in range(1, n_fac):
            if i == 1:
                ps = [_dot(p, p) for p in ps]
            zs = [_dot(p, jnp.concatenate([p, t], axis=-1)) if i < n_fac - 1 else _dot(p, t)
                  for p, t in zip(ps, ts)]
            ts = [t + z[:, -W:] for t, z in zip(ts, zs)]
            ps = [z[:, :W] for z in zs]
        vs = [v_ref[:, sl].astype(F32) for sl in sls]
        vx = [jnp.concatenate([jnp.where(left, 0.0, v), jnp.where(left, v, 0.0)], axis=0) for v in vs]
        hss = [_dot_nt(p[4], s0) + _dot(xk, v) for p, s0, xk, v in zip(pre, s0s, xkey, vx)]
        ubd = [_dot(t, split(hs[:C])) for t, hs in zip(ts, hss)]
        ys = [hs[C:] + _dot(jnp.where(incl, x[C:], 0.0), u) for hs, x, u in zip(hss, xbeta, ubd)]
        for p in pairs:
            r, km, beta, cum, _, _, _ = pre[p]
            last = cum[C - 1:C, :]
            e_last = jnp.exp(last - cum)
            upd = _dot_tn(jnp.concatenate([vs[p], ubd[p][:C] + ubd[p][C:]], axis=0),
                          jnp.concatenate([km * e_last, beta * e_last], axis=0))
            s_ref[ids[p]] = s0s[p] * jnp.exp(last) + jnp.where(bdiag, upd, 0.0)
        outs = []
        for p in pairs:
            sl = sls[p]
            r, km = pre[p][0], pre[p][1]
            y = ys[p]
            mu_y = head_sum(y) * (1.0 / N)
            var = head_sum(jnp.square(y - mu_y)) * (1.0 / N)
            yn = (y - mu_y) * lax.rsqrt(var + RWKV_GN_EPS) * lnw_ref[:, sl] + lnb_ref[:, sl]
            bonus = head_sum(r * km * rk_ref[:, sl]) * vs[p]
            outs.append((yn + bonus) * g_ref[:, sl].astype(F32))
        o_ref[:, ids[0] * W:(ids[-1] + 1) * W] = jnp.concatenate(outs, axis=-1).astype(o_ref.dtype)

    n_pairs = r_ref.shape[-1] // W
    for lo in range(0, n_pairs, RWKV_PAIRS_PER_BATCH):
        run(list(range(lo, min(lo + RWKV_PAIRS_PER_BATCH, n_pairs))))


def rwkv_scan(r, k, v, lw, a, g, k_k, k_a, r_k, ln_w, ln_b):
    B, L, D = r.shape
    C = RWKV_CHUNK
    seq = pl.BlockSpec((None, C, D), lambda b, c: (b, c, 0))
    par = pl.BlockSpec((1, D), lambda b, c: (0, 0))
    return pl.pallas_call(
        _rwkv_scan_kernel,
        grid=(B, L // C),
        in_specs=[seq] * 6 + [par] * 5,
        out_specs=seq,
        out_shape=jax.ShapeDtypeStruct((B, L, D), BF16),
        scratch_shapes=[pltpu.VMEM((D // (2 * RWKV_HEAD), 2 * RWKV_HEAD, 2 * RWKV_HEAD), F32)],
        compiler_params=_cparams("parallel", "arbitrary"),
        name="rwkv_scan",
    )(r, k, v, lw, a, g, *(p.reshape(1, D) for p in (k_k, k_a, r_k, ln_w, ln_b)))


def rwkv_layer(h, nw, p):
    B, L, D = h.shape
    bf = lambda x: x.astype(BF16)
    h2 = h.reshape(B * L, D)
    outs = rwkv_proj(h2, L, nw, p['mu'],
                     bf(p['w_r']), bf(p['w_k']), bf(p['w_v']),
                     bf(p['w_lora_a']), bf(p['w_lora_b']), bf(p['a_lora_a']), bf(p['a_lora_b']),
                     bf(p['g_lora_a']), bf(p['g_lora_b']), p['w0'], p['a0'])
    r, k, v, lw, a, g = (o.reshape(B, L, D) for o in outs)
    o = rwkv_scan(r, k, v, lw, a, g, p['k_k'], p['k_a'], p['r_k'].reshape(D), p['ln_w'], p['ln_b'])
    return o.reshape(B * L, D), bf(p['w_o'])


def _sdecay_groups(qs, ks, vss, ccss, crss, s_ref, P):
    C = qs[0].shape[0]
    causal = _tri(C)
    groups = range(len(qs))
    s0s = [s_ref[g] for g in groups]
    scores = [_dot_nt(qs[g], ks[g]) for g in groups]
    qss = [_dot(qs[g], s0s[g]) for g in groups]
    yss = []
    for g in groups:
        ys = []
        for r, (v, cc, cr) in enumerate(zip(vss[g], ccss[g], crss[g])):
            dec = jnp.exp(jnp.where(causal, cc - cr, -jnp.inf))
            ys.append(_dot(scores[g] * dec, v) + jnp.exp(cc) * qss[g][:, r * P:(r + 1) * P])
        yss.append(ys)
    for g in groups:
        lasts = [cc[C - 1:C, :] for cc in ccss[g]]
        vw = jnp.concatenate([v * jnp.exp(last - cc)
                              for v, cc, last in zip(vss[g], ccss[g], lasts)], axis=-1)
        keep = jnp.concatenate([jnp.broadcast_to(jnp.exp(last), (1, P)) for last in lasts], axis=-1)
        s_ref[g] = keep * s0s[g] + _dot_tn(ks[g], vw)
    return yss


def _m2_scan_kernel(shift_ref, expand_ref, z_ref, xs_ref, xsh_ref, b_ref, bh_ref, c_ref, ch_ref,
                    wx_ref, wb_ref, wc_ref, bx_ref, bb_ref, bc_ref,
                    dtc_ref, dtr_ref, pc_ref, pr_ref, dx_ref, nw_ref, o_ref, s_ref, *, GS, R, P, N, C):
    first = pl.program_id(2) == 0

    @pl.when(first)
    def _():
        s_ref[...] = jnp.zeros_like(s_ref)

    shift = shift_ref[...]

    def conv_silu(x_ref, prev_ref, w_ref, bias_ref):
        x = x_ref[...]
        prev = prev_ref[...]
        xf = jnp.concatenate([jnp.where(first, jnp.zeros_like(prev), prev), x], axis=0)
        taps = jnp.dot(shift, xf, preferred_element_type=F32)
        w = w_ref[...]
        acc = bias_ref[...] + w[M2_CONV - 1:M2_CONV] * x.astype(F32)
        for i in range(M2_CONV - 1):
            acc = acc + w[i:i + 1] * taps[i * C:(i + 1) * C]
        return _silu(acc)

    GP = R * P
    xs = conv_silu(xs_ref, xsh_ref, wx_ref, bx_ref)
    k = conv_silu(b_ref, bh_ref, wb_ref, bb_ref)
    q = conv_silu(c_ref, ch_ref, wc_ref, bc_ref)
    pc = pc_ref[...]
    pr = pr_ref[...]
    dt_c = jax.nn.softplus(dtc_ref[...] + pc[0:1])
    dt_r = jax.nn.softplus(dtr_ref[...] + pr[:, 0:1])
    cc = _cumsum_rows(dt_c * -jnp.exp(pc[1:2]))
    cr = _cumsum_lanes(dt_r * -jnp.exp(pr[:, 1:2]))
    expand = expand_ref[...]

    def widen(x):
        return jnp.dot(jnp.concatenate(_split3(x), axis=-1), expand, preferred_element_type=F32)

    last = cc[C - 1:C]
    e_cum = widen(jnp.exp(cc))
    v = xs * widen(dt_c)
    vw = v * widen(jnp.exp(last - cc))
    keep = e_cum[C - 1:C]
    causal = _tri(C)
    head_of_lane = lax.broadcasted_iota(jnp.int32, (1, GP), 1) // P
    groups = range(GS)
    gsl = [slice(g * GP, (g + 1) * GP) for g in groups]
    qg = [q[:, g * N:(g + 1) * N] for g in groups]
    kg = [k[:, g * N:(g + 1) * N] for g in groups]
    s0s = [s_ref[g] for g in groups]
    scores = [_dot_nt(qg[g], kg[g]) for g in groups]
    qss = [_dot(qg[g], s0s[g]) for g in groups]
    pvs = []
    for g in groups:
        ps = [scores[g] * jnp.exp(jnp.where(causal, cc[:, j:j + 1] - cr[j:j + 1], -jnp.inf))
              for j in range(g * R, (g + 1) * R)]
        vbd = jnp.concatenate([jnp.where(head_of_lane == r, v[:, gsl[g]], 0.0) for r in range(R)], axis=0)
        pvs.append(_dot(jnp.concatenate(ps, axis=-1), vbd))
    for g in groups:
        s_ref[g] = keep[:, gsl[g]] * s0s[g] + _dot_tn(kg[g], vw[:, gsl[g]])
    z = z_ref[...].astype(F32)
    dx = dx_ref[...]
    outs = []
    for g in groups:
        y = pvs[g] + e_cum[:, gsl[g]] * qss[g] + dx[:, gsl[g]] * xs[:, gsl[g]]
        y = y * _silu(z[:, gsl[g]])
        outs.append(y * lax.rsqrt(jnp.mean(y * y, axis=-1, keepdims=True) + NORM_EPS))
    o_ref[...] = (jnp.concatenate(outs, axis=-1) * nw_ref[...]).astype(o_ref.dtype)


def m2_scan(zx, dt, conv_w, conv_b, dt_bias, a_log, d_skip, norm_w, *, d_inner, heads):
    B, L, _ = zx.shape
    G, N, C, GS = M2_GROUPS, M2_STATE, M2_CHUNK, M2_GROUPS_PER_STEP
    R = heads // G
    GP = d_inner // G
    NS = G // GS
    WX, WN, WR = GS * GP, GS * N, GS * R
    x0 = d_inner // WX
    b0 = 2 * d_inner // WN
    c0 = b0 + NS
    wb0 = d_inner // WN
    wc0 = wb0 + NS
    cur = lambda w, off: pl.BlockSpec((None, C, w), lambda b, g, c: (b, c, off + g))
    halo = lambda w, off: pl.BlockSpec((None, C, w), lambda b, g, c: (b, jnp.maximum(c - 1, 0), off + g))
    par = lambda rows, w, off: pl.BlockSpec((rows, w), lambda b, g, c: (0, off + g))
    tap_row = jnp.arange((M2_CONV - 1) * C)
    shift = (jnp.arange(2 * C)[None, :] ==
             (C + tap_row % C - (M2_CONV - 1) + tap_row // C)[:, None]).astype(BF16)
    dt4 = dt.reshape(B, L, NS, WR)
    pcol = jnp.stack([dt_bias, a_log]).reshape(2, NS, WR).transpose(1, 0, 2)
    prow = pcol.transpose(0, 2, 1)
    expand = jnp.tile(jnp.arange(WX)[None, :] // (GP // R) == jnp.arange(WR)[:, None], (3, 1)).astype(BF16)
    cb = conv_b.reshape(1, -1)
    const = lambda x: pl.BlockSpec(x.shape, lambda b, g, c: (0, 0))
    return pl.pallas_call(
        functools.partial(_m2_scan_kernel, GS=GS, R=R, P=GP // R, N=N, C=C),
        grid=(B, NS, L // C),
        in_specs=[const(shift), const(expand),
                  cur(WX, 0), cur(WX, x0), halo(WX, x0), cur(WN, b0), halo(WN, b0), cur(WN, c0), halo(WN, c0),
                  par(M2_CONV, WX, 0), par(M2_CONV, WN, wb0), par(M2_CONV, WN, wc0),
                  par(1, WX, 0), par(1, WN, wb0), par(1, WN, wc0),
                  pl.BlockSpec((None, None, C, WR), lambda b, g, c: (b, g, c, 0)),
                  pl.BlockSpec((None, None, WR, C), lambda b, g, c: (b, g, 0, c)),
                  pl.BlockSpec((None, 2, WR), lambda b, g, c: (g, 0, 0)),
                  pl.BlockSpec((None, WR, 2), lambda b, g, c: (g, 0, 0)),
                  par(1, WX, 0), par(1, WX, 0)],
        out_specs=cur(WX, 0),
        out_shape=jax.ShapeDtypeStruct((B, L, d_inner), BF16),
        scratch_shapes=[pltpu.VMEM((GS, N, GP), F32)],
        compiler_params=_cparams("parallel", "parallel", "arbitrary"),
        name="m2_scan",
    )(shift, expand, zx, zx, zx, zx, zx, zx, zx, conv_w, conv_w, conv_w, cb, cb, cb,
      dt4.transpose(0, 2, 1, 3), dt4.transpose(0, 2, 3, 1), pcol, prow,
      jnp.repeat(d_skip, GP // R).reshape(1, -1), norm_w.reshape(1, -1))


def mamba2_layer(h, nw, p):
    B, L, D = h.shape
    d_inner = p['norm_w'].shape[0]
    heads = p['a_log'].shape[0]
    w_in = p['in_proj']
    n_main = w_in.shape[1] - heads
    w_dt = jnp.pad(w_in[:, n_main:], ((0, 0), (0, 128 - heads)))
    h2 = h.reshape(B * L, D)
    zx = norm_linear(h2, nw, w_in[:, :n_main].astype(BF16), 2048, BF16).reshape(B, L, n_main)
    dt = norm_linear(h2, nw, w_dt.astype(BF16), 128, F32)[:, :heads].reshape(B, L, heads)
    y = m2_scan(zx, dt, p['conv_w'], p['conv_b'], p['dt_bias'], p['a_log'], p['d'], p['norm_w'],
                d_inner=d_inner, heads=heads)
    return y.reshape(B * L, d_inner), p['out_proj'].astype(BF16)


def _ret_scan_kernel(q_ref, k_ref, v_ref, g_ref, cos_ref, sin_ref, lg_ref, o_ref, s_ref,
                     *, H, dk, dv, C, scale):
    @pl.when(pl.program_id(1) == 0)
    def _():
        s_ref[...] = jnp.zeros_like(s_ref)

    cos = cos_ref[...]
    sin = sin_ref[...]
    half = dk // 2

    def rotary(x_ref, hd):
        x1 = x_ref[:, hd * dk:hd * dk + half].astype(F32)
        x2 = x_ref[:, hd * dk + half:(hd + 1) * dk].astype(F32)
        return jnp.concatenate([x1 * cos - x2 * sin, x1 * sin + x2 * cos], axis=-1)

    lg = lg_ref[...]
    steps_c = (lax.broadcasted_iota(jnp.int32, (C, 1), 0) + 1).astype(F32)
    steps_r = (lax.broadcasted_iota(jnp.int32, (1, C), 1) + 1).astype(F32)
    yss = _sdecay_groups(
        [rotary(q_ref, hd) for hd in range(H)], [rotary(k_ref, hd) * scale for hd in range(H)],
        [[v_ref[:, hd * dv:(hd + 1) * dv]] for hd in range(H)],
        [[steps_c * lg[:, hd:hd + 1]] for hd in range(H)],
        [[steps_r * lg[:, hd:hd + 1]] for hd in range(H)], s_ref, dv)
    outs = []
    for hd in range(H):
        y = yss[hd][0]
        mu = jnp.mean(y, axis=-1, keepdims=True)
        var = jnp.mean(jnp.square(y - mu), axis=-1, keepdims=True)
        outs.append(_silu(g_ref[:, hd * dv:(hd + 1) * dv].astype(F32)) * ((y - mu) * lax.rsqrt(var + NORM_EPS)))
    o_ref[...] = jnp.concatenate(outs, axis=-1).astype(o_ref.dtype)


def ret_scan(proj, cos, sin, log_gamma, *, dk, dv):
    B, L, _ = proj.shape
    H, C = RET_HEADS, RET_CHUNK
    qk = lambda j: pl.BlockSpec((None, C, H * dk), lambda b, c: (b, c, j))
    vg = lambda j: pl.BlockSpec((None, C, H * dv), lambda b, c: (b, c, j))
    tab = pl.BlockSpec((C, dk // 2), lambda b, c: (c, 0))
    return pl.pallas_call(
        functools.partial(_ret_scan_kernel, H=H, dk=dk, dv=dv, C=C, scale=dk ** -0.5),
        grid=(B, L // C),
        in_specs=[qk(0), qk(1), vg(1), vg(2), tab, tab, pl.BlockSpec((1, H), lambda b, c: (0, 0))],
        out_specs=vg(0),
        out_shape=jax.ShapeDtypeStruct((B, L, H * dv), BF16),
        scratch_shapes=[pltpu.VMEM((H, dk, dv), F32)],
        compiler_params=_cparams("parallel", "arbitrary"),
        name="ret_scan",
    )(proj, proj, proj, proj, cos, sin, log_gamma.reshape(1, H))


def retnet_layer(h, nw, p):
    B, L, D = h.shape
    w_in = p['in_proj']
    vd = p['out_proj'].shape[0]
    dv = vd // RET_HEADS
    dk = (w_in.shape[1] - 2 * vd) // (2 * RET_HEADS)
    assert 2 * RET_HEADS * dk == vd
    h2 = h.reshape(B * L, D)
    proj = norm_linear(h2, nw, w_in.astype(BF16), 2048, BF16).reshape(B, L, -1)
    half = dk // 2
    inv_freq = 1.0 / (ROPE_BASE ** jnp.linspace(0.0, 1.0, half, dtype=F32))
    ang = jnp.arange(L, dtype=F32)[:, None] * inv_freq[None, :]
    log_gamma = jnp.log1p(-jnp.exp2(-5.0 - jnp.arange(RET_HEADS, dtype=F32)))
    y = ret_scan(proj, jnp.cos(ang), jnp.sin(ang), log_gamma, dk=dk, dv=dv)
    return y.reshape(B * L, vd), p['out_proj'].astype(BF16)


def _gla_kernel(q_ref, k_ref, v_ref, r_ref, glr_ref, gup_ref, gb_ref, nw_ref, o_ref, s_ref,
                *, H, dk, dv, scale):
    C = GLA_CHUNK
    SB = GLA_SUB

    @pl.when(pl.program_id(1) == 0)
    def _():
        s_ref[...] = jnp.zeros_like(s_ref)

    gate = jax.nn.log_sigmoid(_dot(glr_ref[...], gup_ref[...]) + gb_ref[...]) / GLA_TAU
    gc_all = _cumsum_rows(gate)
    row_id = lax.broadcasted_iota(jnp.int32, (C, 1), 0)
    sub_row = lax.broadcasted_iota(jnp.int32, (SB, 1), 0)
    col_id = lax.broadcasted_iota(jnp.int32, (1, C), 1)
    heads = range(H)
    qs = [q_ref[:, hd * dk:(hd + 1) * dk].astype(F32) * scale for hd in heads]
    ks = [k_ref[:, hd * dk:(hd + 1) * dk].astype(F32) for hd in heads]
    gcs = [gc_all[:, hd * dk:(hd + 1) * dk] for hd in heads]
    vs = [v_ref[:, hd * dv:(hd + 1) * dv] for hd in heads]
    s0s = [s_ref[hd] for hd in heads]
    a_rows = [[] for _ in heads]
    for i in range(C // SB):
        lo = i * SB
        for hd in heads:
            qi = qs[hd][lo:lo + SB]
            gi = gcs[hd][lo:lo + SB]
            ki = ks[hd][lo:lo + SB]
            cols = []
            for j in range(SB):
                d = jnp.where(sub_row >= j, gi - gi[j:j + 1], -jnp.inf)
                col = jnp.sum(qi * ki[j:j + 1] * jnp.exp(d), axis=-1, keepdims=True)
                cols.append(jnp.where(col_id == lo + j, col, 0.0))
            blk = _tree_sum(cols)
            if i > 0:
                ref = gcs[hd][lo - 1:lo]
                qt = qi * jnp.exp(gi - ref)
                kt = ks[hd] * jnp.exp(jnp.where(row_id < lo, ref - gcs[hd], -jnp.inf))
                blk = blk + jnp.where(col_id < lo, _dot_nt(qt, kt), 0.0)
            a_rows[hd].append(blk)
    os_ = [_dot(jnp.concatenate(a_rows[hd], axis=0), vs[hd]) + _dot_nt(qs[hd] * jnp.exp(gcs[hd]), s0s[hd])
           for hd in heads]
    for hd in heads:
        last = gcs[hd][C - 1:C]
        s_ref[hd] = s0s[hd] * jnp.exp(last) + _dot_tn(vs[hd], ks[hd] * jnp.exp(last - gcs[hd]))
    nw = nw_ref[...]
    outs = []
    for hd in heads:
        o = os_[hd]
        o = o * lax.rsqrt(jnp.mean(o * o, axis=-1, keepdims=True) + NORM_EPS) * nw
        outs.append(o * _silu(r_ref[:, hd * dv:(hd + 1) * dv].astype(F32)))
    o_ref[...] = jnp.concatenate(outs, axis=-1).astype(o_ref.dtype)


def gla_scan(proj, gate_up, gate_bias, norm_w, *, dk, dv, lora_pad):
    B, L, _ = proj.shape
    H, C = GLA_HEADS, GLA_CHUNK
    qk = lambda j: pl.BlockSpec((None, C, H * dk), lambda b, c: (b, c, j))
    vr = lambda j: pl.BlockSpec((None, C, H * dv), lambda b, c: (b, c, j))
    g0 = (2 * H * dk + 2 * H * dv) // lora_pad
    full = lambda x: pl.BlockSpec(x.shape, lambda b, c: (0, 0))
    consts = (gate_up, gate_bias.reshape(1, -1), norm_w.reshape(1, -1))
    return pl.pallas_call(
        functools.partial(_gla_kernel, H=H, dk=dk, dv=dv, scale=dk ** -0.5),
        grid=(B, L // C),
        in_specs=[qk(0), qk(1), vr(1), vr(2),
                  pl.BlockSpec((None, C, lora_pad), lambda b, c: (b, c, g0))] + [full(x) for x in consts],
        out_specs=vr(0),
        out_shape=jax.ShapeDtypeStruct((B, L, H * dv), BF16),
        scratch_shapes=[pltpu.VMEM((H, dv, dk), F32)],
        compiler_params=_cparams("parallel", "arbitrary"),
        name="gla_scan",
    )(proj, proj, proj, proj, proj, *consts)


def gla_layer(h, nw, p):
    B, L, D = h.shape
    lora, qk = p['gate_up'].shape
    vd = p['out_proj'].shape[0]
    assert 2 * qk == vd
    lora_pad = 128
    w_in = jnp.pad(p['in_proj'], ((0, 0), (0, lora_pad - lora)))
    n_all = w_in.shape[1]
    gate_up = jnp.pad(p['gate_up'], ((0, lora_pad - lora), (0, 0))).astype(BF16)
    h2 = h.reshape(B * L, D)
    proj = norm_linear(h2, nw, w_in.astype(BF16), n_all, BF16).reshape(B, L, n_all)
    o = gla_scan(proj, gate_up, p['gate_bias'], p['norm_w'],
                 dk=qk // GLA_HEADS, dv=vd // GLA_HEADS, lora_pad=lora_pad)
    return o.reshape(B * L, vd), p['out_proj'].astype(BF16)


def kernel(x, meta_tokens, norm_mix, norm_mlp, norm_final, mlp_w_in, mlp_w_out, rwkv_mu, rwkv_w_r, rwkv_w_k, rwkv_w_v, rwkv_w0, rwkv_w_lora_a, rwkv_w_lora_b, rwkv_a0, rwkv_a_lora_a, rwkv_a_lora_b, rwkv_g_lora_a, rwkv_g_lora_b, rwkv_k_k, rwkv_k_a, rwkv_r_k, rwkv_ln_w, rwkv_ln_b, rwkv_w_o, m2_in_proj, m2_conv_w, m2_conv_b, m2_dt_bias, m2_a_log, m2_d, m2_norm_w, m2_out_proj, gla_in_proj, gla_gate_up, gla_gate_bias, gla_norm_w, gla_out_proj, ret_in_proj, ret_out_proj):
    B, S, D = x.shape
    depth = norm_mix.shape[0]
    n_tok = N_META + S
    L = -(-n_tok // SEQ_ALIGN) * SEQ_ALIGN
    meta = jnp.broadcast_to(meta_tokens[None].astype(x.dtype), (B, N_META, D))
    h = jnp.concatenate([meta, x, jnp.zeros((B, L - n_tok, D), x.dtype)], axis=1)
    for i in range(depth):
        m, j = i % 4, i // 4
        if m == 0:
            a, wo = rwkv_layer(h, norm_mix[i], dict(
                mu=rwkv_mu[j], w_r=rwkv_w_r[j], w_k=rwkv_w_k[j], w_v=rwkv_w_v[j], w0=rwkv_w0[j],
                w_lora_a=rwkv_w_lora_a[j], w_lora_b=rwkv_w_lora_b[j], a0=rwkv_a0[j],
                a_lora_a=rwkv_a_lora_a[j], a_lora_b=rwkv_a_lora_b[j],
                g_lora_a=rwkv_g_lora_a[j], g_lora_b=rwkv_g_lora_b[j],
                k_k=rwkv_k_k[j], k_a=rwkv_k_a[j], r_k=rwkv_r_k[j],
                ln_w=rwkv_ln_w[j], ln_b=rwkv_ln_b[j], w_o=rwkv_w_o[j]))
        elif m == 1:
            a, wo = mamba2_layer(h, norm_mix[i], dict(
                in_proj=m2_in_proj[j], conv_w=m2_conv_w[j], conv_b=m2_conv_b[j],
                dt_bias=m2_dt_bias[j], a_log=m2_a_log[j], d=m2_d[j], norm_w=m2_norm_w[j],
                out_proj=m2_out_proj[j]))
        elif m == 2:
            a, wo = gla_layer(h, norm_mix[i], dict(
                in_proj=gla_in_proj[j], gate_up=gla_gate_up[j], gate_bias=gla_gate_bias[j],
                norm_w=gla_norm_w[j], out_proj=gla_out_proj[j]))
        else:
            a, wo = retnet_layer(h, norm_mix[i], dict(in_proj=ret_in_proj[j], out_proj=ret_out_proj[j]))
        h = out_mlp(a, wo, h.reshape(B * L, D), norm_mlp[i], mlp_w_in[i].astype(BF16),
                    mlp_w_out[i].astype(BF16), norm_final, final=(i == depth - 1)).reshape(B, L, D)
    return h[:, N_META:n_tok]
```

```python
import functools
import math

import jax
import jax.numpy as jnp
from jax import lax
from jax.experimental import pallas as pl
from jax.experimental.pallas import tpu as pltpu

F32 = jnp.float32
BF16 = jnp.bfloat16

N_META = 16
NORM_EPS = 1e-5
SEQ_ALIGN = 192
MAX_ROW_TILE = 640
MAX_WIDE_ROW_TILE = 1152
BF16_SUBLANES = 16
F32_SUBLANES = 8
VMEM_LIMIT = 56 * 1024 * 1024

RWKV_HEAD = 64
RWKV_GN_EPS = 64e-5
RWKV_CHUNK = 64
RWKV_PAIRS_PER_BATCH = 8

M2_HEAD = 64
M2_GROUPS = 8
M2_STATE = 128
M2_CONV = 4
M2_CHUNK = 192
M2_GROUPS_PER_STEP = 4

GLA_HEADS = 4
GLA_TAU = 16.0
GLA_CHUNK = 64
GLA_SUB = 8

RET_HEADS = 4
RET_CHUNK = 192
ROPE_BASE = 10000.0


def _cparams(*sem):
    return pltpu.CompilerParams(dimension_semantics=sem, vmem_limit_bytes=VMEM_LIMIT)


def _row_tile(rows, limit=MAX_ROW_TILE):
    for t in range(min(rows, limit) // BF16_SUBLANES * BF16_SUBLANES, 0, -BF16_SUBLANES):
        if rows % t == 0:
            return t
    raise ValueError(rows)


def _dot(a, b):
    return jnp.dot(a.astype(BF16), b.astype(BF16), preferred_element_type=F32)


def _dot_nt(a, b):
    return lax.dot_general(a.astype(BF16), b.astype(BF16), (((1,), (1,)), ((), ())),
                           preferred_element_type=F32)


def _dot_tn(a, b):
    return lax.dot_general(a.astype(BF16), b.astype(BF16), (((0,), (0,)), ((), ())),
                           preferred_element_type=F32)


def _split3(x):
    hi = x.astype(BF16)
    r1 = x - hi.astype(F32)
    mid = r1.astype(BF16)
    return hi, mid, (r1 - mid.astype(F32)).astype(BF16)


def _cumsum_rows(x):
    C = x.shape[0]
    r = lax.broadcasted_iota(jnp.int32, (C, 3 * C), 0)
    c = lax.broadcasted_iota(jnp.int32, (C, 3 * C), 1)
    tri3 = (r >= lax.rem(c, C)).astype(BF16)
    return jnp.dot(tri3, jnp.concatenate(_split3(x), axis=0), preferred_element_type=F32)


def _cumsum_lanes(x):
    C = x.shape[1]
    r = lax.broadcasted_iota(jnp.int32, (3 * C, C), 0)
    c = lax.broadcasted_iota(jnp.int32, (3 * C, C), 1)
    tri3 = (lax.rem(r, C) <= c).astype(BF16)
    return jnp.dot(jnp.concatenate(_split3(x), axis=1), tri3, preferred_element_type=F32)


def _rms_rows(x, w):
    return x * lax.rsqrt(jnp.mean(x * x, axis=-1, keepdims=True) + NORM_EPS) * w


def _tri(n, strict=False, upper=False):
    r = lax.broadcasted_iota(jnp.int32, (n, n), 0)
    c = lax.broadcasted_iota(jnp.int32, (n, n), 1)
    if upper:
        r, c = c, r
    return (r > c) if strict else (r >= c)


def _silu(x):
    return x * jax.nn.sigmoid(x)


def _tree_sum(xs):
    while len(xs) > 1:
        xs = [xs[i] + xs[i + 1] if i + 1 < len(xs) else xs[i] for i in range(0, len(xs), 2)]
    return xs[0]


def _norm_linear_kernel(h_ref, nw_ref, w_ref, o_ref, u_ref):
    @pl.when(pl.program_id(1) == 0)
    def _():
        u_ref[...] = _rms_rows(h_ref[...], nw_ref[...]).astype(BF16)

    o_ref[...] = jnp.dot(u_ref[...], w_ref[...], preferred_element_type=F32).astype(o_ref.dtype)


def norm_linear(h, nw, w, tn, out_dtype):
    M, D = h.shape
    N = w.shape[1]
    tm = _row_tile(M, MAX_WIDE_ROW_TILE)
    return pl.pallas_call(
        _norm_linear_kernel,
        grid=(M // tm, N // tn),
        in_specs=[pl.BlockSpec((tm, D), lambda i, j: (i, 0)),
                  pl.BlockSpec((1, D), lambda i, j: (0, 0)),
                  pl.BlockSpec((D, tn), lambda i, j: (0, j))],
        out_specs=pl.BlockSpec((tm, tn), lambda i, j: (i, j)),
        out_shape=jax.ShapeDtypeStruct((M, N), out_dtype),
        scratch_shapes=[pltpu.VMEM((tm, D), BF16)],
        compiler_params=_cparams("parallel", "arbitrary"),
        name="norm_linear",
    )(h, nw.reshape(1, D), w)


def _linear_res_kernel(a_ref, w_ref, h_ref, o_ref):
    o_ref[...] = h_ref[...] + jnp.dot(a_ref[...], w_ref[...], preferred_element_type=F32)


def linear_res(a, w, h):
    M, K = a.shape
    D = w.shape[1]
    tm = _row_tile(M, MAX_WIDE_ROW_TILE)
    return pl.pallas_call(
        _linear_res_kernel,
        grid=(M // tm,),
        in_specs=[pl.BlockSpec((tm, K), lambda i: (i, 0)),
                  pl.BlockSpec((K, D), lambda i: (0, 0)),
                  pl.BlockSpec((tm, D), lambda i: (i, 0))],
        out_specs=pl.BlockSpec((tm, D), lambda i: (i, 0)),
        out_shape=jax.ShapeDtypeStruct((M, D), F32),
        compiler_params=_cparams("parallel"),
        name="linear_res",
    )(a, w, h)


def _mlp_kernel(h_ref, nw_ref, w1_ref, w2_ref, fnw_ref, o_ref, u_ref, *, final):
    f = pl.program_id(1)

    @pl.when(f == 0)
    def _():
        h = h_ref[...]
        u_ref[...] = _rms_rows(h, nw_ref[...]).astype(BF16)
        o_ref[...] = h

    hid = jnp.dot(u_ref[...], w1_ref[...], preferred_element_type=F32)
    hid = jnp.square(jnp.maximum(hid, 0.0))
    o_ref[...] += _dot(hid, w2_ref[...])

    if final:
        @pl.when(f == pl.num_programs(1) - 1)
        def _():
            o_ref[...] = _rms_rows(o_ref[...], fnw_ref[...])


def mlp_res(h, nw, w1, w2, fnw, final, tf=1024):
    M, D = h.shape
    F = w1.shape[1]
    tm = _row_tile(M, MAX_WIDE_ROW_TILE)
    row = pl.BlockSpec((tm, D), lambda i, f: (i, 0))
    vec = pl.BlockSpec((1, D), lambda i, f: (0, 0))
    return pl.pallas_call(
        functools.partial(_mlp_kernel, final=final),
        grid=(M // tm, F // tf),
        in_specs=[row, vec,
                  pl.BlockSpec((D, tf), lambda i, f: (0, f)),
                  pl.BlockSpec((tf, D), lambda i, f: (f, 0)), vec],
        out_specs=row,
        out_shape=jax.ShapeDtypeStruct((M, D), F32),
        scratch_shapes=[pltpu.VMEM((tm, D), BF16)],
        compiler_params=_cparams("parallel", "arbitrary"),
        name="mlp_res",
    )(h, nw.reshape(1, D), w1, w2, fnw.reshape(1, D))


def _rwkv_proj_kernel(h_ref, halo_ref, nw_ref, mu_ref, wr_ref, wk_ref, wv_ref,
                      wla_ref, wlb_ref, ala_ref, alb_ref, gla_ref, glb_ref,
                      w0_ref, a0_ref,
                      r_ref, k_ref, v_ref, lw_ref, a_ref, g_ref, *, tiles_per_batch):
    nw = nw_ref[...]
    u = _rms_rows(h_ref[...], nw)
    first = pl.program_id(0) % tiles_per_batch == 0
    prev_last = _rms_rows(halo_ref[F32_SUBLANES - 1:F32_SUBLANES, :], nw)
    prev_last = jnp.where(first, 0.0, prev_last)
    row = lax.broadcasted_iota(jnp.int32, (u.shape[0], 1), 0)
    xx = jnp.where(row == 0, prev_last, pltpu.roll(u, 1, axis=0)) - u
    mu = mu_ref[...]
    mix = lambda i: (u + xx * mu[i:i + 1, :]).astype(BF16)
    r_ref[...] = jnp.dot(mix(0), wr_ref[...], preferred_element_type=F32).astype(r_ref.dtype)
    k_ref[...] = jnp.dot(mix(2), wk_ref[...], preferred_element_type=F32).astype(k_ref.dtype)
    v_ref[...] = jnp.dot(mix(3), wv_ref[...], preferred_element_type=F32).astype(v_ref.dtype)
    hw = jnp.tanh(jnp.dot(mix(1), wla_ref[...], preferred_element_type=F32))
    w = -jax.nn.softplus(-(w0_ref[...] + _dot(hw, wlb_ref[...]))) - 0.5
    lw_ref[...] = -jnp.exp(w)
    ha = jnp.dot(mix(4), ala_ref[...], preferred_element_type=F32)
    a_ref[...] = jax.nn.sigmoid(a0_ref[...] + _dot(ha, alb_ref[...]))
    hg = jax.nn.sigmoid(jnp.dot(mix(5), gla_ref[...], preferred_element_type=F32))
    g_ref[...] = _dot(hg, glb_ref[...]).astype(g_ref.dtype)


def rwkv_proj(h, rows_per_batch, nw, mu, wr, wk, wv, wla, wlb, ala, alb, gla, glb, w0, a0):
    M, D = h.shape
    tm = _row_tile(rows_per_batch)
    row = pl.BlockSpec((tm, D), lambda i: (i, 0))
    halo = pl.BlockSpec((F32_SUBLANES, D),
                        lambda i: (jnp.maximum(i * (tm // F32_SUBLANES) - 1, 0), 0))
    full = lambda x: pl.BlockSpec(x.shape, lambda i: (0, 0))
    consts = (nw.reshape(1, D), mu, wr, wk, wv, wla, wlb, ala, alb, gla, glb,
              w0.reshape(1, D), a0.reshape(1, D))
    sds = lambda dt: jax.ShapeDtypeStruct((M, D), dt)
    return pl.pallas_call(
        functools.partial(_rwkv_proj_kernel, tiles_per_batch=rows_per_batch // tm),
        grid=(M // tm,),
        in_specs=[row, halo] + [full(c) for c in consts],
        out_specs=[row] * 6,
        out_shape=[sds(BF16), sds(BF16), sds(BF16), sds(F32), sds(F32), sds(BF16)],
        compiler_params=_cparams("parallel"),
        name="rwkv_proj",
    )(h, h, *consts)


def _rwkv_scan_kernel(r_ref, k_ref, v_ref, lw_ref, a_ref, g_ref,
                      kk_ref, ka_ref, rk_ref, lnw_ref, lnb_ref, o_ref, s_ref):
    C = RWKV_CHUNK
    N = RWKV_HEAD
    assert C == N

    @pl.when(pl.program_id(1) == 0)
    def _():
        s_ref[...] = jnp.zeros_like(s_ref)

    W = 2 * N
    lw_all = lw_ref[...]
    cum_all = _cumsum_rows(lw_all)
    lane = lax.broadcasted_iota(jnp.int32, (1, W), 1)
    left = lane < N
    row_c = lax.broadcasted_iota(jnp.int32, (C, 1), 0)
    strict = row_c > (lane & (N - 1))
    incl = row_c >= (lane & (N - 1))
    mask2 = jnp.concatenate([strict, incl], axis=0)
    row_w = lax.broadcasted_iota(jnp.int32, (W, 1), 0)
    bdiag = (row_w < N) == left
    eye = (row_w == lane).astype(F32)
    split = lambda x: jnp.concatenate([jnp.where(left, x, 0.0), jnp.where(left, 0.0, x)], axis=0)

    def head_sum(x):
        sa = jnp.sum(jnp.where(left, x, 0.0), axis=-1, keepdims=True)
        sb = jnp.sum(jnp.where(left, 0.0, x), axis=-1, keepdims=True)
        return jnp.where(left, sa, sb)

    def run(ids):
        pairs = range(len(ids))
        sls = [slice(p * W, (p + 1) * W) for p in ids]
        s0s = [s_ref[p] for p in ids]
        pre = []
        for sl in sls:
            r = r_ref[:, sl].astype(F32)
            k = k_ref[:, sl].astype(F32)
            a = a_ref[:, sl]
            cum = cum_all[:, sl]
            kkr = k * kk_ref[:, sl]
            kk = kkr * lax.rsqrt(jnp.maximum(head_sum(kkr * kkr), 1e-24))
            km = k * (1.0 + (a - 1.0) * ka_ref[:, sl])
            beta = kk * a
            e_neg = jnp.exp(-cum)
            lhs = jnp.concatenate([-kk * jnp.exp(cum - lw_all[:, sl]), r * jnp.exp(cum)], axis=0)
            pre.append((r, km, beta, cum, lhs, beta * e_neg, km * e_neg))
        x_as = [_dot_nt(jnp.where(left, p[4], 0.0), jnp.concatenate([p[5], p[6]], axis=0)) for p in pre]
        x_bs = [_dot_nt(jnp.where(left, 0.0, p[4]), jnp.concatenate([p[6], p[5]], axis=0)) for p in pre]
        xbeta = [jnp.where(left, xa, xb) for xa, xb in zip(x_as, x_bs)]
        xkey = [jnp.where(mask2, jnp.where(left, xb, xa), 0.0) for xa, xb in zip(x_as, x_bs)]
        ps = [split(jnp.where(strict, x[:C], 0.0)) for x in xbeta]
        ts = [eye + p for p in ps]
        n_fac = int(math.log2(C))
        for i in range(1, n_fac):
            if i == 1:
                ps = [_dot(p, p) for p in ps]
            zs = [_dot(p, jnp.concatenate([p, t], axis=-1)) if i < n_fac - 1 else _dot(p, t)
                  for p, t in zip(ps, ts)]
            ts = [t + z[:, -W:] for t, z in zip(ts, zs)]
            ps = [z[:, :W] for z in zs]
        vs = [v_ref[:, sl].astype(F32) for sl in sls]
        vx = [jnp.concatenate([jnp.where(left, 0.0, v), jnp.where(left, v, 0.0)], axis=0) for v in vs]
        hss = [_dot_nt(p[4], s0) + _dot(xk, v) for p, s0, xk, v in zip(pre, s0s, xkey, vx)]
        ubd = [_dot(t, split(hs[:C])) for t, hs in zip(ts, hss)]
        ys = [hs[C:] + _dot(jnp.where(incl, x[C:], 0.0), u) for hs, x, u in zip(hss, xbeta, ubd)]
        for p in pairs:
            r, km, beta, cum, _, _, _ = pre[p]
            last = cum[C - 1:C, :]
            e_last = jnp.exp(last - cum)
            upd = _dot_tn(jnp.concatenate([vs[p], ubd[p][:C] + ubd[p][C:]], axis=0),
                          jnp.concatenate([km * e_last, beta * e_last], axis=0))
            s_ref[ids[p]] = s0s[p] * jnp.exp(last) + jnp.where(bdiag, upd, 0.0)
        outs = []
        for p in pairs:
            sl = sls[p]
            r, km = pre[p][0], pre[p][1]
            y = ys[p]
            mu_y = head_sum(y) * (1.0 / N)
            var = head_sum(jnp.square(y - mu_y)) * (1.0 / N)
            yn = (y - mu_y) * lax.rsqrt(var + RWKV_GN_EPS) * lnw_ref[:, sl] + lnb_ref[:, sl]
            bonus = head_sum(r * km * rk_ref[:, sl]) * vs[p]
            outs.append((yn + bonus) * g_ref[:, sl].astype(F32))
        o_ref[:, ids[0] * W:(ids[-1] + 1) * W] = jnp.concatenate(outs, axis=-1).astype(o_ref.dtype)

    n_pairs = r_ref.shape[-1] // W
    for lo in range(0, n_pairs, RWKV_PAIRS_PER_BATCH):
        run(list(range(lo, min(lo + RWKV_PAIRS_PER_BATCH, n_pairs))))


def rwkv_scan(r, k, v, lw, a, g, k_k, k_a, r_k, ln_w, ln_b):
    B, L, D = r.shape
    C = RWKV_CHUNK
    seq = pl.BlockSpec((None, C, D), lambda b, c: (b, c, 0))
    par = pl.BlockSpec((1, D), lambda b, c: (0, 0))
    return pl.pallas_call(
        _rwkv_scan_kernel,
        grid=(B, L // C),
        in_specs=[seq] * 6 + [par] * 5,
        out_specs=seq,
        out_shape=jax.ShapeDtypeStruct((B, L, D), BF16),
        scratch_shapes=[pltpu.VMEM((D // (2 * RWKV_HEAD), 2 * RWKV_HEAD, 2 * RWKV_HEAD), F32)],
        compiler_params=_cparams("parallel", "arbitrary"),
        name="rwkv_scan",
    )(r, k, v, lw, a, g, *(p.reshape(1, D) for p in (k_k, k_a, r_k, ln_w, ln_b)))


def rwkv_layer(h, nw, p):
    B, L, D = h.shape
    bf = lambda x: x.astype(BF16)
    h2 = h.reshape(B * L, D)
    outs = rwkv_proj(h2, L, nw, p['mu'],
                     bf(p['w_r']), bf(p['w_k']), bf(p['w_v']),
                     bf(p['w_lora_a']), bf(p['w_lora_b']), bf(p['a_lora_a']), bf(p['a_lora_b']),
                     bf(p['g_lora_a']), bf(p['g_lora_b']), p['w0'], p['a0'])
    r, k, v, lw, a, g = (o.reshape(B, L, D) for o in outs)
    o = rwkv_scan(r, k, v, lw, a, g, p['k_k'], p['k_a'], p['r_k'].reshape(D), p['ln_w'], p['ln_b'])
    return o.reshape(B * L, D), bf(p['w_o'])


def _sdecay_groups(qs, ks, vss, ccss, crss, s_ref, P):
    C = qs[0].shape[0]
    causal = _tri(C)
    groups = range(len(qs))
    s0s = [s_ref[g] for g in groups]
    scores = [_dot_nt(qs[g], ks[g]) for g in groups]
    qss = [_dot(qs[g], s0s[g]) for g in groups]
    yss = []
    for g in groups:
        ys = []
        for r, (v, cc, cr) in enumerate(zip(vss[g], ccss[g], crss[g])):
            dec = jnp.exp(jnp.where(causal, cc - cr, -jnp.inf))
            ys.append(_dot(scores[g] * dec, v) + jnp.exp(cc) * qss[g][:, r * P:(r + 1) * P])
        yss.append(ys)
    for g in groups:
        lasts = [cc[C - 1:C, :] for cc in ccss[g]]
        vw = jnp.concatenate([v * jnp.exp(last - cc)
                              for v, cc, last in zip(vss[g], ccss[g], lasts)], axis=-1)
        keep = jnp.concatenate([jnp.broadcast_to(jnp.exp(last), (1, P)) for last in lasts], axis=-1)
        s_ref[g] = keep * s0s[g] + _dot_tn(ks[g], vw)
    return yss


def _m2_scan_kernel(shift_ref, expand_ref, z_ref, xs_ref, xsh_ref, b_ref, bh_ref, c_ref, ch_ref,
                    wx_ref, wb_ref, wc_ref, bx_ref, bb_ref, bc_ref,
                    dtc_ref, dtr_ref, pc_ref, pr_ref, dx_ref, nw_ref, o_ref, s_ref, *, GS, R, P, N, C):
    first = pl.program_id(2) == 0

    @pl.when(first)
    def _():
        s_ref[...] = jnp.zeros_like(s_ref)

    shift = shift_ref[...]

    def conv_silu(x_ref, prev_ref, w_ref, bias_ref):
        x = x_ref[...]
        prev = prev_ref[...]
        xf = jnp.concatenate([jnp.where(first, jnp.zeros_like(prev), prev), x], axis=0)
        taps = jnp.dot(shift, xf, preferred_element_type=F32)
        w = w_ref[...]
        acc = bias_ref[...] + w[M2_CONV - 1:M2_CONV] * x.astype(F32)
        for i in range(M2_CONV - 1):
            acc = acc + w[i:i + 1] * taps[i * C:(i + 1) * C]
        return _silu(acc)

    GP = R * P
    xs = conv_silu(xs_ref, xsh_ref, wx_ref, bx_ref)
    k = conv_silu(b_ref, bh_ref, wb_ref, bb_ref)
    q = conv_silu(c_ref, ch_ref, wc_ref, bc_ref)
    pc = pc_ref[...]
    pr = pr_ref[...]
    dt_c = jax.nn.softplus(dtc_ref[...] + pc[0:1])
    dt_r = jax.nn.softplus(dtr_ref[...] + pr[:, 0:1])
    cc = _cumsum_rows(dt_c * -jnp.exp(pc[1:2]))
    cr = _cumsum_lanes(dt_r * -jnp.exp(pr[:, 1:2]))
    expand = expand_ref[...]

    def widen(x):
        return jnp.dot(jnp.concatenate(_split3(x), axis=-1), expand, preferred_element_type=F32)

    last = cc[C - 1:C]
    e_cum = widen(jnp.exp(cc))
    v = xs * widen(dt_c)
    vw = v * widen(jnp.exp(last - cc))
    keep = e_cum[C - 1:C]
    causal = _tri(C)
    head_of_lane = lax.broadcasted_iota(jnp.int32, (1, GP), 1) // P
    groups = range(GS)
    gsl = [slice(g * GP, (g + 1) * GP) for g in groups]
    qg = [q[:, g * N:(g + 1) * N] for g in groups]
    kg = [k[:, g * N:(g + 1) * N] for g in groups]
    s0s = [s_ref[g] for g in groups]
    scores = [_dot_nt(qg[g], kg[g]) for g in groups]
    qss = [_dot(qg[g], s0s[g]) for g in groups]
    pvs = []
    for g in groups:
        ps = [scores[g] * jnp.exp(jnp.where(causal, cc[:, j:j + 1] - cr[j:j + 1], -jnp.inf))
              for j in range(g * R, (g + 1) * R)]
        vbd = jnp.concatenate([jnp.where(head_of_lane == r, v[:, gsl[g]], 0.0) for r in range(R)], axis=0)
        pvs.append(_dot(jnp.concatenate(ps, axis=-1), vbd))
    for g in groups:
        s_ref[g] = keep[:, gsl[g]] * s0s[g] + _dot_tn(kg[g], vw[:, gsl[g]])
    z = z_ref[...].astype(F32)
    dx = dx_ref[...]
    outs = []
    for g in groups:
        y = pvs[g] + e_cum[:, gsl[g]] * qss[g] + dx[:, gsl[g]] * xs[:, gsl[g]]
        y = y * _silu(z[:, gsl[g]])
        outs.append(y * lax.rsqrt(jnp.mean(y * y, axis=-1, keepdims=True) + NORM_EPS))
    o_ref[...] = (jnp.concatenate(outs, axis=-1) * nw_ref[...]).astype(o_ref.dtype)


def m2_scan(zx, dt, conv_w, conv_b, dt_bias, a_log, d_skip, norm_w, *, d_inner, heads):
    B, L, _ = zx.shape
    G, N, C, GS = M2_GROUPS, M2_STATE, M2_CHUNK, M2_GROUPS_PER_STEP
    R = heads // G
    GP = d_inner // G
    NS = G // GS
    WX, WN, WR = GS * GP, GS * N, GS * R
    x0 = d_inner // WX
    b0 = 2 * d_inner // WN
    c0 = b0 + NS
    wb0 = d_inner // WN
    wc0 = wb0 + NS
    cur = lambda w, off: pl.BlockSpec((None, C, w), lambda b, g, c: (b, c, off + g))
    HB = BF16_SUBLANES
    halo = lambda w, off: pl.BlockSpec(
        (None, HB, w), lambda b, g, c: (b, jnp.maximum(c * (C // HB) - 1, 0), off + g))
    par = lambda rows, w, off: pl.BlockSpec((rows, w), lambda b, g, c: (0, off + g))
    tap_row = jnp.arange((M2_CONV - 1) * C)
    shift = (jnp.arange(HB + C)[None, :] ==
             (HB + tap_row % C - (M2_CONV - 1) + tap_row // C)[:, None]).astype(BF16)
    dt4 = dt.reshape(B, L, NS, WR)
    pcol = jnp.stack([dt_bias, a_log]).reshape(2, NS, WR).transpose(1, 0, 2)
    prow = pcol.transpose(0, 2, 1)
    expand = jnp.tile(jnp.arange(WX)[None, :] // (GP // R) == jnp.arange(WR)[:, None], (3, 1)).astype(BF16)
    cb = conv_b.reshape(1, -1)
    const = lambda x: pl.BlockSpec(x.shape, lambda b, g, c: (0, 0))
    return pl.pallas_call(
        functools.partial(_m2_scan_kernel, GS=GS, R=R, P=GP // R, N=N, C=C),
        grid=(B, NS, L // C),
        in_specs=[const(shift), const(expand),
                  cur(WX, 0), cur(WX, x0), halo(WX, x0), cur(WN, b0), halo(WN, b0), cur(WN, c0), halo(WN, c0),
                  par(M2_CONV, WX, 0), par(M2_CONV, WN, wb0), par(M2_CONV, WN, wc0),
                  par(1, WX, 0), par(1, WN, wb0), par(1, WN, wc0),
                  pl.BlockSpec((None, None, C, WR), lambda b, g, c: (b, g, c, 0)),
                  pl.BlockSpec((None, None, None, WR, C), lambda b, g, c: (b, g, c, 0, 0)),
                  pl.BlockSpec((None, 2, WR), lambda b, g, c: (g, 0, 0)),
                  pl.BlockSpec((None, WR, 2), lambda b, g, c: (g, 0, 0)),
                  par(1, WX, 0), par(1, WX, 0)],
        out_specs=cur(WX, 0),
        out_shape=jax.ShapeDtypeStruct((B, L, d_inner), BF16),
        scratch_shapes=[pltpu.VMEM((GS, N, GP), F32)],
        compiler_params=_cparams("parallel", "parallel", "arbitrary"),
        name="m2_scan",
    )(shift, expand, zx, zx, zx, zx, zx, zx, zx, conv_w, conv_w, conv_w, cb, cb, cb,
      dt4.transpose(0, 2, 1, 3), dt4.reshape(B, L // C, C, NS, WR).transpose(0, 3, 1, 4, 2), pcol, prow,
      jnp.repeat(d_skip, GP // R).reshape(1, -1), norm_w.reshape(1, -1))


def mamba2_layer(h, nw, p):
    B, L, D = h.shape
    d_inner = p['norm_w'].shape[0]
    heads = p['a_log'].shape[0]
    w_in = p['in_proj']
    n_main = w_in.shape[1] - heads
    w_dt = jnp.pad(w_in[:, n_main:], ((0, 0), (0, 128 - heads)))
    h2 = h.reshape(B * L, D)
    zx = norm_linear(h2, nw, w_in[:, :n_main].astype(BF16), 2048, BF16).reshape(B, L, n_main)
    dt = norm_linear(h2, nw, w_dt.astype(BF16), 128, F32)[:, :heads].reshape(B, L, heads)
    y = m2_scan(zx, dt, p['conv_w'], p['conv_b'], p['dt_bias'], p['a_log'], p['d'], p['norm_w'],
                d_inner=d_inner, heads=heads)
    return y.reshape(B * L, d_inner), p['out_proj'].astype(BF16)


def _ret_scan_kernel(q_ref, k_ref, v_ref, g_ref, cos_ref, sin_ref, lg_ref, o_ref, s_ref,
                     *, H, dk, dv, C, scale):
    @pl.when(pl.program_id(1) == 0)
    def _():
        s_ref[...] = jnp.zeros_like(s_ref)

    cos = cos_ref[...]
    sin = sin_ref[...]
    half = dk // 2

    def rotary(x_ref, hd):
        x1 = x_ref[:, hd * dk:hd * dk + half].astype(F32)
        x2 = x_ref[:, hd * dk + half:(hd + 1) * dk].astype(F32)
        return jnp.concatenate([x1 * cos - x2 * sin, x1 * sin + x2 * cos], axis=-1)

    lg = lg_ref[...]
    steps_c = (lax.broadcasted_iota(jnp.int32, (C, 1), 0) + 1).astype(F32)
    steps_r = (lax.broadcasted_iota(jnp.int32, (1, C), 1) + 1).astype(F32)
    yss = _sdecay_groups(
        [rotary(q_ref, hd) for hd in range(H)], [rotary(k_ref, hd) * scale for hd in range(H)],
        [[v_ref[:, hd * dv:(hd + 1) * dv]] for hd in range(H)],
        [[steps_c * lg[:, hd:hd + 1]] for hd in range(H)],
        [[steps_r * lg[:, hd:hd + 1]] for hd in range(H)], s_ref, dv)
    outs = []
    for hd in range(H):
        y = yss[hd][0]
        mu = jnp.mean(y, axis=-1, keepdims=True)
        var = jnp.mean(jnp.square(y - mu), axis=-1, keepdims=True)
        outs.append(_silu(g_ref[:, hd * dv:(hd + 1) * dv].astype(F32)) * ((y - mu) * lax.rsqrt(var + NORM_EPS)))
    o_ref[...] = jnp.concatenate(outs, axis=-1).astype(o_ref.dtype)


def ret_scan(proj, cos, sin, log_gamma, *, dk, dv):
    B, L, _ = proj.shape
    H, C = RET_HEADS, RET_CHUNK
    qk = lambda j: pl.BlockSpec((None, C, H * dk), lambda b, c: (b, c, j))
    vg = lambda j: pl.BlockSpec((None, C, H * dv), lambda b, c: (b, c, j))
    tab = pl.BlockSpec((C, dk // 2), lambda b, c: (c, 0))
    return pl.pallas_call(
        functools.partial(_ret_scan_kernel, H=H, dk=dk, dv=dv, C=C, scale=dk ** -0.5),
        grid=(B, L // C),
        in_specs=[qk(0), qk(1), vg(1), vg(2), tab, tab, pl.BlockSpec((1, H), lambda b, c: (0, 0))],
        out_specs=vg(0),
        out_shape=jax.ShapeDtypeStruct((B, L, H * dv), BF16),
        scratch_shapes=[pltpu.VMEM((H, dk, dv), F32)],
        compiler_params=_cparams("parallel", "arbitrary"),
        name="ret_scan",
    )(proj, proj, proj, proj, cos, sin, log_gamma.reshape(1, H))


def retnet_layer(h, nw, p):
    B, L, D = h.shape
    w_in = p['in_proj']
    vd = p['out_proj'].shape[0]
    dv = vd // RET_HEADS
    dk = (w_in.shape[1] - 2 * vd) // (2 * RET_HEADS)
    assert 2 * RET_HEADS * dk == vd
    h2 = h.reshape(B * L, D)
    proj = norm_linear(h2, nw, w_in.astype(BF16), 2048, BF16).reshape(B, L, -1)
    half = dk // 2
    inv_freq = 1.0 / (ROPE_BASE ** jnp.linspace(0.0, 1.0, half, dtype=F32))
    ang = jnp.arange(L, dtype=F32)[:, None] * inv_freq[None, :]
    log_gamma = jnp.log1p(-jnp.exp2(-5.0 - jnp.arange(RET_HEADS, dtype=F32)))
    y = ret_scan(proj, jnp.cos(ang), jnp.sin(ang), log_gamma, dk=dk, dv=dv)
    return y.reshape(B * L, vd), p['out_proj'].astype(BF16)


def _gla_kernel(q_ref, k_ref, v_ref, r_ref, glr_ref, gup_ref, gb_ref, nw_ref, o_ref, s_ref,
                *, H, dk, dv, scale):
    C = GLA_CHUNK
    SB = GLA_SUB

    @pl.when(pl.program_id(1) == 0)
    def _():
        s_ref[...] = jnp.zeros_like(s_ref)

    gate = jax.nn.log_sigmoid(_dot(glr_ref[...], gup_ref[...]) + gb_ref[...]) / GLA_TAU
    gc_all = _cumsum_rows(gate)
    row_id = lax.broadcasted_iota(jnp.int32, (C, 1), 0)
    sub_row = lax.broadcasted_iota(jnp.int32, (SB, 1), 0)
    col_id = lax.broadcasted_iota(jnp.int32, (1, C), 1)
    heads = range(H)
    qs = [q_ref[:, hd * dk:(hd + 1) * dk].astype(F32) * scale for hd in heads]
    ks = [k_ref[:, hd * dk:(hd + 1) * dk].astype(F32) for hd in heads]
    gcs = [gc_all[:, hd * dk:(hd + 1) * dk] for hd in heads]
    vs = [v_ref[:, hd * dv:(hd + 1) * dv] for hd in heads]
    s0s = [s_ref[hd] for hd in heads]
    a_rows = [[] for _ in heads]
    for i in range(C // SB):
        lo = i * SB
        for hd in heads:
            qi = qs[hd][lo:lo + SB]
            gi = gcs[hd][lo:lo + SB]
            ki = ks[hd][lo:lo + SB]
            cols = []
            for j in range(SB):
                d = jnp.where(sub_row >= j, gi - gi[j:j + 1], -jnp.inf)
                col = jnp.sum(qi * ki[j:j + 1] * jnp.exp(d), axis=-1, keepdims=True)
                cols.append(jnp.where(col_id == lo + j, col, 0.0))
            blk = _tree_sum(cols)
            if i > 0:
                ref = gcs[hd][lo - 1:lo]
                qt = qi * jnp.exp(gi - ref)
                kt = ks[hd] * jnp.exp(jnp.where(row_id < lo, ref - gcs[hd], -jnp.inf))
                blk = blk + jnp.where(col_id < lo, _dot_nt(qt, kt), 0.0)
            a_rows[hd].append(blk)
    os_ = [_dot(jnp.concatenate(a_rows[hd], axis=0), vs[hd]) + _dot_nt(qs[hd] * jnp.exp(gcs[hd]), s0s[hd])
           for hd in heads]
    for hd in heads:
        last = gcs[hd][C - 1:C]
        s_ref[hd] = s0s[hd] * jnp.exp(last) + _dot_tn(vs[hd], ks[hd] * jnp.exp(last - gcs[hd]))
    nw = nw_ref[...]
    outs = []
    for hd in heads:
        o = os_[hd]
        o = o * lax.rsqrt(jnp.mean(o * o, axis=-1, keepdims=True) + NORM_EPS) * nw
        outs.append(o * _silu(r_ref[:, hd * dv:(hd + 1) * dv].astype(F32)))
    o_ref[...] = jnp.concatenate(outs, axis=-1).astype(o_ref.dtype)


def gla_scan(proj, gate_up, gate_bias, norm_w, *, dk, dv, lora_pad):
    B, L, _ = proj.shape
    H, C = GLA_HEADS, GLA_CHUNK
    qk = lambda j: pl.BlockSpec((None, C, H * dk), lambda b, c: (b, c, j))
    vr = lambda j: pl.BlockSpec((None, C, H * dv), lambda b, c: (b, c, j))
    g0 = (2 * H * dk + 2 * H * dv) // lora_pad
    full = lambda x: pl.BlockSpec(x.shape, lambda b, c: (0, 0))
    consts = (gate_up, gate_bias.reshape(1, -1), norm_w.reshape(1, -1))
    return pl.pallas_call(
        functools.partial(_gla_kernel, H=H, dk=dk, dv=dv, scale=dk ** -0.5),
        grid=(B, L // C),
        in_specs=[qk(0), qk(1), vr(1), vr(2),
                  pl.BlockSpec((None, C, lora_pad), lambda b, c: (b, c, g0))] + [full(x) for x in consts],
        out_specs=vr(0),
        out_shape=jax.ShapeDtypeStruct((B, L, H * dv), BF16),
        scratch_shapes=[pltpu.VMEM((H, dv, dk), F32)],
        compiler_params=_cparams("parallel", "arbitrary"),
        name="gla_scan",
    )(proj, proj, proj, proj, proj, *consts)


def gla_layer(h, nw, p):
    B, L, D = h.shape
    lora, qk = p['gate_up'].shape
    vd = p['out_proj'].shape[0]
    assert 2 * qk == vd
    lora_pad = 128
    w_in = jnp.pad(p['in_proj'], ((0, 0), (0, lora_pad - lora)))
    n_all = w_in.shape[1]
    gate_up = jnp.pad(p['gate_up'], ((0, lora_pad - lora), (0, 0))).astype(BF16)
    h2 = h.reshape(B * L, D)
    proj = norm_linear(h2, nw, w_in.astype(BF16), n_all, BF16).reshape(B, L, n_all)
    o = gla_scan(proj, gate_up, p['gate_bias'], p['norm_w'],
                 dk=qk // GLA_HEADS, dv=vd // GLA_HEADS, lora_pad=lora_pad)
    return o.reshape(B * L, vd), p['out_proj'].astype(BF16)


def kernel(x, meta_tokens, norm_mix, norm_mlp, norm_final, mlp_w_in, mlp_w_out, rwkv_mu, rwkv_w_r, rwkv_w_k, rwkv_w_v, rwkv_w0, rwkv_w_lora_a, rwkv_w_lora_b, rwkv_a0, rwkv_a_lora_a, rwkv_a_lora_b, rwkv_g_lora_a, rwkv_g_lora_b, rwkv_k_k, rwkv_k_a, rwkv_r_k, rwkv_ln_w, rwkv_ln_b, rwkv_w_o, m2_in_proj, m2_conv_w, m2_conv_b, m2_dt_bias, m2_a_log, m2_d, m2_norm_w, m2_out_proj, gla_in_proj, gla_gate_up, gla_gate_bias, gla_norm_w, gla_out_proj, ret_in_proj, ret_out_proj):
    B, S, D = x.shape
    depth = norm_mix.shape[0]
    n_tok = N_META + S
    L = -(-n_tok // SEQ_ALIGN) * SEQ_ALIGN
    meta = jnp.broadcast_to(meta_tokens[None].astype(x.dtype), (B, N_META, D))
    h = jnp.concatenate([meta, x, jnp.zeros((B, L - n_tok, D), x.dtype)], axis=1)
    for i in range(depth):
        m, j = i % 4, i // 4
        if m == 0:
            a, wo = rwkv_layer(h, norm_mix[i], dict(
                mu=rwkv_mu[j], w_r=rwkv_w_r[j], w_k=rwkv_w_k[j], w_v=rwkv_w_v[j], w0=rwkv_w0[j],
                w_lora_a=rwkv_w_lora_a[j], w_lora_b=rwkv_w_lora_b[j], a0=rwkv_a0[j],
                a_lora_a=rwkv_a_lora_a[j], a_lora_b=rwkv_a_lora_b[j],
                g_lora_a=rwkv_g_lora_a[j], g_lora_b=rwkv_g_lora_b[j],
                k_k=rwkv_k_k[j], k_a=rwkv_k_a[j], r_k=rwkv_r_k[j],
                ln_w=rwkv_ln_w[j], ln_b=rwkv_ln_b[j], w_o=rwkv_w_o[j]))
        elif m == 1:
            a, wo = mamba2_layer(h, norm_mix[i], dict(
                in_proj=m2_in_proj[j], conv_w=m2_conv_w[j], conv_b=m2_conv_b[j],
                dt_bias=m2_dt_bias[j], a_log=m2_a_log[j], d=m2_d[j], norm_w=m2_norm_w[j],
                out_proj=m2_out_proj[j]))
        elif m == 2:
            a, wo = gla_layer(h, norm_mix[i], dict(
                in_proj=gla_in_proj[j], gate_up=gla_gate_up[j], gate_bias=gla_gate_bias[j],
                norm_w=gla_norm_w[j], out_proj=gla_out_proj[j]))
        else:
            a, wo = retnet_layer(h, norm_mix[i], dict(in_proj=ret_in_proj[j], out_proj=ret_out_proj[j]))
        h = mlp_res(linear_res(a, wo, h.reshape(B * L, D)), norm_mlp[i], mlp_w_in[i].astype(BF16),
                    mlp_w_out[i].astype(BF16), norm_final, final=(i == depth - 1)).reshape(B, L, D)
    return h[:, N_META:n_tok]
```

```python
import functools
import math

import jax
import jax.numpy as jnp
from jax import lax
from jax.experimental import pallas as pl
from jax.experimental.pallas import tpu as pltpu

F32 = jnp.float32
BF16 = jnp.bfloat16

N_META = 16
NORM_EPS = 1e-5
SEQ_ALIGN = 192
MAX_ROW_TILE = 640
MAX_WIDE_ROW_TILE = 1152
BF16_SUBLANES = 16
F32_SUBLANES = 8
VMEM_LIMIT = 56 * 1024 * 1024

RWKV_HEAD = 64
RWKV_GN_EPS = 64e-5
RWKV_CHUNK = 64
RWKV_SUBCHUNKS = 3

M2_HEAD = 64
M2_GROUPS = 8
M2_STATE = 128
M2_CONV = 4
M2_CHUNK = 192
M2_GROUPS_PER_STEP = 4

GLA_HEADS = 4
GLA_TAU = 16.0
GLA_CHUNK = 64
GLA_SUBCHUNKS = 3
GLA_SUB = 8

RET_HEADS = 4
RET_CHUNK = 192
ROPE_BASE = 10000.0


def _cparams(*sem):
    return pltpu.CompilerParams(dimension_semantics=sem, vmem_limit_bytes=VMEM_LIMIT)


def _row_tile(rows, limit=MAX_ROW_TILE):
    for t in range(min(rows, limit) // BF16_SUBLANES * BF16_SUBLANES, 0, -BF16_SUBLANES):
        if rows % t == 0:
            return t
    raise ValueError(rows)


def _dot(a, b):
    return jnp.dot(a.astype(BF16), b.astype(BF16), preferred_element_type=F32)


def _dot_nt(a, b):
    return lax.dot_general(a.astype(BF16), b.astype(BF16), (((1,), (1,)), ((), ())),
                           preferred_element_type=F32)


def _dot_tn(a, b):
    return lax.dot_general(a.astype(BF16), b.astype(BF16), (((0,), (0,)), ((), ())),
                           preferred_element_type=F32)


def _split3(x):
    hi = x.astype(BF16)
    r1 = x - hi.astype(F32)
    mid = r1.astype(BF16)
    return hi, mid, (r1 - mid.astype(F32)).astype(BF16)


def _cumsum_rows(x):
    C = x.shape[0]
    r = lax.broadcasted_iota(jnp.int32, (C, 3 * C), 0)
    c = lax.broadcasted_iota(jnp.int32, (C, 3 * C), 1)
    tri3 = (r >= lax.rem(c, C)).astype(BF16)
    return jnp.dot(tri3, jnp.concatenate(_split3(x), axis=0), preferred_element_type=F32)


def _cumsum_lanes(x):
    C = x.shape[1]
    r = lax.broadcasted_iota(jnp.int32, (3 * C, C), 0)
    c = lax.broadcasted_iota(jnp.int32, (3 * C, C), 1)
    tri3 = (lax.rem(r, C) <= c).astype(BF16)
    return jnp.dot(jnp.concatenate(_split3(x), axis=1), tri3, preferred_element_type=F32)


def _rms_rows(x, w):
    return x * lax.rsqrt(jnp.mean(x * x, axis=-1, keepdims=True) + NORM_EPS) * w


def _tri(n, strict=False, upper=False):
    r = lax.broadcasted_iota(jnp.int32, (n, n), 0)
    c = lax.broadcasted_iota(jnp.int32, (n, n), 1)
    if upper:
        r, c = c, r
    return (r > c) if strict else (r >= c)


def _silu(x):
    return x * jax.nn.sigmoid(x)


def _tree_sum(xs):
    while len(xs) > 1:
        xs = [xs[i] + xs[i + 1] if i + 1 < len(xs) else xs[i] for i in range(0, len(xs), 2)]
    return xs[0]


def _norm_linear_kernel(h_ref, nw_ref, w_ref, o_ref, u_ref):
    @pl.when(pl.program_id(1) == 0)
    def _():
        u_ref[...] = _rms_rows(h_ref[...], nw_ref[...]).astype(BF16)

    o_ref[...] = jnp.dot(u_ref[...], w_ref[...], preferred_element_type=F32).astype(o_ref.dtype)


def norm_linear(h, nw, w, tn, out_dtype):
    M, D = h.shape
    N = w.shape[1]
    tm = _row_tile(M, MAX_WIDE_ROW_TILE)
    return pl.pallas_call(
        _norm_linear_kernel,
        grid=(M // tm, N // tn),
        in_specs=[pl.BlockSpec((tm, D), lambda i, j: (i, 0)),
                  pl.BlockSpec((1, D), lambda i, j: (0, 0)),
                  pl.BlockSpec((D, tn), lambda i, j: (0, j))],
        out_specs=pl.BlockSpec((tm, tn), lambda i, j: (i, j)),
        out_shape=jax.ShapeDtypeStruct((M, N), out_dtype),
        scratch_shapes=[pltpu.VMEM((tm, D), BF16)],
        compiler_params=_cparams("parallel", "arbitrary"),
        name="norm_linear",
    )(h, nw.reshape(1, D), w)


def _linear_res_kernel(a_ref, w_ref, h_ref, o_ref):
    o_ref[...] = h_ref[...] + jnp.dot(a_ref[...], w_ref[...], preferred_element_type=F32)


def linear_res(a, w, h):
    M, K = a.shape
    D = w.shape[1]
    tm = _row_tile(M, MAX_WIDE_ROW_TILE)
    return pl.pallas_call(
        _linear_res_kernel,
        grid=(M // tm,),
        in_specs=[pl.BlockSpec((tm, K), lambda i: (i, 0)),
                  pl.BlockSpec((K, D), lambda i: (0, 0)),
                  pl.BlockSpec((tm, D), lambda i: (i, 0))],
        out_specs=pl.BlockSpec((tm, D), lambda i: (i, 0)),
        out_shape=jax.ShapeDtypeStruct((M, D), F32),
        compiler_params=_cparams("parallel"),
        name="linear_res",
    )(a, w, h)


def _mlp_kernel(h_ref, nw_ref, w1_ref, w2_ref, fnw_ref, o_ref, u_ref, *, final):
    f = pl.program_id(1)

    @pl.when(f == 0)
    def _():
        h = h_ref[...]
        u_ref[...] = _rms_rows(h, nw_ref[...]).astype(BF16)
        o_ref[...] = h

    hid = jnp.dot(u_ref[...], w1_ref[...], preferred_element_type=F32)
    hid = jnp.square(jnp.maximum(hid, 0.0))
    o_ref[...] += _dot(hid, w2_ref[...])

    if final:
        @pl.when(f == pl.num_programs(1) - 1)
        def _():
            o_ref[...] = _rms_rows(o_ref[...], fnw_ref[...])


def mlp_res(h, nw, w1, w2, fnw, final, tf=1024):
    M, D = h.shape
    F = w1.shape[1]
    tm = _row_tile(M, MAX_WIDE_ROW_TILE)
    row = pl.BlockSpec((tm, D), lambda i, f: (i, 0))
    vec = pl.BlockSpec((1, D), lambda i, f: (0, 0))
    return pl.pallas_call(
        functools.partial(_mlp_kernel, final=final),
        grid=(M // tm, F // tf),
        in_specs=[row, vec,
                  pl.BlockSpec((D, tf), lambda i, f: (0, f)),
                  pl.BlockSpec((tf, D), lambda i, f: (f, 0)), vec],
        out_specs=row,
        out_shape=jax.ShapeDtypeStruct((M, D), F32),
        scratch_shapes=[pltpu.VMEM((tm, D), BF16)],
        compiler_params=_cparams("parallel", "arbitrary"),
        name="mlp_res",
    )(h, nw.reshape(1, D), w1, w2, fnw.reshape(1, D))


def _rwkv_proj_kernel(h_ref, halo_ref, nw_ref, mu_ref, wr_ref, wk_ref, wv_ref,
                      wla_ref, wlb_ref, ala_ref, alb_ref, gla_ref, glb_ref,
                      w0_ref, a0_ref,
                      r_ref, k_ref, v_ref, lw_ref, a_ref, g_ref, *, tiles_per_batch):
    nw = nw_ref[...]
    u = _rms_rows(h_ref[...], nw)
    first = pl.program_id(0) % tiles_per_batch == 0
    prev_last = _rms_rows(halo_ref[F32_SUBLANES - 1:F32_SUBLANES, :], nw)
    prev_last = jnp.where(first, 0.0, prev_last)
    row = lax.broadcasted_iota(jnp.int32, (u.shape[0], 1), 0)
    xx = jnp.where(row == 0, prev_last, pltpu.roll(u, 1, axis=0)) - u
    mu = mu_ref[...]
    mix = lambda i: (u + xx * mu[i:i + 1, :]).astype(BF16)
    r_ref[...] = jnp.dot(mix(0), wr_ref[...], preferred_element_type=F32).astype(r_ref.dtype)
    k_ref[...] = jnp.dot(mix(2), wk_ref[...], preferred_element_type=F32).astype(k_ref.dtype)
    v_ref[...] = jnp.dot(mix(3), wv_ref[...], preferred_element_type=F32).astype(v_ref.dtype)
    hw = jnp.tanh(jnp.dot(mix(1), wla_ref[...], preferred_element_type=F32))
    w = -jax.nn.softplus(-(w0_ref[...] + _dot(hw, wlb_ref[...]))) - 0.5
    lw_ref[...] = -jnp.exp(w)
    ha = jnp.dot(mix(4), ala_ref[...], preferred_element_type=F32)
    a_ref[...] = jax.nn.sigmoid(a0_ref[...] + _dot(ha, alb_ref[...]))
    hg = jax.nn.sigmoid(jnp.dot(mix(5), gla_ref[...], preferred_element_type=F32))
    g_ref[...] = _dot(hg, glb_ref[...]).astype(g_ref.dtype)


def rwkv_proj(h, rows_per_batch, nw, mu, wr, wk, wv, wla, wlb, ala, alb, gla, glb, w0, a0):
    M, D = h.shape
    tm = _row_tile(rows_per_batch)
    row = pl.BlockSpec((tm, D), lambda i: (i, 0))
    halo = pl.BlockSpec((F32_SUBLANES, D),
                        lambda i: (jnp.maximum(i * (tm // F32_SUBLANES) - 1, 0), 0))
    full = lambda x: pl.BlockSpec(x.shape, lambda i: (0, 0))
    consts = (nw.reshape(1, D), mu, wr, wk, wv, wla, wlb, ala, alb, gla, glb,
              w0.reshape(1, D), a0.reshape(1, D))
    sds = lambda dt: jax.ShapeDtypeStruct((M, D), dt)
    return pl.pallas_call(
        functools.partial(_rwkv_proj_kernel, tiles_per_batch=rows_per_batch // tm),
        grid=(M // tm,),
        in_specs=[row, halo] + [full(c) for c in consts],
        out_specs=[row] * 6,
        out_shape=[sds(BF16), sds(BF16), sds(BF16), sds(F32), sds(F32), sds(BF16)],
        compiler_params=_cparams("parallel"),
        name="rwkv_proj",
    )(h, h, *consts)


def _rwkv_scan_kernel(r_ref, k_ref, v_ref, lw_ref, a_ref, g_ref,
                      kk_ref, ka_ref, rk_ref, lnw_ref, lnb_ref, o_ref, s_ref):
    C = RWKV_CHUNK
    N = RWKV_HEAD
    assert C == N

    @pl.when(pl.program_id(1) == 0)
    def _():
        s_ref[...] = jnp.zeros_like(s_ref)

    W = 2 * N
    lane = lax.broadcasted_iota(jnp.int32, (1, W), 1)
    left = lane < N
    row_c = lax.broadcasted_iota(jnp.int32, (C, 1), 0)
    strict = row_c > (lane & (N - 1))
    incl = row_c >= (lane & (N - 1))
    mask2 = jnp.concatenate([strict, incl], axis=0)
    row_w = lax.broadcasted_iota(jnp.int32, (W, 1), 0)
    bdiag = (row_w < N) == left
    eye = (row_w == lane).astype(F32)
    split = lambda x: jnp.concatenate([jnp.where(left, x, 0.0), jnp.where(left, 0.0, x)], axis=0)

    def head_sum(x):
        sa = jnp.sum(jnp.where(left, x, 0.0), axis=-1, keepdims=True)
        sb = jnp.sum(jnp.where(left, 0.0, x), axis=-1, keepdims=True)
        return jnp.where(left, sa, sb)

    n_pairs = r_ref.shape[-1] // W
    n_sub = r_ref.shape[0] // C
    sls = [slice(p * W, (p + 1) * W) for p in range(n_pairs)]
    items = [(j, p) for j in range(n_sub) for p in range(n_pairs)]
    rows = [slice(j * C, (j + 1) * C) for j in range(n_sub)]
    lws = [lw_ref[rw, :] for rw in rows]
    cums = [_cumsum_rows(lw) for lw in lws]
    pre = []
    for j, p in items:
        rw, sl = rows[j], sls[p]
        r = r_ref[rw, sl].astype(F32)
        k = k_ref[rw, sl].astype(F32)
        a = a_ref[rw, sl]
        cum = cums[j][:, sl]
        kkr = k * kk_ref[:, sl]
        kk = kkr * lax.rsqrt(jnp.maximum(head_sum(kkr * kkr), 1e-24))
        km = k * (1.0 + (a - 1.0) * ka_ref[:, sl])
        beta = kk * a
        e_neg = jnp.exp(-cum)
        lhs = jnp.concatenate([-kk * jnp.exp(cum - lws[j][:, sl]), r * jnp.exp(cum)], axis=0)
        pre.append((r, km, beta, cum, lhs, beta * e_neg, km * e_neg))
    x_as = [_dot_nt(jnp.where(left, q[4], 0.0), jnp.concatenate([q[5], q[6]], axis=0)) for q in pre]
    x_bs = [_dot_nt(jnp.where(left, 0.0, q[4]), jnp.concatenate([q[6], q[5]], axis=0)) for q in pre]
    xbeta = [jnp.where(left, xa, xb) for xa, xb in zip(x_as, x_bs)]
    xkey = [jnp.where(mask2, jnp.where(left, xb, xa), 0.0) for xa, xb in zip(x_as, x_bs)]
    ps = [split(jnp.where(strict, x[:C], 0.0)) for x in xbeta]
    ts = [eye + q for q in ps]
    n_fac = int(math.log2(C))
    for i in range(1, n_fac):
        if i == 1:
            ps = [_dot(q, q) for q in ps]
        zs = [_dot(q, jnp.concatenate([q, t], axis=-1)) if i < n_fac - 1 else _dot(q, t)
              for q, t in zip(ps, ts)]
        ts = [t + z[:, -W:] for t, z in zip(ts, zs)]
        ps = [z[:, :W] for z in zs]
    states = [s_ref[p] for p in range(n_pairs)]
    for j in range(n_sub):
        rw = rows[j]
        base = j * n_pairs
        vs = [v_ref[rw, sl].astype(F32) for sl in sls]
        vx = [jnp.concatenate([jnp.where(left, 0.0, v), jnp.where(left, v, 0.0)], axis=0) for v in vs]
        hss = [_dot_nt(pre[base + p][4], states[p]) + _dot(xkey[base + p], vx[p])
               for p in range(n_pairs)]
        ubd = [_dot(ts[base + p], split(hss[p][:C])) for p in range(n_pairs)]
        ys = [hss[p][C:] + _dot(jnp.where(incl, xbeta[base + p][C:], 0.0), ubd[p]) for p in range(n_pairs)]
        new_states = []
        for p in range(n_pairs):
            r, km, beta, cum, _, _, _ = pre[base + p]
            last = cum[C - 1:C, :]
            e_last = jnp.exp(last - cum)
            upd = _dot_tn(jnp.concatenate([vs[p], ubd[p][:C] + ubd[p][C:]], axis=0),
                          jnp.concatenate([km * e_last, beta * e_last], axis=0))
            new_states.append(states[p] * jnp.exp(last) + jnp.where(bdiag, upd, 0.0))
        states = new_states
        outs = []
        for p in range(n_pairs):
            sl = sls[p]
            r, km = pre[base + p][0], pre[base + p][1]
            y = ys[p]
            mu_y = head_sum(y) * (1.0 / N)
            var = head_sum(jnp.square(y - mu_y)) * (1.0 / N)
            yn = (y - mu_y) * lax.rsqrt(var + RWKV_GN_EPS) * lnw_ref[:, sl] + lnb_ref[:, sl]
            bonus = head_sum(r * km * rk_ref[:, sl]) * vs[p]
            outs.append((yn + bonus) * g_ref[rw, sl].astype(F32))
        o_ref[rw, :] = jnp.concatenate(outs, axis=-1).astype(o_ref.dtype)
    for p in range(n_pairs):
        s_ref[p] = states[p]


def rwkv_scan(r, k, v, lw, a, g, k_k, k_a, r_k, ln_w, ln_b):
    B, L, D = r.shape
    rows = RWKV_CHUNK * RWKV_SUBCHUNKS
    seq = pl.BlockSpec((None, rows, D), lambda b, c: (b, c, 0))
    par = pl.BlockSpec((1, D), lambda b, c: (0, 0))
    return pl.pallas_call(
        _rwkv_scan_kernel,
        grid=(B, L // rows),
        in_specs=[seq] * 6 + [par] * 5,
        out_specs=seq,
        out_shape=jax.ShapeDtypeStruct((B, L, D), BF16),
        scratch_shapes=[pltpu.VMEM((D // (2 * RWKV_HEAD), 2 * RWKV_HEAD, 2 * RWKV_HEAD), F32)],
        compiler_params=_cparams("parallel", "arbitrary"),
        name="rwkv_scan",
    )(r, k, v, lw, a, g, *(p.reshape(1, D) for p in (k_k, k_a, r_k, ln_w, ln_b)))


def rwkv_layer(h, nw, p):
    B, L, D = h.shape
    bf = lambda x: x.astype(BF16)
    h2 = h.reshape(B * L, D)
    outs = rwkv_proj(h2, L, nw, p['mu'],
                     bf(p['w_r']), bf(p['w_k']), bf(p['w_v']),
                     bf(p['w_lora_a']), bf(p['w_lora_b']), bf(p['a_lora_a']), bf(p['a_lora_b']),
                     bf(p['g_lora_a']), bf(p['g_lora_b']), p['w0'], p['a0'])
    r, k, v, lw, a, g = (o.reshape(B, L, D) for o in outs)
    o = rwkv_scan(r, k, v, lw, a, g, p['k_k'], p['k_a'], p['r_k'].reshape(D), p['ln_w'], p['ln_b'])
    return o.reshape(B * L, D), bf(p['w_o'])


def _sdecay_groups(qs, ks, vss, ccss, crss, s_ref, P):
    C = qs[0].shape[0]
    causal = _tri(C)
    groups = range(len(qs))
    s0s = [s_ref[g] for g in groups]
    scores = [_dot_nt(qs[g], ks[g]) for g in groups]
    qss = [_dot(qs[g], s0s[g]) for g in groups]
    yss = []
    for g in groups:
        ys = []
        for r, (v, cc, cr) in enumerate(zip(vss[g], ccss[g], crss[g])):
            dec = jnp.exp(jnp.where(causal, cc - cr, -jnp.inf))
            ys.append(_dot(scores[g] * dec, v) + jnp.exp(cc) * qss[g][:, r * P:(r + 1) * P])
        yss.append(ys)
    for g in groups:
        lasts = [cc[C - 1:C, :] for cc in ccss[g]]
        vw = jnp.concatenate([v * jnp.exp(last - cc)
                              for v, cc, last in zip(vss[g], ccss[g], lasts)], axis=-1)
        keep = jnp.concatenate([jnp.broadcast_to(jnp.exp(last), (1, P)) for last in lasts], axis=-1)
        s_ref[g] = keep * s0s[g] + _dot_tn(ks[g], vw)
    return yss


def _m2_scan_kernel(shift_ref, expand_ref, z_ref, xs_ref, xsh_ref, b_ref, bh_ref, c_ref, ch_ref,
                    wx_ref, wb_ref, wc_ref, bx_ref, bb_ref, bc_ref,
                    dtc_ref, dtr_ref, pc_ref, pr_ref, dx_ref, nw_ref, o_ref, s_ref, *, GS, R, P, N, C):
    first = pl.program_id(2) == 0

    @pl.when(first)
    def _():
        s_ref[...] = jnp.zeros_like(s_ref)

    shift = shift_ref[...]

    def conv_silu(x_ref, prev_ref, w_ref, bias_ref):
        x = x_ref[...]
        prev = prev_ref[...]
        xf = jnp.concatenate([jnp.where(first, jnp.zeros_like(prev), prev), x], axis=0)
        taps = jnp.dot(shift, xf, preferred_element_type=F32)
        w = w_ref[...]
        acc = bias_ref[...] + w[M2_CONV - 1:M2_CONV] * x.astype(F32)
        for i in range(M2_CONV - 1):
            acc = acc + w[i:i + 1] * taps[i * C:(i + 1) * C]
        return _silu(acc)

    GP = R * P
    xs = conv_silu(xs_ref, xsh_ref, wx_ref, bx_ref)
    k = conv_silu(b_ref, bh_ref, wb_ref, bb_ref)
    q = conv_silu(c_ref, ch_ref, wc_ref, bc_ref)
    pc = pc_ref[...]
    pr = pr_ref[...]
    dt_c = jax.nn.softplus(dtc_ref[...] + pc[0:1])
    dt_r = jax.nn.softplus(dtr_ref[...] + pr[:, 0:1])
    cc = _cumsum_rows(dt_c * -jnp.exp(pc[1:2]))
    cr = _cumsum_lanes(dt_r * -jnp.exp(pr[:, 1:2]))
    expand = expand_ref[...]

    def widen(x):
        return jnp.dot(jnp.concatenate(_split3(x), axis=-1), expand, preferred_element_type=F32)

    last = cc[C - 1:C]
    e_cum = widen(jnp.exp(cc))
    v = xs * widen(dt_c)
    vw = v * widen(jnp.exp(last - cc))
    keep = e_cum[C - 1:C]
    causal = _tri(C)
    head_of_lane = lax.broadcasted_iota(jnp.int32, (1, GP), 1) // P
    groups = range(GS)
    gsl = [slice(g * GP, (g + 1) * GP) for g in groups]
    qg = [q[:, g * N:(g + 1) * N] for g in groups]
    kg = [k[:, g * N:(g + 1) * N] for g in groups]
    s0s = [s_ref[g] for g in groups]
    scores = [_dot_nt(qg[g], kg[g]) for g in groups]
    qss = [_dot(qg[g], s0s[g]) for g in groups]
    pvs = []
    for g in groups:
        ps = [scores[g] * jnp.exp(jnp.where(causal, cc[:, j:j + 1] - cr[j:j + 1], -jnp.inf))
              for j in range(g * R, (g + 1) * R)]
        vbd = jnp.concatenate([jnp.where(head_of_lane == r, v[:, gsl[g]], 0.0) for r in range(R)], axis=0)
        pvs.append(_dot(jnp.concatenate(ps, axis=-1), vbd))
    for g in groups:
        s_ref[g] = keep[:, gsl[g]] * s0s[g] + _dot_tn(kg[g], vw[:, gsl[g]])
    z = z_ref[...].astype(F32)
    dx = dx_ref[...]
    outs = []
    for g in groups:
        y = pvs[g] + e_cum[:, gsl[g]] * qss[g] + dx[:, gsl[g]] * xs[:, gsl[g]]
        y = y * _silu(z[:, gsl[g]])
        outs.append(y * lax.rsqrt(jnp.mean(y * y, axis=-1, keepdims=True) + NORM_EPS))
    o_ref[...] = (jnp.concatenate(outs, axis=-1) * nw_ref[...]).astype(o_ref.dtype)


def m2_scan(zx, dt, conv_w, conv_b, dt_bias, a_log, d_skip, norm_w, *, d_inner, heads):
    B, L, _ = zx.shape
    G, N, C, GS = M2_GROUPS, M2_STATE, M2_CHUNK, M2_GROUPS_PER_STEP
    R = heads // G
    GP = d_inner // G
    NS = G // GS
    WX, WN, WR = GS * GP, GS * N, GS * R
    x0 = d_inner // WX
    b0 = 2 * d_inner // WN
    c0 = b0 + NS
    wb0 = d_inner // WN
    wc0 = wb0 + NS
    cur = lambda w, off: pl.BlockSpec((None, C, w), lambda b, g, c: (b, c, off + g))
    HB = BF16_SUBLANES
    halo = lambda w, off: pl.BlockSpec(
        (None, HB, w), lambda b, g, c: (b, jnp.maximum(c * (C // HB) - 1, 0), off + g))
    par = lambda rows, w, off: pl.BlockSpec((rows, w), lambda b, g, c: (0, off + g))
    tap_row = jnp.arange((M2_CONV - 1) * C)
    shift = (jnp.arange(HB + C)[None, :] ==
             (HB + tap_row % C - (M2_CONV - 1) + tap_row // C)[:, None]).astype(BF16)
    dt4 = dt.reshape(B, L, NS, WR)
    pcol = jnp.stack([dt_bias, a_log]).reshape(2, NS, WR).transpose(1, 0, 2)
    prow = pcol.transpose(0, 2, 1)
    expand = jnp.tile(jnp.arange(WX)[None, :] // (GP // R) == jnp.arange(WR)[:, None], (3, 1)).astype(BF16)
    cb = conv_b.reshape(1, -1)
    const = lambda x: pl.BlockSpec(x.shape, lambda b, g, c: (0, 0))
    return pl.pallas_call(
        functools.partial(_m2_scan_kernel, GS=GS, R=R, P=GP // R, N=N, C=C),
        grid=(B, NS, L // C),
        in_specs=[const(shift), const(expand),
                  cur(WX, 0), cur(WX, x0), halo(WX, x0), cur(WN, b0), halo(WN, b0), cur(WN, c0), halo(WN, c0),
                  par(M2_CONV, WX, 0), par(M2_CONV, WN, wb0), par(M2_CONV, WN, wc0),
                  par(1, WX, 0), par(1, WN, wb0), par(1, WN, wc0),
                  pl.BlockSpec((None, None, C, WR), lambda b, g, c: (b, g, c, 0)),
                  pl.BlockSpec((None, None, None, WR, C), lambda b, g, c: (b, g, c, 0, 0)),
                  pl.BlockSpec((None, 2, WR), lambda b, g, c: (g, 0, 0)),
                  pl.BlockSpec((None, WR, 2), lambda b, g, c: (g, 0, 0)),
                  par(1, WX, 0), par(1, WX, 0)],
        out_specs=cur(WX, 0),
        out_shape=jax.ShapeDtypeStruct((B, L, d_inner), BF16),
        scratch_shapes=[pltpu.VMEM((GS, N, GP), F32)],
        compiler_params=_cparams("parallel", "parallel", "arbitrary"),
        name="m2_scan",
    )(shift, expand, zx, zx, zx, zx, zx, zx, zx, conv_w, conv_w, conv_w, cb, cb, cb,
      dt4.transpose(0, 2, 1, 3), dt4.reshape(B, L // C, C, NS, WR).transpose(0, 3, 1, 4, 2), pcol, prow,
      jnp.repeat(d_skip, GP // R).reshape(1, -1), norm_w.reshape(1, -1))


def mamba2_layer(h, nw, p):
    B, L, D = h.shape
    d_inner = p['norm_w'].shape[0]
    heads = p['a_log'].shape[0]
    w_in = p['in_proj']
    n_main = w_in.shape[1] - heads
    w_dt = jnp.pad(w_in[:, n_main:], ((0, 0), (0, 128 - heads)))
    h2 = h.reshape(B * L, D)
    zx = norm_linear(h2, nw, w_in[:, :n_main].astype(BF16), 2048, BF16).reshape(B, L, n_main)
    dt = norm_linear(h2, nw, w_dt.astype(BF16), 128, F32)[:, :heads].reshape(B, L, heads)
    y = m2_scan(zx, dt, p['conv_w'], p['conv_b'], p['dt_bias'], p['a_log'], p['d'], p['norm_w'],
                d_inner=d_inner, heads=heads)
    return y.reshape(B * L, d_inner), p['out_proj'].astype(BF16)


def _ret_scan_kernel(q_ref, k_ref, v_ref, g_ref, cos_ref, sin_ref, lg_ref, o_ref, s_ref,
                     *, H, dk, dv, C, scale):
    @pl.when(pl.program_id(1) == 0)
    def _():
        s_ref[...] = jnp.zeros_like(s_ref)

    cos = cos_ref[...]
    sin = sin_ref[...]
    half = dk // 2

    def rotary(x_ref, hd):
        x1 = x_ref[:, hd * dk:hd * dk + half].astype(F32)
        x2 = x_ref[:, hd * dk + half:(hd + 1) * dk].astype(F32)
        return jnp.concatenate([x1 * cos - x2 * sin, x1 * sin + x2 * cos], axis=-1)

    lg = lg_ref[...]
    steps_c = (lax.broadcasted_iota(jnp.int32, (C, 1), 0) + 1).astype(F32)
    steps_r = (lax.broadcasted_iota(jnp.int32, (1, C), 1) + 1).astype(F32)
    yss = _sdecay_groups(
        [rotary(q_ref, hd) for hd in range(H)], [rotary(k_ref, hd) * scale for hd in range(H)],
        [[v_ref[:, hd * dv:(hd + 1) * dv]] for hd in range(H)],
        [[steps_c * lg[:, hd:hd + 1]] for hd in range(H)],
        [[steps_r * lg[:, hd:hd + 1]] for hd in range(H)], s_ref, dv)
    outs = []
    for hd in range(H):
        y = yss[hd][0]
        mu = jnp.mean(y, axis=-1, keepdims=True)
        var = jnp.mean(jnp.square(y - mu), axis=-1, keepdims=True)
        outs.append(_silu(g_ref[:, hd * dv:(hd + 1) * dv].astype(F32)) * ((y - mu) * lax.rsqrt(var + NORM_EPS)))
    o_ref[...] = jnp.concatenate(outs, axis=-1).astype(o_ref.dtype)


def ret_scan(proj, cos, sin, log_gamma, *, dk, dv):
    B, L, _ = proj.shape
    H, C = RET_HEADS, RET_CHUNK
    qk = lambda j: pl.BlockSpec((None, C, H * dk), lambda b, c: (b, c, j))
    vg = lambda j: pl.BlockSpec((None, C, H * dv), lambda b, c: (b, c, j))
    tab = pl.BlockSpec((C, dk // 2), lambda b, c: (c, 0))
    return pl.pallas_call(
        functools.partial(_ret_scan_kernel, H=H, dk=dk, dv=dv, C=C, scale=dk ** -0.5),
        grid=(B, L // C),
        in_specs=[qk(0), qk(1), vg(1), vg(2), tab, tab, pl.BlockSpec((1, H), lambda b, c: (0, 0))],
        out_specs=vg(0),
        out_shape=jax.ShapeDtypeStruct((B, L, H * dv), BF16),
        scratch_shapes=[pltpu.VMEM((H, dk, dv), F32)],
        compiler_params=_cparams("parallel", "arbitrary"),
        name="ret_scan",
    )(proj, proj, proj, proj, cos, sin, log_gamma.reshape(1, H))


def retnet_layer(h, nw, p):
    B, L, D = h.shape
    w_in = p['in_proj']
    vd = p['out_proj'].shape[0]
    dv = vd // RET_HEADS
    dk = (w_in.shape[1] - 2 * vd) // (2 * RET_HEADS)
    assert 2 * RET_HEADS * dk == vd
    h2 = h.reshape(B * L, D)
    proj = norm_linear(h2, nw, w_in.astype(BF16), 2048, BF16).reshape(B, L, -1)
    half = dk // 2
    inv_freq = 1.0 / (ROPE_BASE ** jnp.linspace(0.0, 1.0, half, dtype=F32))
    ang = jnp.arange(L, dtype=F32)[:, None] * inv_freq[None, :]
    log_gamma = jnp.log1p(-jnp.exp2(-5.0 - jnp.arange(RET_HEADS, dtype=F32)))
    y = ret_scan(proj, jnp.cos(ang), jnp.sin(ang), log_gamma, dk=dk, dv=dv)
    return y.reshape(B * L, vd), p['out_proj'].astype(BF16)


def _gla_kernel(q_ref, k_ref, v_ref, r_ref, glr_ref, gup_ref, gb_ref, nw_ref, o_ref, s_ref,
                *, H, dk, dv, scale):
    C = GLA_CHUNK
    SB = GLA_SUB

    @pl.when(pl.program_id(1) == 0)
    def _():
        s_ref[...] = jnp.zeros_like(s_ref)

    n_sub = q_ref.shape[0] // C
    rows = [slice(j * C, (j + 1) * C) for j in range(n_sub)]
    gup, gb = gup_ref[...], gb_ref[...]
    gc_alls = [_cumsum_rows(jax.nn.log_sigmoid(_dot(glr_ref[rw, :], gup) + gb) / GLA_TAU) for rw in rows]
    row_id = lax.broadcasted_iota(jnp.int32, (C, 1), 0)
    sub_row = lax.broadcasted_iota(jnp.int32, (SB, 1), 0)
    col_id = lax.broadcasted_iota(jnp.int32, (1, C), 1)
    units = [(j, hd) for j in range(n_sub) for hd in range(H)]
    qs = [q_ref[rows[j], hd * dk:(hd + 1) * dk].astype(F32) * scale for j, hd in units]
    ks = [k_ref[rows[j], hd * dk:(hd + 1) * dk].astype(F32) for j, hd in units]
    gcs = [gc_alls[j][:, hd * dk:(hd + 1) * dk] for j, hd in units]
    vs = [v_ref[rows[j], hd * dv:(hd + 1) * dv] for j, hd in units]
    a_rows = [[] for _ in units]
    for i in range(C // SB):
        lo = i * SB
        for u in range(len(units)):
            qi = qs[u][lo:lo + SB]
            gi = gcs[u][lo:lo + SB]
            ki = ks[u][lo:lo + SB]
            cols = []
            for j in range(SB):
                d = jnp.where(sub_row >= j, gi - gi[j:j + 1], -jnp.inf)
                col = jnp.sum(qi * ki[j:j + 1] * jnp.exp(d), axis=-1, keepdims=True)
                cols.append(jnp.where(col_id == lo + j, col, 0.0))
            blk = _tree_sum(cols)
            if i > 0:
                ref = gcs[u][lo - 1:lo]
                qt = qi * jnp.exp(gi - ref)
                kt = ks[u] * jnp.exp(jnp.where(row_id < lo, ref - gcs[u], -jnp.inf))
                blk = blk + jnp.where(col_id < lo, _dot_nt(qt, kt), 0.0)
            a_rows[u].append(blk)
    intra = [_dot(jnp.concatenate(a_rows[u], axis=0), vs[u]) for u in range(len(units))]
    lasts = [gc[C - 1:C] for gc in gcs]
    kvs = [_dot_tn(vs[u], ks[u] * jnp.exp(lasts[u] - gcs[u])) for u in range(len(units))]
    states = [s_ref[hd] for hd in range(H)]
    nw = nw_ref[...]
    for j in range(n_sub):
        outs = []
        for hd in range(H):
            u = j * H + hd
            o = intra[u] + _dot_nt(qs[u] * jnp.exp(gcs[u]), states[hd])
            states[hd] = states[hd] * jnp.exp(lasts[u]) + kvs[u]
            o = o * lax.rsqrt(jnp.mean(o * o, axis=-1, keepdims=True) + NORM_EPS) * nw
            outs.append(o * _silu(r_ref[rows[j], hd * dv:(hd + 1) * dv].astype(F32)))
        o_ref[rows[j], :] = jnp.concatenate(outs, axis=-1).astype(o_ref.dtype)
    for hd in range(H):
        s_ref[hd] = states[hd]


def gla_scan(proj, gate_up, gate_bias, norm_w, *, dk, dv, lora_pad):
    B, L, _ = proj.shape
    H, C = GLA_HEADS, GLA_CHUNK * GLA_SUBCHUNKS
    qk = lambda j: pl.BlockSpec((None, C, H * dk), lambda b, c: (b, c, j))
    vr = lambda j: pl.BlockSpec((None, C, H * dv), lambda b, c: (b, c, j))
    g0 = (2 * H * dk + 2 * H * dv) // lora_pad
    full = lambda x: pl.BlockSpec(x.shape, lambda b, c: (0, 0))
    consts = (gate_up, gate_bias.reshape(1, -1), norm_w.reshape(1, -1))
    return pl.pallas_call(
        functools.partial(_gla_kernel, H=H, dk=dk, dv=dv, scale=dk ** -0.5),
        grid=(B, L // C),
        in_specs=[qk(0), qk(1), vr(1), vr(2),
                  pl.BlockSpec((None, C, lora_pad), lambda b, c: (b, c, g0))] + [full(x) for x in consts],
        out_specs=vr(0),
        out_shape=jax.ShapeDtypeStruct((B, L, H * dv), BF16),
        scratch_shapes=[pltpu.VMEM((H, dv, dk), F32)],
        compiler_params=_cparams("parallel", "arbitrary"),
        name="gla_scan",
    )(proj, proj, proj, proj, proj, *consts)


def gla_layer(h, nw, p):
    B, L, D = h.shape
    lora, qk = p['gate_up'].shape
    vd = p['out_proj'].shape[0]
    assert 2 * qk == vd
    lora_pad = 128
    w_in = jnp.pad(p['in_proj'], ((0, 0), (0, lora_pad - lora)))
    n_all = w_in.shape[1]
    gate_up = jnp.pad(p['gate_up'], ((0, lora_pad - lora), (0, 0))).astype(BF16)
    h2 = h.reshape(B * L, D)
    proj = norm_linear(h2, nw, w_in.astype(BF16), n_all, BF16).reshape(B, L, n_all)
    o = gla_scan(proj, gate_up, p['gate_bias'], p['norm_w'],
                 dk=qk // GLA_HEADS, dv=vd // GLA_HEADS, lora_pad=lora_pad)
    return o.reshape(B * L, vd), p['out_proj'].astype(BF16)


def kernel(x, meta_tokens, norm_mix, norm_mlp, norm_final, mlp_w_in, mlp_w_out, rwkv_mu, rwkv_w_r, rwkv_w_k, rwkv_w_v, rwkv_w0, rwkv_w_lora_a, rwkv_w_lora_b, rwkv_a0, rwkv_a_lora_a, rwkv_a_lora_b, rwkv_g_lora_a, rwkv_g_lora_b, rwkv_k_k, rwkv_k_a, rwkv_r_k, rwkv_ln_w, rwkv_ln_b, rwkv_w_o, m2_in_proj, m2_conv_w, m2_conv_b, m2_dt_bias, m2_a_log, m2_d, m2_norm_w, m2_out_proj, gla_in_proj, gla_gate_up, gla_gate_bias, gla_norm_w, gla_out_proj, ret_in_proj, ret_out_proj):
    B, S, D = x.shape
    depth = norm_mix.shape[0]
    n_tok = N_META + S
    L = -(-n_tok // SEQ_ALIGN) * SEQ_ALIGN
    meta = jnp.broadcast_to(meta_tokens[None].astype(x.dtype), (B, N_META, D))
    h = jnp.concatenate([meta, x, jnp.zeros((B, L - n_tok, D), x.dtype)], axis=1)
    for i in range(depth):
        m, j = i % 4, i // 4
        if m == 0:
            a, wo = rwkv_layer(h, norm_mix[i], dict(
                mu=rwkv_mu[j], w_r=rwkv_w_r[j], w_k=rwkv_w_k[j], w_v=rwkv_w_v[j], w0=rwkv_w0[j],
                w_lora_a=rwkv_w_lora_a[j], w_lora_b=rwkv_w_lora_b[j], a0=rwkv_a0[j],
                a_lora_a=rwkv_a_lora_a[j], a_lora_b=rwkv_a_lora_b[j],
                g_lora_a=rwkv_g_lora_a[j], g_lora_b=rwkv_g_lora_b[j],
                k_k=rwkv_k_k[j], k_a=rwkv_k_a[j], r_k=rwkv_r_k[j],
                ln_w=rwkv_ln_w[j], ln_b=rwkv_ln_b[j], w_o=rwkv_w_o[j]))
        elif m == 1:
            a, wo = mamba2_layer(h, norm_mix[i], dict(
                in_proj=m2_in_proj[j], conv_w=m2_conv_w[j], conv_b=m2_conv_b[j],
                dt_bias=m2_dt_bias[j], a_log=m2_a_log[j], d=m2_d[j], norm_w=m2_norm_w[j],
                out_proj=m2_out_proj[j]))
        elif m == 2:
            a, wo = gla_layer(h, norm_mix[i], dict(
                in_proj=gla_in_proj[j], gate_up=gla_gate_up[j], gate_bias=gla_gate_bias[j],
                norm_w=gla_norm_w[j], out_proj=gla_out_proj[j]))
        else:
            a, wo = retnet_layer(h, norm_mix[i], dict(in_proj=ret_in_proj[j], out_proj=ret_out_proj[j]))
        h = mlp_res(linear_res(a, wo, h.reshape(B * L, D)), norm_mlp[i], mlp_w_in[i].astype(BF16),
                    mlp_w_out[i].astype(BF16), norm_final, final=(i == depth - 1)).reshape(B, L, D)
    return h[:, N_META:n_tok]
```

```python
import functools
import math

import jax
import jax.numpy as jnp
from jax import lax
from jax.experimental import pallas as pl
from jax.experimental.pallas import tpu as pltpu

F32 = jnp.float32
BF16 = jnp.bfloat16

N_META = 16
NORM_EPS = 1e-5
SEQ_ALIGN = 192
MAX_ROW_TILE = 640
MAX_WIDE_ROW_TILE = 1152
BF16_SUBLANES = 16
F32_SUBLANES = 8
VMEM_LIMIT = 56 * 1024 * 1024

RWKV_HEAD = 64
RWKV_GN_EPS = 64e-5
RWKV_CHUNK = 64
RWKV_SUBCHUNKS = 3

M2_HEAD = 64
M2_GROUPS = 8
M2_STATE = 128
M2_CONV = 4
M2_CHUNK = 192
M2_GROUPS_PER_STEP = 8

GLA_HEADS = 4
GLA_TAU = 16.0
GLA_CHUNK = 64
GLA_SUBCHUNKS = 3
GLA_SUB = 8

RET_HEADS = 4
RET_CHUNK = 192
ROPE_BASE = 10000.0


def _cparams(*sem):
    return pltpu.CompilerParams(dimension_semantics=sem, vmem_limit_bytes=VMEM_LIMIT)


def _row_tile(rows, limit=MAX_ROW_TILE):
    for t in range(min(rows, limit) // BF16_SUBLANES * BF16_SUBLANES, 0, -BF16_SUBLANES):
        if rows % t == 0:
            return t
    raise ValueError(rows)


def _dot(a, b):
    return jnp.dot(a.astype(BF16), b.astype(BF16), preferred_element_type=F32)


def _dot_nt(a, b):
    return lax.dot_general(a.astype(BF16), b.astype(BF16), (((1,), (1,)), ((), ())),
                           preferred_element_type=F32)


def _dot_tn(a, b):
    return lax.dot_general(a.astype(BF16), b.astype(BF16), (((0,), (0,)), ((), ())),
                           preferred_element_type=F32)


def _split3(x):
    hi = x.astype(BF16)
    r1 = x - hi.astype(F32)
    mid = r1.astype(BF16)
    return hi, mid, (r1 - mid.astype(F32)).astype(BF16)


def _cumsum_rows(x):
    C = x.shape[0]
    r = lax.broadcasted_iota(jnp.int32, (C, 3 * C), 0)
    c = lax.broadcasted_iota(jnp.int32, (C, 3 * C), 1)
    tri3 = (r >= lax.rem(c, C)).astype(BF16)
    return jnp.dot(tri3, jnp.concatenate(_split3(x), axis=0), preferred_element_type=F32)


def _cumsum_lanes(x):
    C = x.shape[1]
    r = lax.broadcasted_iota(jnp.int32, (3 * C, C), 0)
    c = lax.broadcasted_iota(jnp.int32, (3 * C, C), 1)
    tri3 = (lax.rem(r, C) <= c).astype(BF16)
    return jnp.dot(jnp.concatenate(_split3(x), axis=1), tri3, preferred_element_type=F32)


def _rms_rows(x, w):
    return x * lax.rsqrt(jnp.mean(x * x, axis=-1, keepdims=True) + NORM_EPS) * w


def _tri(n, strict=False, upper=False):
    r = lax.broadcasted_iota(jnp.int32, (n, n), 0)
    c = lax.broadcasted_iota(jnp.int32, (n, n), 1)
    if upper:
        r, c = c, r
    return (r > c) if strict else (r >= c)


def _silu(x):
    return x * jax.nn.sigmoid(x)


def _tree_sum(xs):
    while len(xs) > 1:
        xs = [xs[i] + xs[i + 1] if i + 1 < len(xs) else xs[i] for i in range(0, len(xs), 2)]
    return xs[0]


def _norm_linear_kernel(h_ref, nw_ref, w_ref, o_ref, u_ref):
    @pl.when(pl.program_id(1) == 0)
    def _():
        u_ref[...] = _rms_rows(h_ref[...], nw_ref[...]).astype(BF16)

    o_ref[...] = jnp.dot(u_ref[...], w_ref[...], preferred_element_type=F32).astype(o_ref.dtype)


def norm_linear(h, nw, w, tn, out_dtype):
    M, D = h.shape
    N = w.shape[1]
    tm = _row_tile(M, MAX_WIDE_ROW_TILE)
    return pl.pallas_call(
        _norm_linear_kernel,
        grid=(M // tm, N // tn),
        in_specs=[pl.BlockSpec((tm, D), lambda i, j: (i, 0)),
                  pl.BlockSpec((1, D), lambda i, j: (0, 0)),
                  pl.BlockSpec((D, tn), lambda i, j: (0, j))],
        out_specs=pl.BlockSpec((tm, tn), lambda i, j: (i, j)),
        out_shape=jax.ShapeDtypeStruct((M, N), out_dtype),
        scratch_shapes=[pltpu.VMEM((tm, D), BF16)],
        compiler_params=_cparams("parallel", "arbitrary"),
        name="norm_linear",
    )(h, nw.reshape(1, D), w)


def _linear_res_kernel(a_ref, w_ref, h_ref, o_ref):
    o_ref[...] = h_ref[...] + jnp.dot(a_ref[...], w_ref[...], preferred_element_type=F32)


def linear_res(a, w, h):
    M, K = a.shape
    D = w.shape[1]
    tm = _row_tile(M, MAX_WIDE_ROW_TILE)
    return pl.pallas_call(
        _linear_res_kernel,
        grid=(M // tm,),
        in_specs=[pl.BlockSpec((tm, K), lambda i: (i, 0)),
                  pl.BlockSpec((K, D), lambda i: (0, 0)),
                  pl.BlockSpec((tm, D), lambda i: (i, 0))],
        out_specs=pl.BlockSpec((tm, D), lambda i: (i, 0)),
        out_shape=jax.ShapeDtypeStruct((M, D), F32),
        compiler_params=_cparams("parallel"),
        name="linear_res",
    )(a, w, h)


def _mlp_kernel(h_ref, nw_ref, w1_ref, w2_ref, fnw_ref, o_ref, u_ref, *, final):
    f = pl.program_id(1)

    @pl.when(f == 0)
    def _():
        h = h_ref[...]
        u_ref[...] = _rms_rows(h, nw_ref[...]).astype(BF16)
        o_ref[...] = h

    hid = jnp.dot(u_ref[...], w1_ref[...], preferred_element_type=F32)
    hid = jnp.square(jnp.maximum(hid, 0.0))
    o_ref[...] += _dot(hid, w2_ref[...])

    if final:
        @pl.when(f == pl.num_programs(1) - 1)
        def _():
            o_ref[...] = _rms_rows(o_ref[...], fnw_ref[...])


def mlp_res(h, nw, w1, w2, fnw, final, tf=1024):
    M, D = h.shape
    F = w1.shape[1]
    tm = _row_tile(M, MAX_WIDE_ROW_TILE)
    row = pl.BlockSpec((tm, D), lambda i, f: (i, 0))
    vec = pl.BlockSpec((1, D), lambda i, f: (0, 0))
    return pl.pallas_call(
        functools.partial(_mlp_kernel, final=final),
        grid=(M // tm, F // tf),
        in_specs=[row, vec,
                  pl.BlockSpec((D, tf), lambda i, f: (0, f)),
                  pl.BlockSpec((tf, D), lambda i, f: (f, 0)), vec],
        out_specs=row,
        out_shape=jax.ShapeDtypeStruct((M, D), F32),
        scratch_shapes=[pltpu.VMEM((tm, D), BF16)],
        compiler_params=_cparams("parallel", "arbitrary"),
        name="mlp_res",
    )(h, nw.reshape(1, D), w1, w2, fnw.reshape(1, D))


def _rwkv_proj_kernel(h_ref, halo_ref, nw_ref, mu_ref, wr_ref, wk_ref, wv_ref,
                      wla_ref, wlb_ref, ala_ref, alb_ref, gla_ref, glb_ref,
                      w0_ref, a0_ref,
                      r_ref, k_ref, v_ref, lw_ref, a_ref, g_ref, *, tiles_per_batch):
    nw = nw_ref[...]
    u = _rms_rows(h_ref[...], nw)
    first = pl.program_id(0) % tiles_per_batch == 0
    prev_last = _rms_rows(halo_ref[F32_SUBLANES - 1:F32_SUBLANES, :], nw)
    prev_last = jnp.where(first, 0.0, prev_last)
    row = lax.broadcasted_iota(jnp.int32, (u.shape[0], 1), 0)
    xx = jnp.where(row == 0, prev_last, pltpu.roll(u, 1, axis=0)) - u
    mu = mu_ref[...]
    mix = lambda i: (u + xx * mu[i:i + 1, :]).astype(BF16)
    r_ref[...] = jnp.dot(mix(0), wr_ref[...], preferred_element_type=F32).astype(r_ref.dtype)
    k_ref[...] = jnp.dot(mix(2), wk_ref[...], preferred_element_type=F32).astype(k_ref.dtype)
    v_ref[...] = jnp.dot(mix(3), wv_ref[...], preferred_element_type=F32).astype(v_ref.dtype)
    hw = jnp.tanh(jnp.dot(mix(1), wla_ref[...], preferred_element_type=F32))
    w = -jax.nn.softplus(-(w0_ref[...] + _dot(hw, wlb_ref[...]))) - 0.5
    lw_ref[...] = -jnp.exp(w)
    ha = jnp.dot(mix(4), ala_ref[...], preferred_element_type=F32)
    a_ref[...] = jax.nn.sigmoid(a0_ref[...] + _dot(ha, alb_ref[...]))
    hg = jax.nn.sigmoid(jnp.dot(mix(5), gla_ref[...], preferred_element_type=F32))
    g_ref[...] = _dot(hg, glb_ref[...]).astype(g_ref.dtype)


def rwkv_proj(h, rows_per_batch, nw, mu, wr, wk, wv, wla, wlb, ala, alb, gla, glb, w0, a0):
    M, D = h.shape
    tm = _row_tile(rows_per_batch)
    row = pl.BlockSpec((tm, D), lambda i: (i, 0))
    halo = pl.BlockSpec((F32_SUBLANES, D),
                        lambda i: (jnp.maximum(i * (tm // F32_SUBLANES) - 1, 0), 0))
    full = lambda x: pl.BlockSpec(x.shape, lambda i: (0, 0))
    consts = (nw.reshape(1, D), mu, wr, wk, wv, wla, wlb, ala, alb, gla, glb,
              w0.reshape(1, D), a0.reshape(1, D))
    sds = lambda dt: jax.ShapeDtypeStruct((M, D), dt)
    return pl.pallas_call(
        functools.partial(_rwkv_proj_kernel, tiles_per_batch=rows_per_batch // tm),
        grid=(M // tm,),
        in_specs=[row, halo] + [full(c) for c in consts],
        out_specs=[row] * 6,
        out_shape=[sds(BF16), sds(BF16), sds(BF16), sds(F32), sds(F32), sds(BF16)],
        compiler_params=_cparams("parallel"),
        name="rwkv_proj",
    )(h, h, *consts)


def _rwkv_scan_kernel(r_ref, k_ref, v_ref, lw_ref, a_ref, g_ref,
                      kk_ref, ka_ref, rk_ref, lnw_ref, lnb_ref, o_ref, s_ref):
    C = RWKV_CHUNK
    N = RWKV_HEAD
    assert C == N

    @pl.when(pl.program_id(1) == 0)
    def _():
        s_ref[...] = jnp.zeros_like(s_ref)

    W = 2 * N
    lane = lax.broadcasted_iota(jnp.int32, (1, W), 1)
    left = lane < N
    row_c = lax.broadcasted_iota(jnp.int32, (C, 1), 0)
    strict = row_c > (lane & (N - 1))
    incl = row_c >= (lane & (N - 1))
    mask2 = jnp.concatenate([strict, incl], axis=0)
    row_w = lax.broadcasted_iota(jnp.int32, (W, 1), 0)
    bdiag = (row_w < N) == left
    eye = (row_w == lane).astype(F32)
    split = lambda x: jnp.concatenate([jnp.where(left, x, 0.0), jnp.where(left, 0.0, x)], axis=0)

    def head_sum(x):
        sa = jnp.sum(jnp.where(left, x, 0.0), axis=-1, keepdims=True)
        sb = jnp.sum(jnp.where(left, 0.0, x), axis=-1, keepdims=True)
        return jnp.where(left, sa, sb)

    n_pairs = r_ref.shape[-1] // W
    n_sub = r_ref.shape[0] // C
    sls = [slice(p * W, (p + 1) * W) for p in range(n_pairs)]
    items = [(j, p) for j in range(n_sub) for p in range(n_pairs)]
    rows = [slice(j * C, (j + 1) * C) for j in range(n_sub)]
    lws = [lw_ref[rw, :] for rw in rows]
    cums = [_cumsum_rows(lw) for lw in lws]
    pre = []
    for j, p in items:
        rw, sl = rows[j], sls[p]
        r = r_ref[rw, sl].astype(F32)
        k = k_ref[rw, sl].astype(F32)
        a = a_ref[rw, sl]
        cum = cums[j][:, sl]
        kkr = k * kk_ref[:, sl]
        kk = kkr * lax.rsqrt(jnp.maximum(head_sum(kkr * kkr), 1e-24))
        km = k * (1.0 + (a - 1.0) * ka_ref[:, sl])
        beta = kk * a
        e_neg = jnp.exp(-cum)
        lhs = jnp.concatenate([-kk * jnp.exp(cum - lws[j][:, sl]), r * jnp.exp(cum)], axis=0)
        pre.append((r, km, beta, cum, lhs, beta * e_neg, km * e_neg))
    x_as = [_dot_nt(jnp.where(left, q[4], 0.0), jnp.concatenate([q[5], q[6]], axis=0)) for q in pre]
    x_bs = [_dot_nt(jnp.where(left, 0.0, q[4]), jnp.concatenate([q[6], q[5]], axis=0)) for q in pre]
    xbeta = [jnp.where(left, xa, xb) for xa, xb in zip(x_as, x_bs)]
    xkey = [jnp.where(mask2, jnp.where(left, xb, xa), 0.0) for xa, xb in zip(x_as, x_bs)]
    ps = [split(jnp.where(strict, x[:C], 0.0)) for x in xbeta]
    ts = [eye + q for q in ps]
    n_fac = int(math.log2(C))
    for i in range(1, n_fac):
        if i == 1:
            ps = [_dot(q, q) for q in ps]
        zs = [_dot(q, jnp.concatenate([q, t], axis=-1)) if i < n_fac - 1 else _dot(q, t)
              for q, t in zip(ps, ts)]
        ts = [t + z[:, -W:] for t, z in zip(ts, zs)]
        ps = [z[:, :W] for z in zs]
    states = [s_ref[p] for p in range(n_pairs)]
    for j in range(n_sub):
        rw = rows[j]
        base = j * n_pairs
        vs = [v_ref[rw, sl].astype(F32) for sl in sls]
        vx = [jnp.concatenate([jnp.where(left, 0.0, v), jnp.where(left, v, 0.0)], axis=0) for v in vs]
        hss = [_dot_nt(pre[base + p][4], states[p]) + _dot(xkey[base + p], vx[p])
               for p in range(n_pairs)]
        ubd = [_dot(ts[base + p], split(hss[p][:C])) for p in range(n_pairs)]
        ys = [hss[p][C:] + _dot(jnp.where(incl, xbeta[base + p][C:], 0.0), ubd[p]) for p in range(n_pairs)]
        new_states = []
        for p in range(n_pairs):
            r, km, beta, cum, _, _, _ = pre[base + p]
            last = cum[C - 1:C, :]
            e_last = jnp.exp(last - cum)
            upd = _dot_tn(jnp.concatenate([vs[p], ubd[p][:C] + ubd[p][C:]], axis=0),
                          jnp.concatenate([km * e_last, beta * e_last], axis=0))
            new_states.append(states[p] * jnp.exp(last) + jnp.where(bdiag, upd, 0.0))
        states = new_states
        outs = []
        for p in range(n_pairs):
            sl = sls[p]
            r, km = pre[base + p][0], pre[base + p][1]
            y = ys[p]
            mu_y = head_sum(y) * (1.0 / N)
            var = head_sum(jnp.square(y - mu_y)) * (1.0 / N)
            yn = (y - mu_y) * lax.rsqrt(var + RWKV_GN_EPS) * lnw_ref[:, sl] + lnb_ref[:, sl]
            bonus = head_sum(r * km * rk_ref[:, sl]) * vs[p]
            outs.append((yn + bonus) * g_ref[rw, sl].astype(F32))
        o_ref[rw, :] = jnp.concatenate(outs, axis=-1).astype(o_ref.dtype)
    for p in range(n_pairs):
        s_ref[p] = states[p]


def rwkv_scan(r, k, v, lw, a, g, k_k, k_a, r_k, ln_w, ln_b):
    B, L, D = r.shape
    rows = RWKV_CHUNK * RWKV_SUBCHUNKS
    seq = pl.BlockSpec((None, rows, D), lambda b, c: (b, c, 0))
    par = pl.BlockSpec((1, D), lambda b, c: (0, 0))
    return pl.pallas_call(
        _rwkv_scan_kernel,
        grid=(B, L // rows),
        in_specs=[seq] * 6 + [par] * 5,
        out_specs=seq,
        out_shape=jax.ShapeDtypeStruct((B, L, D), BF16),
        scratch_shapes=[pltpu.VMEM((D // (2 * RWKV_HEAD), 2 * RWKV_HEAD, 2 * RWKV_HEAD), F32)],
        compiler_params=_cparams("parallel", "arbitrary"),
        name="rwkv_scan",
    )(r, k, v, lw, a, g, *(p.reshape(1, D) for p in (k_k, k_a, r_k, ln_w, ln_b)))


def rwkv_layer(h, nw, p):
    B, L, D = h.shape
    bf = lambda x: x.astype(BF16)
    h2 = h.reshape(B * L, D)
    outs = rwkv_proj(h2, L, nw, p['mu'],
                     bf(p['w_r']), bf(p['w_k']), bf(p['w_v']),
                     bf(p['w_lora_a']), bf(p['w_lora_b']), bf(p['a_lora_a']), bf(p['a_lora_b']),
                     bf(p['g_lora_a']), bf(p['g_lora_b']), p['w0'], p['a0'])
    r, k, v, lw, a, g = (o.reshape(B, L, D) for o in outs)
    o = rwkv_scan(r, k, v, lw, a, g, p['k_k'], p['k_a'], p['r_k'].reshape(D), p['ln_w'], p['ln_b'])
    return o.reshape(B * L, D), bf(p['w_o'])


def _sdecay_groups(qs, ks, vss, ccss, crss, s_ref, P):
    C = qs[0].shape[0]
    causal = _tri(C)
    groups = range(len(qs))
    s0s = [s_ref[g] for g in groups]
    scores = [_dot_nt(qs[g], ks[g]) for g in groups]
    qss = [_dot(qs[g], s0s[g]) for g in groups]
    yss = []
    for g in groups:
        ys = []
        for r, (v, cc, cr) in enumerate(zip(vss[g], ccss[g], crss[g])):
            dec = jnp.exp(jnp.where(causal, cc - cr, -jnp.inf))
            ys.append(_dot(scores[g] * dec, v) + jnp.exp(cc) * qss[g][:, r * P:(r + 1) * P])
        yss.append(ys)
    for g in groups:
        lasts = [cc[C - 1:C, :] for cc in ccss[g]]
        vw = jnp.concatenate([v * jnp.exp(last - cc)
                              for v, cc, last in zip(vss[g], ccss[g], lasts)], axis=-1)
        keep = jnp.concatenate([jnp.broadcast_to(jnp.exp(last), (1, P)) for last in lasts], axis=-1)
        s_ref[g] = keep * s0s[g] + _dot_tn(ks[g], vw)
    return yss


def _m2_scan_kernel(shift_ref, expand_ref, z_ref, xs_ref, xsh_ref, b_ref, bh_ref, c_ref, ch_ref,
                    wx_ref, wb_ref, wc_ref, bx_ref, bb_ref, bc_ref,
                    dtc_ref, dtr_ref, pc_ref, pr_ref, dx_ref, nw_ref, o_ref, s_ref, *, GS, R, P, N, C):
    first = pl.program_id(2) == 0

    @pl.when(first)
    def _():
        s_ref[...] = jnp.zeros_like(s_ref)

    shift = shift_ref[...]

    def conv_silu(x_ref, prev_ref, w_ref, bias_ref):
        x = x_ref[...]
        prev = prev_ref[...]
        xf = jnp.concatenate([jnp.where(first, jnp.zeros_like(prev), prev), x], axis=0)
        taps = jnp.dot(shift, xf, preferred_element_type=F32)
        w = w_ref[...]
        acc = bias_ref[...] + w[M2_CONV - 1:M2_CONV] * x.astype(F32)
        for i in range(M2_CONV - 1):
            acc = acc + w[i:i + 1] * taps[i * C:(i + 1) * C]
        return _silu(acc)

    GP = R * P
    xs = conv_silu(xs_ref, xsh_ref, wx_ref, bx_ref)
    k = conv_silu(b_ref, bh_ref, wb_ref, bb_ref)
    q = conv_silu(c_ref, ch_ref, wc_ref, bc_ref)
    pc = pc_ref[...]
    pr = pr_ref[...]
    dt_c = jax.nn.softplus(dtc_ref[...] + pc[0:1])
    dt_r = jax.nn.softplus(dtr_ref[...] + pr[:, 0:1])
    cc = _cumsum_rows(dt_c * -jnp.exp(pc[1:2]))
    cr = _cumsum_lanes(dt_r * -jnp.exp(pr[:, 1:2]))
    expand = expand_ref[...]

    def widen(x):
        return jnp.dot(jnp.concatenate(_split3(x), axis=-1), expand, preferred_element_type=F32)

    last = cc[C - 1:C]
    e_cum = widen(jnp.exp(cc))
    v = xs * widen(dt_c)
    vw = v * widen(jnp.exp(last - cc))
    keep = e_cum[C - 1:C]
    causal = _tri(C)
    head_of_lane = lax.broadcasted_iota(jnp.int32, (1, GP), 1) // P
    groups = range(GS)
    gsl = [slice(g * GP, (g + 1) * GP) for g in groups]
    qg = [q[:, g * N:(g + 1) * N] for g in groups]
    kg = [k[:, g * N:(g + 1) * N] for g in groups]
    s0s = [s_ref[g] for g in groups]
    scores = [_dot_nt(qg[g], kg[g]) for g in groups]
    qss = [_dot(qg[g], s0s[g]) for g in groups]
    pvs = []
    for g in groups:
        ps = [scores[g] * jnp.exp(jnp.where(causal, cc[:, j:j + 1] - cr[j:j + 1], -jnp.inf))
              for j in range(g * R, (g + 1) * R)]
        vbd = jnp.concatenate([jnp.where(head_of_lane == r, v[:, gsl[g]], 0.0) for r in range(R)], axis=0)
        pvs.append(_dot(jnp.concatenate(ps, axis=-1), vbd))
    for g in groups:
        s_ref[g] = keep[:, gsl[g]] * s0s[g] + _dot_tn(kg[g], vw[:, gsl[g]])
    z = z_ref[...].astype(F32)
    dx = dx_ref[...]
    outs = []
    for g in groups:
        y = pvs[g] + e_cum[:, gsl[g]] * qss[g] + dx[:, gsl[g]] * xs[:, gsl[g]]
        y = y * _silu(z[:, gsl[g]])
        outs.append(y * lax.rsqrt(jnp.mean(y * y, axis=-1, keepdims=True) + NORM_EPS))
    o_ref[...] = (jnp.concatenate(outs, axis=-1) * nw_ref[...]).astype(o_ref.dtype)


def m2_scan(zx, dt, conv_w, conv_b, dt_bias, a_log, d_skip, norm_w, *, d_inner, heads):
    B, L, _ = zx.shape
    G, N, C, GS = M2_GROUPS, M2_STATE, M2_CHUNK, M2_GROUPS_PER_STEP
    R = heads // G
    GP = d_inner // G
    NS = G // GS
    WX, WN, WR = GS * GP, GS * N, GS * R
    x0 = d_inner // WX
    b0 = 2 * d_inner // WN
    c0 = b0 + NS
    wb0 = d_inner // WN
    wc0 = wb0 + NS
    cur = lambda w, off: pl.BlockSpec((None, C, w), lambda b, g, c: (b, c, off + g))
    HB = BF16_SUBLANES
    halo = lambda w, off: pl.BlockSpec(
        (None, HB, w), lambda b, g, c: (b, jnp.maximum(c * (C // HB) - 1, 0), off + g))
    par = lambda rows, w, off: pl.BlockSpec((rows, w), lambda b, g, c: (0, off + g))
    tap_row = jnp.arange((M2_CONV - 1) * C)
    shift = (jnp.arange(HB + C)[None, :] ==
             (HB + tap_row % C - (M2_CONV - 1) + tap_row // C)[:, None]).astype(BF16)
    dt4 = dt.reshape(B, L, NS, WR)
    pcol = jnp.stack([dt_bias, a_log]).reshape(2, NS, WR).transpose(1, 0, 2)
    prow = pcol.transpose(0, 2, 1)
    expand = jnp.tile(jnp.arange(WX)[None, :] // (GP // R) == jnp.arange(WR)[:, None], (3, 1)).astype(BF16)
    cb = conv_b.reshape(1, -1)
    const = lambda x: pl.BlockSpec(x.shape, lambda b, g, c: (0, 0))
    return pl.pallas_call(
        functools.partial(_m2_scan_kernel, GS=GS, R=R, P=GP // R, N=N, C=C),
        grid=(B, NS, L // C),
        in_specs=[const(shift), const(expand),
                  cur(WX, 0), cur(WX, x0), halo(WX, x0), cur(WN, b0), halo(WN, b0), cur(WN, c0), halo(WN, c0),
                  par(M2_CONV, WX, 0), par(M2_CONV, WN, wb0), par(M2_CONV, WN, wc0),
                  par(1, WX, 0), par(1, WN, wb0), par(1, WN, wc0),
                  pl.BlockSpec((None, None, C, WR), lambda b, g, c: (b, g, c, 0)),
                  pl.BlockSpec((None, None, None, WR, C), lambda b, g, c: (b, g, c, 0, 0)),
                  pl.BlockSpec((None, 2, WR), lambda b, g, c: (g, 0, 0)),
                  pl.BlockSpec((None, WR, 2), lambda b, g, c: (g, 0, 0)),
                  par(1, WX, 0), par(1, WX, 0)],
        out_specs=cur(WX, 0),
        out_shape=jax.ShapeDtypeStruct((B, L, d_inner), BF16),
        scratch_shapes=[pltpu.VMEM((GS, N, GP), F32)],
        compiler_params=_cparams("parallel", "parallel", "arbitrary"),
        name="m2_scan",
    )(shift, expand, zx, zx, zx, zx, zx, zx, zx, conv_w, conv_w, conv_w, cb, cb, cb,
      dt4.transpose(0, 2, 1, 3), dt4.reshape(B, L // C, C, NS, WR).transpose(0, 3, 1, 4, 2), pcol, prow,
      jnp.repeat(d_skip, GP // R).reshape(1, -1), norm_w.reshape(1, -1))


def mamba2_layer(h, nw, p):
    B, L, D = h.shape
    d_inner = p['norm_w'].shape[0]
    heads = p['a_log'].shape[0]
    w_in = p['in_proj']
    n_main = w_in.shape[1] - heads
    w_dt = jnp.pad(w_in[:, n_main:], ((0, 0), (0, 128 - heads)))
    h2 = h.reshape(B * L, D)
    zx = norm_linear(h2, nw, w_in[:, :n_main].astype(BF16), 2048, BF16).reshape(B, L, n_main)
    dt = norm_linear(h2, nw, w_dt.astype(BF16), 128, F32)[:, :heads].reshape(B, L, heads)
    y = m2_scan(zx, dt, p['conv_w'], p['conv_b'], p['dt_bias'], p['a_log'], p['d'], p['norm_w'],
                d_inner=d_inner, heads=heads)
    return y.reshape(B * L, d_inner), p['out_proj'].astype(BF16)


def _ret_scan_kernel(q_ref, k_ref, v_ref, g_ref, cos_ref, sin_ref, lg_ref, o_ref, s_ref,
                     *, H, dk, dv, C, scale):
    @pl.when(pl.program_id(1) == 0)
    def _():
        s_ref[...] = jnp.zeros_like(s_ref)

    cos = cos_ref[...]
    sin = sin_ref[...]
    half = dk // 2

    def rotary(x_ref, hd):
        x1 = x_ref[:, hd * dk:hd * dk + half].astype(F32)
        x2 = x_ref[:, hd * dk + half:(hd + 1) * dk].astype(F32)
        return jnp.concatenate([x1 * cos - x2 * sin, x1 * sin + x2 * cos], axis=-1)

    lg = lg_ref[...]
    steps_c = (lax.broadcasted_iota(jnp.int32, (C, 1), 0) + 1).astype(F32)
    steps_r = (lax.broadcasted_iota(jnp.int32, (1, C), 1) + 1).astype(F32)
    yss = _sdecay_groups(
        [rotary(q_ref, hd) for hd in range(H)], [rotary(k_ref, hd) * scale for hd in range(H)],
        [[v_ref[:, hd * dv:(hd + 1) * dv]] for hd in range(H)],
        [[steps_c * lg[:, hd:hd + 1]] for hd in range(H)],
        [[steps_r * lg[:, hd:hd + 1]] for hd in range(H)], s_ref, dv)
    outs = []
    for hd in range(H):
        y = yss[hd][0]
        mu = jnp.mean(y, axis=-1, keepdims=True)
        var = jnp.mean(jnp.square(y - mu), axis=-1, keepdims=True)
        outs.append(_silu(g_ref[:, hd * dv:(hd + 1) * dv].astype(F32)) * ((y - mu) * lax.rsqrt(var + NORM_EPS)))
    o_ref[...] = jnp.concatenate(outs, axis=-1).astype(o_ref.dtype)


def ret_scan(proj, cos, sin, log_gamma, *, dk, dv):
    B, L, _ = proj.shape
    H, C = RET_HEADS, RET_CHUNK
    qk = lambda j: pl.BlockSpec((None, C, H * dk), lambda b, c: (b, c, j))
    vg = lambda j: pl.BlockSpec((None, C, H * dv), lambda b, c: (b, c, j))
    tab = pl.BlockSpec((C, dk // 2), lambda b, c: (c, 0))
    return pl.pallas_call(
        functools.partial(_ret_scan_kernel, H=H, dk=dk, dv=dv, C=C, scale=dk ** -0.5),
        grid=(B, L // C),
        in_specs=[qk(0), qk(1), vg(1), vg(2), tab, tab, pl.BlockSpec((1, H), lambda b, c: (0, 0))],
        out_specs=vg(0),
        out_shape=jax.ShapeDtypeStruct((B, L, H * dv), BF16),
        scratch_shapes=[pltpu.VMEM((H, dk, dv), F32)],
        compiler_params=_cparams("parallel", "arbitrary"),
        name="ret_scan",
    )(proj, proj, proj, proj, cos, sin, log_gamma.reshape(1, H))


def retnet_layer(h, nw, p):
    B, L, D = h.shape
    w_in = p['in_proj']
    vd = p['out_proj'].shape[0]
    dv = vd // RET_HEADS
    dk = (w_in.shape[1] - 2 * vd) // (2 * RET_HEADS)
    assert 2 * RET_HEADS * dk == vd
    h2 = h.reshape(B * L, D)
    proj = norm_linear(h2, nw, w_in.astype(BF16), 2048, BF16).reshape(B, L, -1)
    half = dk // 2
    inv_freq = 1.0 / (ROPE_BASE ** jnp.linspace(0.0, 1.0, half, dtype=F32))
    ang = jnp.arange(L, dtype=F32)[:, None] * inv_freq[None, :]
    log_gamma = jnp.log1p(-jnp.exp2(-5.0 - jnp.arange(RET_HEADS, dtype=F32)))
    y = ret_scan(proj, jnp.cos(ang), jnp.sin(ang), log_gamma, dk=dk, dv=dv)
    return y.reshape(B * L, vd), p['out_proj'].astype(BF16)


def _gla_kernel(q_ref, k_ref, v_ref, r_ref, glr_ref, gup_ref, gb_ref, nw_ref, o_ref, s_ref,
                *, H, dk, dv, scale):
    C = GLA_CHUNK
    SB = GLA_SUB

    @pl.when(pl.program_id(1) == 0)
    def _():
        s_ref[...] = jnp.zeros_like(s_ref)

    n_sub = q_ref.shape[0] // C
    rows = [slice(j * C, (j + 1) * C) for j in range(n_sub)]
    gup, gb = gup_ref[...], gb_ref[...]
    gc_alls = [_cumsum_rows(jax.nn.log_sigmoid(_dot(glr_ref[rw, :], gup) + gb) / GLA_TAU) for rw in rows]
    row_id = lax.broadcasted_iota(jnp.int32, (C, 1), 0)
    sub_row = lax.broadcasted_iota(jnp.int32, (SB, 1), 0)
    col_id = lax.broadcasted_iota(jnp.int32, (1, C), 1)
    units = [(j, hd) for j in range(n_sub) for hd in range(H)]
    qs = [q_ref[rows[j], hd * dk:(hd + 1) * dk].astype(F32) * scale for j, hd in units]
    ks = [k_ref[rows[j], hd * dk:(hd + 1) * dk].astype(F32) for j, hd in units]
    gcs = [gc_alls[j][:, hd * dk:(hd + 1) * dk] for j, hd in units]
    vs = [v_ref[rows[j], hd * dv:(hd + 1) * dv] for j, hd in units]
    a_rows = [[] for _ in units]
    for i in range(C // SB):
        lo = i * SB
        for u in range(len(units)):
            qi = qs[u][lo:lo + SB]
            gi = gcs[u][lo:lo + SB]
            ki = ks[u][lo:lo + SB]
            cols = []
            for j in range(SB):
                d = jnp.where(sub_row >= j, gi - gi[j:j + 1], -jnp.inf)
                col = jnp.sum(qi * ki[j:j + 1] * jnp.exp(d), axis=-1, keepdims=True)
                cols.append(jnp.where(col_id == lo + j, col, 0.0))
            blk = _tree_sum(cols)
            if i > 0:
                ref = gcs[u][lo - 1:lo]
                qt = qi * jnp.exp(gi - ref)
                kt = ks[u] * jnp.exp(jnp.where(row_id < lo, ref - gcs[u], -jnp.inf))
                blk = blk + jnp.where(col_id < lo, _dot_nt(qt, kt), 0.0)
            a_rows[u].append(blk)
    intra = [_dot(jnp.concatenate(a_rows[u], axis=0), vs[u]) for u in range(len(units))]
    lasts = [gc[C - 1:C] for gc in gcs]
    kvs = [_dot_tn(vs[u], ks[u] * jnp.exp(lasts[u] - gcs[u])) for u in range(len(units))]
    states = [s_ref[hd] for hd in range(H)]
    nw = nw_ref[...]
    for j in range(n_sub):
        outs = []
        for hd in range(H):
            u = j * H + hd
            o = intra[u] + _dot_nt(qs[u] * jnp.exp(gcs[u]), states[hd])
            states[hd] = states[hd] * jnp.exp(lasts[u]) + kvs[u]
            o = o * lax.rsqrt(jnp.mean(o * o, axis=-1, keepdims=True) + NORM_EPS) * nw
            outs.append(o * _silu(r_ref[rows[j], hd * dv:(hd + 1) * dv].astype(F32)))
        o_ref[rows[j], :] = jnp.concatenate(outs, axis=-1).astype(o_ref.dtype)
    for hd in range(H):
        s_ref[hd] = states[hd]


def gla_scan(proj, gate_up, gate_bias, norm_w, *, dk, dv, lora_pad):
    B, L, _ = proj.shape
    H, C = GLA_HEADS, GLA_CHUNK * GLA_SUBCHUNKS
    qk = lambda j: pl.BlockSpec((None, C, H * dk), lambda b, c: (b, c, j))
    vr = lambda j: pl.BlockSpec((None, C, H * dv), lambda b, c: (b, c, j))
    g0 = (2 * H * dk + 2 * H * dv) // lora_pad
    full = lambda x: pl.BlockSpec(x.shape, lambda b, c: (0, 0))
    consts = (gate_up, gate_bias.reshape(1, -1), norm_w.reshape(1, -1))
    return pl.pallas_call(
        functools.partial(_gla_kernel, H=H, dk=dk, dv=dv, scale=dk ** -0.5),
        grid=(B, L // C),
        in_specs=[qk(0), qk(1), vr(1), vr(2),
                  pl.BlockSpec((None, C, lora_pad), lambda b, c: (b, c, g0))] + [full(x) for x in consts],
        out_specs=vr(0),
        out_shape=jax.ShapeDtypeStruct((B, L, H * dv), BF16),
        scratch_shapes=[pltpu.VMEM((H, dv, dk), F32)],
        compiler_params=_cparams("parallel", "arbitrary"),
        name="gla_scan",
    )(proj, proj, proj, proj, proj, *consts)


def gla_layer(h, nw, p):
    B, L, D = h.shape
    lora, qk = p['gate_up'].shape
    vd = p['out_proj'].shape[0]
    assert 2 * qk == vd
    lora_pad = 128
    w_in = jnp.pad(p['in_proj'], ((0, 0), (0, lora_pad - lora)))
    n_all = w_in.shape[1]
    gate_up = jnp.pad(p['gate_up'], ((0, lora_pad - lora), (0, 0))).astype(BF16)
    h2 = h.reshape(B * L, D)
    proj = norm_linear(h2, nw, w_in.astype(BF16), n_all, BF16).reshape(B, L, n_all)
    o = gla_scan(proj, gate_up, p['gate_bias'], p['norm_w'],
                 dk=qk // GLA_HEADS, dv=vd // GLA_HEADS, lora_pad=lora_pad)
    return o.reshape(B * L, vd), p['out_proj'].astype(BF16)


def kernel(x, meta_tokens, norm_mix, norm_mlp, norm_final, mlp_w_in, mlp_w_out, rwkv_mu, rwkv_w_r, rwkv_w_k, rwkv_w_v, rwkv_w0, rwkv_w_lora_a, rwkv_w_lora_b, rwkv_a0, rwkv_a_lora_a, rwkv_a_lora_b, rwkv_g_lora_a, rwkv_g_lora_b, rwkv_k_k, rwkv_k_a, rwkv_r_k, rwkv_ln_w, rwkv_ln_b, rwkv_w_o, m2_in_proj, m2_conv_w, m2_conv_b, m2_dt_bias, m2_a_log, m2_d, m2_norm_w, m2_out_proj, gla_in_proj, gla_gate_up, gla_gate_bias, gla_norm_w, gla_out_proj, ret_in_proj, ret_out_proj):
    B, S, D = x.shape
    depth = norm_mix.shape[0]
    n_tok = N_META + S
    L = -(-n_tok // SEQ_ALIGN) * SEQ_ALIGN
    meta = jnp.broadcast_to(meta_tokens[None].astype(x.dtype), (B, N_META, D))
    h = jnp.concatenate([meta, x, jnp.zeros((B, L - n_tok, D), x.dtype)], axis=1)
    for i in range(depth):
        m, j = i % 4, i // 4
        if m == 0:
            a, wo = rwkv_layer(h, norm_mix[i], dict(
                mu=rwkv_mu[j], w_r=rwkv_w_r[j], w_k=rwkv_w_k[j], w_v=rwkv_w_v[j], w0=rwkv_w0[j],
                w_lora_a=rwkv_w_lora_a[j], w_lora_b=rwkv_w_lora_b[j], a0=rwkv_a0[j],
                a_lora_a=rwkv_a_lora_a[j], a_lora_b=rwkv_a_lora_b[j],
                g_lora_a=rwkv_g_lora_a[j], g_lora_b=rwkv_g_lora_b[j],
                k_k=rwkv_k_k[j], k_a=rwkv_k_a[j], r_k=rwkv_r_k[j],
                ln_w=rwkv_ln_w[j], ln_b=rwkv_ln_b[j], w_o=rwkv_w_o[j]))
        elif m == 1:
            a, wo = mamba2_layer(h, norm_mix[i], dict(
                in_proj=m2_in_proj[j], conv_w=m2_conv_w[j], conv_b=m2_conv_b[j],
                dt_bias=m2_dt_bias[j], a_log=m2_a_log[j], d=m2_d[j], norm_w=m2_norm_w[j],
                out_proj=m2_out_proj[j]))
        elif m == 2:
            a, wo = gla_layer(h, norm_mix[i], dict(
                in_proj=gla_in_proj[j], gate_up=gla_gate_up[j], gate_bias=gla_gate_bias[j],
                norm_w=gla_norm_w[j], out_proj=gla_out_proj[j]))
        else:
            a, wo = retnet_layer(h, norm_mix[i], dict(in_proj=ret_in_proj[j], out_proj=ret_out_proj[j]))
        h = mlp_res(linear_res(a, wo, h.reshape(B * L, D)), norm_mlp[i], mlp_w_in[i].astype(BF16),
                    mlp_w_out[i].astype(BF16), norm_final, final=(i == depth - 1)).reshape(B, L, D)
    return h[:, N_META:n_tok]
```

```python
import functools
import math

import jax
import jax.numpy as jnp
from jax import lax
from jax.experimental import pallas as pl
from jax.experimental.pallas import tpu as pltpu

F32 = jnp.float32
BF16 = jnp.bfloat16

N_META = 16
NORM_EPS = 1e-5
SEQ_ALIGN = 192
MAX_ROW_TILE = 640
MAX_WIDE_ROW_TILE = 1152
BF16_SUBLANES = 16
F32_SUBLANES = 8
VMEM_LIMIT = 56 * 1024 * 1024

RWKV_HEAD = 64
RWKV_GN_EPS = 64e-5
RWKV_CHUNK = 64
RWKV_SUBCHUNKS = 3

M2_HEAD = 64
M2_GROUPS = 8
M2_STATE = 128
M2_CONV = 4
M2_CHUNK = 192
M2_GROUPS_PER_STEP = 8

GLA_HEADS = 4
GLA_TAU = 16.0
GLA_CHUNK = 64
GLA_SUBCHUNKS = 3
GLA_SUB = 8

RET_HEADS = 4
RET_CHUNK = 192
ROPE_BASE = 10000.0


def _cparams(*sem):
    return pltpu.CompilerParams(dimension_semantics=sem, vmem_limit_bytes=VMEM_LIMIT)


def _row_tile(rows, limit=MAX_ROW_TILE):
    for t in range(min(rows, limit) // BF16_SUBLANES * BF16_SUBLANES, 0, -BF16_SUBLANES):
        if rows % t == 0:
            return t
    raise ValueError(rows)


def _dot(a, b):
    return jnp.dot(a.astype(BF16), b.astype(BF16), preferred_element_type=F32)


def _dot_nt(a, b):
    return lax.dot_general(a.astype(BF16), b.astype(BF16), (((1,), (1,)), ((), ())),
                           preferred_element_type=F32)


def _dot_tn(a, b):
    return lax.dot_general(a.astype(BF16), b.astype(BF16), (((0,), (0,)), ((), ())),
                           preferred_element_type=F32)


def _split3(x):
    hi = x.astype(BF16)
    r1 = x - hi.astype(F32)
    mid = r1.astype(BF16)
    return hi, mid, (r1 - mid.astype(F32)).astype(BF16)


def _cumsum_rows(x):
    C = x.shape[0]
    r = lax.broadcasted_iota(jnp.int32, (C, 3 * C), 0)
    c = lax.broadcasted_iota(jnp.int32, (C, 3 * C), 1)
    tri3 = (r >= lax.rem(c, C)).astype(BF16)
    return jnp.dot(tri3, jnp.concatenate(_split3(x), axis=0), preferred_element_type=F32)


def _cumsum_lanes(x):
    C = x.shape[1]
    r = lax.broadcasted_iota(jnp.int32, (3 * C, C), 0)
    c = lax.broadcasted_iota(jnp.int32, (3 * C, C), 1)
    tri3 = (lax.rem(r, C) <= c).astype(BF16)
    return jnp.dot(jnp.concatenate(_split3(x), axis=1), tri3, preferred_element_type=F32)


def _rms_rows(x, w):
    return x * lax.rsqrt(jnp.mean(x * x, axis=-1, keepdims=True) + NORM_EPS) * w


def _tri(n, strict=False, upper=False):
    r = lax.broadcasted_iota(jnp.int32, (n, n), 0)
    c = lax.broadcasted_iota(jnp.int32, (n, n), 1)
    if upper:
        r, c = c, r
    return (r > c) if strict else (r >= c)


def _silu(x):
    return x * jax.nn.sigmoid(x)


def _tree_sum(xs):
    while len(xs) > 1:
        xs = [xs[i] + xs[i + 1] if i + 1 < len(xs) else xs[i] for i in range(0, len(xs), 2)]
    return xs[0]


def _norm_linear_kernel(h_ref, nw_ref, w_ref, o_ref, u_ref):
    @pl.when(pl.program_id(1) == 0)
    def _():
        u_ref[...] = _rms_rows(h_ref[...], nw_ref[...]).astype(BF16)

    o_ref[...] = jnp.dot(u_ref[...], w_ref[...], preferred_element_type=F32).astype(o_ref.dtype)


def norm_linear(h, nw, w, tn, out_dtype):
    M, D = h.shape
    N = w.shape[1]
    tm = _row_tile(M, MAX_WIDE_ROW_TILE)
    return pl.pallas_call(
        _norm_linear_kernel,
        grid=(M // tm, N // tn),
        in_specs=[pl.BlockSpec((tm, D), lambda i, j: (i, 0)),
                  pl.BlockSpec((1, D), lambda i, j: (0, 0)),
                  pl.BlockSpec((D, tn), lambda i, j: (0, j))],
        out_specs=pl.BlockSpec((tm, tn), lambda i, j: (i, j)),
        out_shape=jax.ShapeDtypeStruct((M, N), out_dtype),
        scratch_shapes=[pltpu.VMEM((tm, D), BF16)],
        compiler_params=_cparams("parallel", "arbitrary"),
        name="norm_linear",
    )(h, nw.reshape(1, D), w)


def _linear_res_kernel(a_ref, w_ref, h_ref, o_ref):
    o_ref[...] = h_ref[...] + jnp.dot(a_ref[...], w_ref[...], preferred_element_type=F32)


def linear_res(a, w, h):
    M, K = a.shape
    D = w.shape[1]
    tm = _row_tile(M, MAX_WIDE_ROW_TILE)
    return pl.pallas_call(
        _linear_res_kernel,
        grid=(M // tm,),
        in_specs=[pl.BlockSpec((tm, K), lambda i: (i, 0)),
                  pl.BlockSpec((K, D), lambda i: (0, 0)),
                  pl.BlockSpec((tm, D), lambda i: (i, 0))],
        out_specs=pl.BlockSpec((tm, D), lambda i: (i, 0)),
        out_shape=jax.ShapeDtypeStruct((M, D), F32),
        compiler_params=_cparams("parallel"),
        name="linear_res",
    )(a, w, h)


def _mlp_kernel(h_ref, nw_ref, w1_ref, w2_ref, fnw_ref, o_ref, u_ref, *, final):
    f = pl.program_id(1)

    @pl.when(f == 0)
    def _():
        h = h_ref[...]
        u_ref[...] = _rms_rows(h, nw_ref[...]).astype(BF16)
        o_ref[...] = h

    hid = jnp.dot(u_ref[...], w1_ref[...], preferred_element_type=F32)
    hid = jnp.square(jnp.maximum(hid, 0.0))
    o_ref[...] += _dot(hid, w2_ref[...])

    if final:
        @pl.when(f == pl.num_programs(1) - 1)
        def _():
            o_ref[...] = _rms_rows(o_ref[...], fnw_ref[...])


def mlp_res(h, nw, w1, w2, fnw, final, tf=2048):
    M, D = h.shape
    F = w1.shape[1]
    tm = _row_tile(M, MAX_WIDE_ROW_TILE)
    row = pl.BlockSpec((tm, D), lambda i, f: (i, 0))
    vec = pl.BlockSpec((1, D), lambda i, f: (0, 0))
    return pl.pallas_call(
        functools.partial(_mlp_kernel, final=final),
        grid=(M // tm, F // tf),
        in_specs=[row, vec,
                  pl.BlockSpec((D, tf), lambda i, f: (0, f)),
                  pl.BlockSpec((tf, D), lambda i, f: (f, 0)), vec],
        out_specs=row,
        out_shape=jax.ShapeDtypeStruct((M, D), F32),
        scratch_shapes=[pltpu.VMEM((tm, D), BF16)],
        compiler_params=_cparams("parallel", "arbitrary"),
        name="mlp_res",
    )(h, nw.reshape(1, D), w1, w2, fnw.reshape(1, D))


def _rwkv_proj_kernel(h_ref, halo_ref, nw_ref, mu_ref, wr_ref, wk_ref, wv_ref,
                      wla_ref, wlb_ref, ala_ref, alb_ref, gla_ref, glb_ref,
                      w0_ref, a0_ref,
                      r_ref, k_ref, v_ref, lw_ref, a_ref, g_ref, *, tiles_per_batch):
    nw = nw_ref[...]
    u = _rms_rows(h_ref[...], nw)
    first = pl.program_id(0) % tiles_per_batch == 0
    prev_last = _rms_rows(halo_ref[F32_SUBLANES - 1:F32_SUBLANES, :], nw)
    prev_last = jnp.where(first, 0.0, prev_last)
    row = lax.broadcasted_iota(jnp.int32, (u.shape[0], 1), 0)
    xx = jnp.where(row == 0, prev_last, pltpu.roll(u, 1, axis=0)) - u
    mu = mu_ref[...]
    mix = lambda i: (u + xx * mu[i:i + 1, :]).astype(BF16)
    r_ref[...] = jnp.dot(mix(0), wr_ref[...], preferred_element_type=F32).astype(r_ref.dtype)
    k_ref[...] = jnp.dot(mix(2), wk_ref[...], preferred_element_type=F32).astype(k_ref.dtype)
    v_ref[...] = jnp.dot(mix(3), wv_ref[...], preferred_element_type=F32).astype(v_ref.dtype)
    hw = jnp.tanh(jnp.dot(mix(1), wla_ref[...], preferred_element_type=F32))
    w = -jax.nn.softplus(-(w0_ref[...] + _dot(hw, wlb_ref[...]))) - 0.5
    lw_ref[...] = -jnp.exp(w)
    ha = jnp.dot(mix(4), ala_ref[...], preferred_element_type=F32)
    a_ref[...] = jax.nn.sigmoid(a0_ref[...] + _dot(ha, alb_ref[...]))
    hg = jax.nn.sigmoid(jnp.dot(mix(5), gla_ref[...], preferred_element_type=F32))
    g_ref[...] = _dot(hg, glb_ref[...]).astype(g_ref.dtype)


def rwkv_proj(h, rows_per_batch, nw, mu, wr, wk, wv, wla, wlb, ala, alb, gla, glb, w0, a0):
    M, D = h.shape
    tm = _row_tile(rows_per_batch)
    row = pl.BlockSpec((tm, D), lambda i: (i, 0))
    halo = pl.BlockSpec((F32_SUBLANES, D),
                        lambda i: (jnp.maximum(i * (tm // F32_SUBLANES) - 1, 0), 0))
    full = lambda x: pl.BlockSpec(x.shape, lambda i: (0, 0))
    consts = (nw.reshape(1, D), mu, wr, wk, wv, wla, wlb, ala, alb, gla, glb,
              w0.reshape(1, D), a0.reshape(1, D))
    sds = lambda dt: jax.ShapeDtypeStruct((M, D), dt)
    return pl.pallas_call(
        functools.partial(_rwkv_proj_kernel, tiles_per_batch=rows_per_batch // tm),
        grid=(M // tm,),
        in_specs=[row, halo] + [full(c) for c in consts],
        out_specs=[row] * 6,
        out_shape=[sds(BF16), sds(BF16), sds(BF16), sds(F32), sds(F32), sds(BF16)],
        compiler_params=_cparams("parallel"),
        name="rwkv_proj",
    )(h, h, *consts)


def _rwkv_scan_kernel(r_ref, k_ref, v_ref, lw_ref, a_ref, g_ref,
                      kk_ref, ka_ref, rk_ref, lnw_ref, lnb_ref, o_ref, s_ref):
    C = RWKV_CHUNK
    N = RWKV_HEAD
    assert C == N

    @pl.when(pl.program_id(1) == 0)
    def _():
        s_ref[...] = jnp.zeros_like(s_ref)

    W = 2 * N
    lane = lax.broadcasted_iota(jnp.int32, (1, W), 1)
    left = lane < N
    row_c = lax.broadcasted_iota(jnp.int32, (C, 1), 0)
    strict = row_c > (lane & (N - 1))
    incl = row_c >= (lane & (N - 1))
    mask2 = jnp.concatenate([strict, incl], axis=0)
    row_w = lax.broadcasted_iota(jnp.int32, (W, 1), 0)
    bdiag = (row_w < N) == left
    eye = (row_w == lane).astype(F32)
    split = lambda x: jnp.concatenate([jnp.where(left, x, 0.0), jnp.where(left, 0.0, x)], axis=0)

    def head_sum(x):
        sa = jnp.sum(jnp.where(left, x, 0.0), axis=-1, keepdims=True)
        sb = jnp.sum(jnp.where(left, 0.0, x), axis=-1, keepdims=True)
        return jnp.where(left, sa, sb)

    n_pairs = r_ref.shape[-1] // W
    n_sub = r_ref.shape[0] // C
    sls = [slice(p * W, (p + 1) * W) for p in range(n_pairs)]
    items = [(j, p) for j in range(n_sub) for p in range(n_pairs)]
    rows = [slice(j * C, (j + 1) * C) for j in range(n_sub)]
    lws = [lw_ref[rw, :] for rw in rows]
    cums = [_cumsum_rows(lw) for lw in lws]
    pre = []
    for j, p in items:
        rw, sl = rows[j], sls[p]
        r = r_ref[rw, sl].astype(F32)
        k = k_ref[rw, sl].astype(F32)
        a = a_ref[rw, sl]
        cum = cums[j][:, sl]
        kkr = k * kk_ref[:, sl]
        kk = kkr * lax.rsqrt(jnp.maximum(head_sum(kkr * kkr), 1e-24))
        km = k * (1.0 + (a - 1.0) * ka_ref[:, sl])
        beta = kk * a
        e_neg = jnp.exp(-cum)
        lhs = jnp.concatenate([-kk * jnp.exp(cum - lws[j][:, sl]), r * jnp.exp(cum)], axis=0)
        pre.append((r, km, beta, cum, lhs, beta * e_neg, km * e_neg))
    x_as = [_dot_nt(jnp.where(left, q[4], 0.0), jnp.concatenate([q[5], q[6]], axis=0)) for q in pre]
    x_bs = [_dot_nt(jnp.where(left, 0.0, q[4]), jnp.concatenate([q[6], q[5]], axis=0)) for q in pre]
    xbeta = [jnp.where(left, xa, xb) for xa, xb in zip(x_as, x_bs)]
    xkey = [jnp.where(mask2, jnp.where(left, xb, xa), 0.0) for xa, xb in zip(x_as, x_bs)]
    ps = [split(jnp.where(strict, x[:C], 0.0)) for x in xbeta]
    ts = [eye + q for q in ps]
    n_fac = int(math.log2(C))
    for i in range(1, n_fac):
        if i == 1:
            ps = [_dot(q, q) for q in ps]
        zs = [_dot(q, jnp.concatenate([q, t], axis=-1)) if i < n_fac - 1 else _dot(q, t)
              for q, t in zip(ps, ts)]
        ts = [t + z[:, -W:] for t, z in zip(ts, zs)]
        ps = [z[:, :W] for z in zs]
    states = [s_ref[p] for p in range(n_pairs)]
    for j in range(n_sub):
        rw = rows[j]
        base = j * n_pairs
        vs = [v_ref[rw, sl].astype(F32) for sl in sls]
        vx = [jnp.concatenate([jnp.where(left, 0.0, v), jnp.where(left, v, 0.0)], axis=0) for v in vs]
        hss = [_dot_nt(pre[base + p][4], states[p]) + _dot(xkey[base + p], vx[p])
               for p in range(n_pairs)]
        ubd = [_dot(ts[base + p], split(hss[p][:C])) for p in range(n_pairs)]
        ys = [hss[p][C:] + _dot(jnp.where(incl, xbeta[base + p][C:], 0.0), ubd[p]) for p in range(n_pairs)]
        new_states = []
        for p in range(n_pairs):
            r, km, beta, cum, _, _, _ = pre[base + p]
            last = cum[C - 1:C, :]
            e_last = jnp.exp(last - cum)
            upd = _dot_tn(jnp.concatenate([vs[p], ubd[p][:C] + ubd[p][C:]], axis=0),
                          jnp.concatenate([km * e_last, beta * e_last], axis=0))
            new_states.append(states[p] * jnp.exp(last) + jnp.where(bdiag, upd, 0.0))
        states = new_states
        outs = []
        for p in range(n_pairs):
            sl = sls[p]
            r, km = pre[base + p][0], pre[base + p][1]
            y = ys[p]
            mu_y = head_sum(y) * (1.0 / N)
            var = head_sum(jnp.square(y - mu_y)) * (1.0 / N)
            yn = (y - mu_y) * lax.rsqrt(var + RWKV_GN_EPS) * lnw_ref[:, sl] + lnb_ref[:, sl]
            bonus = head_sum(r * km * rk_ref[:, sl]) * vs[p]
            outs.append((yn + bonus) * g_ref[rw, sl].astype(F32))
        o_ref[rw, :] = jnp.concatenate(outs, axis=-1).astype(o_ref.dtype)
    for p in range(n_pairs):
        s_ref[p] = states[p]


def rwkv_scan(r, k, v, lw, a, g, k_k, k_a, r_k, ln_w, ln_b):
    B, L, D = r.shape
    rows = RWKV_CHUNK * RWKV_SUBCHUNKS
    seq = pl.BlockSpec((None, rows, D), lambda b, c: (b, c, 0))
    par = pl.BlockSpec((1, D), lambda b, c: (0, 0))
    return pl.pallas_call(
        _rwkv_scan_kernel,
        grid=(B, L // rows),
        in_specs=[seq] * 6 + [par] * 5,
        out_specs=seq,
        out_shape=jax.ShapeDtypeStruct((B, L, D), BF16),
        scratch_shapes=[pltpu.VMEM((D // (2 * RWKV_HEAD), 2 * RWKV_HEAD, 2 * RWKV_HEAD), F32)],
        compiler_params=_cparams("parallel", "arbitrary"),
        name="rwkv_scan",
    )(r, k, v, lw, a, g, *(p.reshape(1, D) for p in (k_k, k_a, r_k, ln_w, ln_b)))


def rwkv_layer(h, nw, p):
    B, L, D = h.shape
    bf = lambda x: x.astype(BF16)
    h2 = h.reshape(B * L, D)
    outs = rwkv_proj(h2, L, nw, p['mu'],
                     bf(p['w_r']), bf(p['w_k']), bf(p['w_v']),
                     bf(p['w_lora_a']), bf(p['w_lora_b']), bf(p['a_lora_a']), bf(p['a_lora_b']),
                     bf(p['g_lora_a']), bf(p['g_lora_b']), p['w0'], p['a0'])
    r, k, v, lw, a, g = (o.reshape(B, L, D) for o in outs)
    o = rwkv_scan(r, k, v, lw, a, g, p['k_k'], p['k_a'], p['r_k'].reshape(D), p['ln_w'], p['ln_b'])
    return o.reshape(B * L, D), bf(p['w_o'])


def _sdecay_groups(qs, ks, vss, ccss, crss, s_ref, P):
    C = qs[0].shape[0]
    causal = _tri(C)
    groups = range(len(qs))
    s0s = [s_ref[g] for g in groups]
    scores = [_dot_nt(qs[g], ks[g]) for g in groups]
    qss = [_dot(qs[g], s0s[g]) for g in groups]
    yss = []
    for g in groups:
        ys = []
        for r, (v, cc, cr) in enumerate(zip(vss[g], ccss[g], crss[g])):
            dec = jnp.exp(jnp.where(causal, cc - cr, -jnp.inf))
            ys.append(_dot(scores[g] * dec, v) + jnp.exp(cc) * qss[g][:, r * P:(r + 1) * P])
        yss.append(ys)
    for g in groups:
        lasts = [cc[C - 1:C, :] for cc in ccss[g]]
        vw = jnp.concatenate([v * jnp.exp(last - cc)
                              for v, cc, last in zip(vss[g], ccss[g], lasts)], axis=-1)
        keep = jnp.concatenate([jnp.broadcast_to(jnp.exp(last), (1, P)) for last in lasts], axis=-1)
        s_ref[g] = keep * s0s[g] + _dot_tn(ks[g], vw)
    return yss


def _m2_scan_kernel(shift_ref, expand_ref, z_ref, xs_ref, xsh_ref, b_ref, bh_ref, c_ref, ch_ref,
                    wx_ref, wb_ref, wc_ref, bx_ref, bb_ref, bc_ref,
                    dtc_ref, dtr_ref, pc_ref, pr_ref, dx_ref, nw_ref, o_ref, s_ref, *, GS, R, P, N, C):
    first = pl.program_id(2) == 0

    @pl.when(first)
    def _():
        s_ref[...] = jnp.zeros_like(s_ref)

    shift = shift_ref[...]

    def conv_silu(x_ref, prev_ref, w_ref, bias_ref):
        x = x_ref[...]
        prev = prev_ref[...]
        xf = jnp.concatenate([jnp.where(first, jnp.zeros_like(prev), prev), x], axis=0)
        taps = jnp.dot(shift, xf, preferred_element_type=F32)
        w = w_ref[...]
        acc = bias_ref[...] + w[M2_CONV - 1:M2_CONV] * x.astype(F32)
        for i in range(M2_CONV - 1):
            acc = acc + w[i:i + 1] * taps[i * C:(i + 1) * C]
        return _silu(acc)

    GP = R * P
    xs = conv_silu(xs_ref, xsh_ref, wx_ref, bx_ref)
    k = conv_silu(b_ref, bh_ref, wb_ref, bb_ref)
    q = conv_silu(c_ref, ch_ref, wc_ref, bc_ref)
    pc = pc_ref[...]
    pr = pr_ref[...]
    dt_c = jax.nn.softplus(dtc_ref[...] + pc[0:1])
    dt_r = jax.nn.softplus(dtr_ref[...] + pr[:, 0:1])
    cc = _cumsum_rows(dt_c * -jnp.exp(pc[1:2]))
    cr = _cumsum_lanes(dt_r * -jnp.exp(pr[:, 1:2]))
    expand = expand_ref[...]

    def widen(x):
        return jnp.dot(jnp.concatenate(_split3(x), axis=-1), expand, preferred_element_type=F32)

    last = cc[C - 1:C]
    e_cum = widen(jnp.exp(cc))
    v = xs * widen(dt_c)
    vw = v * widen(jnp.exp(last - cc))
    keep = e_cum[C - 1:C]
    causal = _tri(C)
    head_of_lane = lax.broadcasted_iota(jnp.int32, (1, GP), 1) // P
    groups = range(GS)
    gsl = [slice(g * GP, (g + 1) * GP) for g in groups]
    qg = [q[:, g * N:(g + 1) * N] for g in groups]
    kg = [k[:, g * N:(g + 1) * N] for g in groups]
    s0s = [s_ref[g] for g in groups]
    scores = [_dot_nt(qg[g], kg[g]) for g in groups]
    qss = [_dot(qg[g], s0s[g]) for g in groups]
    pvs = []
    for g in groups:
        ps = [scores[g] * jnp.exp(jnp.where(causal, cc[:, j:j + 1] - cr[j:j + 1], -jnp.inf))
              for j in range(g * R, (g + 1) * R)]
        vbd = jnp.concatenate([jnp.where(head_of_lane == r, v[:, gsl[g]], 0.0) for r in range(R)], axis=0)
        pvs.append(_dot(jnp.concatenate(ps, axis=-1), vbd))
    for g in groups:
        s_ref[g] = keep[:, gsl[g]] * s0s[g] + _dot_tn(kg[g], vw[:, gsl[g]])
    z = z_ref[...].astype(F32)
    dx = dx_ref[...]
    outs = []
    for g in groups:
        y = pvs[g] + e_cum[:, gsl[g]] * qss[g] + dx[:, gsl[g]] * xs[:, gsl[g]]
        y = y * _silu(z[:, gsl[g]])
        outs.append(y * lax.rsqrt(jnp.mean(y * y, axis=-1, keepdims=True) + NORM_EPS))
    o_ref[...] = (jnp.concatenate(outs, axis=-1) * nw_ref[...]).astype(o_ref.dtype)


def m2_scan(zx, dt, conv_w, conv_b, dt_bias, a_log, d_skip, norm_w, *, d_inner, heads):
    B, L, _ = zx.shape
    G, N, C, GS = M2_GROUPS, M2_STATE, M2_CHUNK, M2_GROUPS_PER_STEP
    R = heads // G
    GP = d_inner // G
    NS = G // GS
    WX, WN, WR = GS * GP, GS * N, GS * R
    x0 = d_inner // WX
    b0 = 2 * d_inner // WN
    c0 = b0 + NS
    wb0 = d_inner // WN
    wc0 = wb0 + NS
    cur = lambda w, off: pl.BlockSpec((None, C, w), lambda b, g, c: (b, c, off + g))
    HB = BF16_SUBLANES
    halo = lambda w, off: pl.BlockSpec(
        (None, HB, w), lambda b, g, c: (b, jnp.maximum(c * (C // HB) - 1, 0), off + g))
    par = lambda rows, w, off: pl.BlockSpec((rows, w), lambda b, g, c: (0, off + g))
    tap_row = jnp.arange((M2_CONV - 1) * C)
    shift = (jnp.arange(HB + C)[None, :] ==
             (HB + tap_row % C - (M2_CONV - 1) + tap_row // C)[:, None]).astype(BF16)
    dt4 = dt.reshape(B, L, NS, WR)
    pcol = jnp.stack([dt_bias, a_log]).reshape(2, NS, WR).transpose(1, 0, 2)
    prow = pcol.transpose(0, 2, 1)
    expand = jnp.tile(jnp.arange(WX)[None, :] // (GP // R) == jnp.arange(WR)[:, None], (3, 1)).astype(BF16)
    cb = conv_b.reshape(1, -1)
    const = lambda x: pl.BlockSpec(x.shape, lambda b, g, c: (0, 0))
    return pl.pallas_call(
        functools.partial(_m2_scan_kernel, GS=GS, R=R, P=GP // R, N=N, C=C),
        grid=(B, NS, L // C),
        in_specs=[const(shift), const(expand),
                  cur(WX, 0), cur(WX, x0), halo(WX, x0), cur(WN, b0), halo(WN, b0), cur(WN, c0), halo(WN, c0),
                  par(M2_CONV, WX, 0), par(M2_CONV, WN, wb0), par(M2_CONV, WN, wc0),
                  par(1, WX, 0), par(1, WN, wb0), par(1, WN, wc0),
                  pl.BlockSpec((None, None, C, WR), lambda b, g, c: (b, g, c, 0)),
                  pl.BlockSpec((None, None, None, WR, C), lambda b, g, c: (b, g, c, 0, 0)),
                  pl.BlockSpec((None, 2, WR), lambda b, g, c: (g, 0, 0)),
                  pl.BlockSpec((None, WR, 2), lambda b, g, c: (g, 0, 0)),
                  par(1, WX, 0), par(1, WX, 0)],
        out_specs=cur(WX, 0),
        out_shape=jax.ShapeDtypeStruct((B, L, d_inner), BF16),
        scratch_shapes=[pltpu.VMEM((GS, N, GP), F32)],
        compiler_params=_cparams("parallel", "parallel", "arbitrary"),
        name="m2_scan",
    )(shift, expand, zx, zx, zx, zx, zx, zx, zx, conv_w, conv_w, conv_w, cb, cb, cb,
      dt4.transpose(0, 2, 1, 3), dt4.reshape(B, L // C, C, NS, WR).transpose(0, 3, 1, 4, 2), pcol, prow,
      jnp.repeat(d_skip, GP // R).reshape(1, -1), norm_w.reshape(1, -1))


def mamba2_layer(h, nw, p):
    B, L, D = h.shape
    d_inner = p['norm_w'].shape[0]
    heads = p['a_log'].shape[0]
    w_in = p['in_proj']
    n_main = w_in.shape[1] - heads
    w_dt = jnp.pad(w_in[:, n_main:], ((0, 0), (0, 128 - heads)))
    h2 = h.reshape(B * L, D)
    zx = norm_linear(h2, nw, w_in[:, :n_main].astype(BF16), 3072, BF16).reshape(B, L, n_main)
    dt = norm_linear(h2, nw, w_dt.astype(BF16), 128, F32)[:, :heads].reshape(B, L, heads)
    y = m2_scan(zx, dt, p['conv_w'], p['conv_b'], p['dt_bias'], p['a_log'], p['d'], p['norm_w'],
                d_inner=d_inner, heads=heads)
    return y.reshape(B * L, d_inner), p['out_proj'].astype(BF16)


def _ret_scan_kernel(q_ref, k_ref, v_ref, g_ref, cos_ref, sin_ref, lg_ref, o_ref, s_ref,
                     *, H, dk, dv, C, scale):
    @pl.when(pl.program_id(1) == 0)
    def _():
        s_ref[...] = jnp.zeros_like(s_ref)

    cos = cos_ref[...]
    sin = sin_ref[...]
    half = dk // 2

    def rotary(x_ref, hd):
        x1 = x_ref[:, hd * dk:hd * dk + half].astype(F32)
        x2 = x_ref[:, hd * dk + half:(hd + 1) * dk].astype(F32)
        return jnp.concatenate([x1 * cos - x2 * sin, x1 * sin + x2 * cos], axis=-1)

    lg = lg_ref[...]
    steps_c = (lax.broadcasted_iota(jnp.int32, (C, 1), 0) + 1).astype(F32)
    steps_r = (lax.broadcasted_iota(jnp.int32, (1, C), 1) + 1).astype(F32)
    yss = _sdecay_groups(
        [rotary(q_ref, hd) for hd in range(H)], [rotary(k_ref, hd) * scale for hd in range(H)],
        [[v_ref[:, hd * dv:(hd + 1) * dv]] for hd in range(H)],
        [[steps_c * lg[:, hd:hd + 1]] for hd in range(H)],
        [[steps_r * lg[:, hd:hd + 1]] for hd in range(H)], s_ref, dv)
    outs = []
    for hd in range(H):
        y = yss[hd][0]
        mu = jnp.mean(y, axis=-1, keepdims=True)
        var = jnp.mean(jnp.square(y - mu), axis=-1, keepdims=True)
        outs.append(_silu(g_ref[:, hd * dv:(hd + 1) * dv].astype(F32)) * ((y - mu) * lax.rsqrt(var + NORM_EPS)))
    o_ref[...] = jnp.concatenate(outs, axis=-1).astype(o_ref.dtype)


def ret_scan(proj, cos, sin, log_gamma, *, dk, dv):
    B, L, _ = proj.shape
    H, C = RET_HEADS, RET_CHUNK
    qk = lambda j: pl.BlockSpec((None, C, H * dk), lambda b, c: (b, c, j))
    vg = lambda j: pl.BlockSpec((None, C, H * dv), lambda b, c: (b, c, j))
    tab = pl.BlockSpec((C, dk // 2), lambda b, c: (c, 0))
    return pl.pallas_call(
        functools.partial(_ret_scan_kernel, H=H, dk=dk, dv=dv, C=C, scale=dk ** -0.5),
        grid=(B, L // C),
        in_specs=[qk(0), qk(1), vg(1), vg(2), tab, tab, pl.BlockSpec((1, H), lambda b, c: (0, 0))],
        out_specs=vg(0),
        out_shape=jax.ShapeDtypeStruct((B, L, H * dv), BF16),
        scratch_shapes=[pltpu.VMEM((H, dk, dv), F32)],
        compiler_params=_cparams("parallel", "arbitrary"),
        name="ret_scan",
    )(proj, proj, proj, proj, cos, sin, log_gamma.reshape(1, H))


def retnet_layer(h, nw, p):
    B, L, D = h.shape
    w_in = p['in_proj']
    vd = p['out_proj'].shape[0]
    dv = vd // RET_HEADS
    dk = (w_in.shape[1] - 2 * vd) // (2 * RET_HEADS)
    assert 2 * RET_HEADS * dk == vd
    h2 = h.reshape(B * L, D)
    proj = norm_linear(h2, nw, w_in.astype(BF16), 3072, BF16).reshape(B, L, -1)
    half = dk // 2
    inv_freq = 1.0 / (ROPE_BASE ** jnp.linspace(0.0, 1.0, half, dtype=F32))
    ang = jnp.arange(L, dtype=F32)[:, None] * inv_freq[None, :]
    log_gamma = jnp.log1p(-jnp.exp2(-5.0 - jnp.arange(RET_HEADS, dtype=F32)))
    y = ret_scan(proj, jnp.cos(ang), jnp.sin(ang), log_gamma, dk=dk, dv=dv)
    return y.reshape(B * L, vd), p['out_proj'].astype(BF16)


def _gla_kernel(q_ref, k_ref, v_ref, r_ref, glr_ref, gup_ref, gb_ref, nw_ref, o_ref, s_ref,
                *, H, dk, dv, scale):
    C = GLA_CHUNK
    SB = GLA_SUB

    @pl.when(pl.program_id(1) == 0)
    def _():
        s_ref[...] = jnp.zeros_like(s_ref)

    n_sub = q_ref.shape[0] // C
    rows = [slice(j * C, (j + 1) * C) for j in range(n_sub)]
    gup, gb = gup_ref[...], gb_ref[...]
    gc_alls = [_cumsum_rows(jax.nn.log_sigmoid(_dot(glr_ref[rw, :], gup) + gb) / GLA_TAU) for rw in rows]
    row_id = lax.broadcasted_iota(jnp.int32, (C, 1), 0)
    sub_row = lax.broadcasted_iota(jnp.int32, (SB, 1), 0)
    col_id = lax.broadcasted_iota(jnp.int32, (1, C), 1)
    units = [(j, hd) for j in range(n_sub) for hd in range(H)]
    qs = [q_ref[rows[j], hd * dk:(hd + 1) * dk].astype(F32) * scale for j, hd in units]
    ks = [k_ref[rows[j], hd * dk:(hd + 1) * dk].astype(F32) for j, hd in units]
    gcs = [gc_alls[j][:, hd * dk:(hd + 1) * dk] for j, hd in units]
    vs = [v_ref[rows[j], hd * dv:(hd + 1) * dv] for j, hd in units]
    a_rows = [[] for _ in units]
    for i in range(C // SB):
        lo = i * SB
        for u in range(len(units)):
            qi = qs[u][lo:lo + SB]
            gi = gcs[u][lo:lo + SB]
            ki = ks[u][lo:lo + SB]
            cols = []
            for j in range(SB):
                d = jnp.where(sub_row >= j, gi - gi[j:j + 1], -jnp.inf)
                col = jnp.sum(qi * ki[j:j + 1] * jnp.exp(d), axis=-1, keepdims=True)
                cols.append(jnp.where(col_id == lo + j, col, 0.0))
            blk = _tree_sum(cols)
            if i > 0:
                ref = gcs[u][lo - 1:lo]
                qt = qi * jnp.exp(gi - ref)
                kt = ks[u] * jnp.exp(jnp.where(row_id < lo, ref - gcs[u], -jnp.inf))
                blk = blk + jnp.where(col_id < lo, _dot_nt(qt, kt), 0.0)
            a_rows[u].append(blk)
    intra = [_dot(jnp.concatenate(a_rows[u], axis=0), vs[u]) for u in range(len(units))]
    lasts = [gc[C - 1:C] for gc in gcs]
    kvs = [_dot_tn(vs[u], ks[u] * jnp.exp(lasts[u] - gcs[u])) for u in range(len(units))]
    states = [s_ref[hd] for hd in range(H)]
    nw = nw_ref[...]
    for j in range(n_sub):
        outs = []
        for hd in range(H):
            u = j * H + hd
            o = intra[u] + _dot_nt(qs[u] * jnp.exp(gcs[u]), states[hd])
            states[hd] = states[hd] * jnp.exp(lasts[u]) + kvs[u]
            o = o * lax.rsqrt(jnp.mean(o * o, axis=-1, keepdims=True) + NORM_EPS) * nw
            outs.append(o * _silu(r_ref[rows[j], hd * dv:(hd + 1) * dv].astype(F32)))
        o_ref[rows[j], :] = jnp.concatenate(outs, axis=-1).astype(o_ref.dtype)
    for hd in range(H):
        s_ref[hd] = states[hd]


def gla_scan(proj, gate_up, gate_bias, norm_w, *, dk, dv, lora_pad):
    B, L, _ = proj.shape
    H, C = GLA_HEADS, GLA_CHUNK * GLA_SUBCHUNKS
    qk = lambda j: pl.BlockSpec((None, C, H * dk), lambda b, c: (b, c, j))
    vr = lambda j: pl.BlockSpec((None, C, H * dv), lambda b, c: (b, c, j))
    g0 = (2 * H * dk + 2 * H * dv) // lora_pad
    full = lambda x: pl.BlockSpec(x.shape, lambda b, c: (0, 0))
    consts = (gate_up, gate_bias.reshape(1, -1), norm_w.reshape(1, -1))
    return pl.pallas_call(
        functools.partial(_gla_kernel, H=H, dk=dk, dv=dv, scale=dk ** -0.5),
        grid=(B, L // C),
        in_specs=[qk(0), qk(1), vr(1), vr(2),
                  pl.BlockSpec((None, C, lora_pad), lambda b, c: (b, c, g0))] + [full(x) for x in consts],
        out_specs=vr(0),
        out_shape=jax.ShapeDtypeStruct((B, L, H * dv), BF16),
        scratch_shapes=[pltpu.VMEM((H, dv, dk), F32)],
        compiler_params=_cparams("parallel", "arbitrary"),
        name="gla_scan",
    )(proj, proj, proj, proj, proj, *consts)


def gla_layer(h, nw, p):
    B, L, D = h.shape
    lora, qk = p['gate_up'].shape
    vd = p['out_proj'].shape[0]
    assert 2 * qk == vd
    lora_pad = 128
    w_in = jnp.pad(p['in_proj'], ((0, 0), (0, lora_pad - lora)))
    n_all = w_in.shape[1]
    gate_up = jnp.pad(p['gate_up'], ((0, lora_pad - lora), (0, 0))).astype(BF16)
    h2 = h.reshape(B * L, D)
    proj = norm_linear(h2, nw, w_in.astype(BF16), n_all, BF16).reshape(B, L, n_all)
    o = gla_scan(proj, gate_up, p['gate_bias'], p['norm_w'],
                 dk=qk // GLA_HEADS, dv=vd // GLA_HEADS, lora_pad=lora_pad)
    return o.reshape(B * L, vd), p['out_proj'].astype(BF16)


def kernel(x, meta_tokens, norm_mix, norm_mlp, norm_final, mlp_w_in, mlp_w_out, rwkv_mu, rwkv_w_r, rwkv_w_k, rwkv_w_v, rwkv_w0, rwkv_w_lora_a, rwkv_w_lora_b, rwkv_a0, rwkv_a_lora_a, rwkv_a_lora_b, rwkv_g_lora_a, rwkv_g_lora_b, rwkv_k_k, rwkv_k_a, rwkv_r_k, rwkv_ln_w, rwkv_ln_b, rwkv_w_o, m2_in_proj, m2_conv_w, m2_conv_b, m2_dt_bias, m2_a_log, m2_d, m2_norm_w, m2_out_proj, gla_in_proj, gla_gate_up, gla_gate_bias, gla_norm_w, gla_out_proj, ret_in_proj, ret_out_proj):
    B, S, D = x.shape
    depth = norm_mix.shape[0]
    n_tok = N_META + S
    L = -(-n_tok // SEQ_ALIGN) * SEQ_ALIGN
    meta = jnp.broadcast_to(meta_tokens[None].astype(x.dtype), (B, N_META, D))
    h = jnp.concatenate([meta, x, jnp.zeros((B, L - n_tok, D), x.dtype)], axis=1)
    for i in range(depth):
        m, j = i % 4, i // 4
        if m == 0:
            a, wo = rwkv_layer(h, norm_mix[i], dict(
                mu=rwkv_mu[j], w_r=rwkv_w_r[j], w_k=rwkv_w_k[j], w_v=rwkv_w_v[j], w0=rwkv_w0[j],
                w_lora_a=rwkv_w_lora_a[j], w_lora_b=rwkv_w_lora_b[j], a0=rwkv_a0[j],
                a_lora_a=rwkv_a_lora_a[j], a_lora_b=rwkv_a_lora_b[j],
                g_lora_a=rwkv_g_lora_a[j], g_lora_b=rwkv_g_lora_b[j],
                k_k=rwkv_k_k[j], k_a=rwkv_k_a[j], r_k=rwkv_r_k[j],
                ln_w=rwkv_ln_w[j], ln_b=rwkv_ln_b[j], w_o=rwkv_w_o[j]))
        elif m == 1:
            a, wo = mamba2_layer(h, norm_mix[i], dict(
                in_proj=m2_in_proj[j], conv_w=m2_conv_w[j], conv_b=m2_conv_b[j],
                dt_bias=m2_dt_bias[j], a_log=m2_a_log[j], d=m2_d[j], norm_w=m2_norm_w[j],
                out_proj=m2_out_proj[j]))
        elif m == 2:
            a, wo = gla_layer(h, norm_mix[i], dict(
                in_proj=gla_in_proj[j], gate_up=gla_gate_up[j], gate_bias=gla_gate_bias[j],
                norm_w=gla_norm_w[j], out_proj=gla_out_proj[j]))
        else:
            a, wo = retnet_layer(h, norm_mix[i], dict(in_proj=ret_in_proj[j], out_proj=ret_out_proj[j]))
        h = mlp_res(linear_res(a, wo, h.reshape(B * L, D)), norm_mlp[i], mlp_w_in[i].astype(BF16),
                    mlp_w_out[i].astype(BF16), norm_final, final=(i == depth - 1)).reshape(B, L, D)
    return h[:, N_META:n_tok]
```

```python
import functools
import math

import jax
import jax.numpy as jnp
from jax import lax
from jax.experimental import pallas as pl
from jax.experimental.pallas import tpu as pltpu

F32 = jnp.float32
BF16 = jnp.bfloat16

N_META = 16
NORM_EPS = 1e-5
SEQ_ALIGN = 192
MAX_ROW_TILE = 640
MAX_WIDE_ROW_TILE = 1152
BF16_SUBLANES = 16
F32_SUBLANES = 8
VMEM_LIMIT = 56 * 1024 * 1024

RWKV_HEAD = 64
RWKV_GN_EPS = 64e-5
RWKV_CHUNK = 64
RWKV_SUBCHUNKS = 3

M2_HEAD = 64
M2_GROUPS = 8
M2_STATE = 128
M2_CONV = 4
M2_CHUNK = 192
M2_GROUPS_PER_STEP = 8

GLA_HEADS = 4
GLA_TAU = 16.0
GLA_CHUNK = 64
GLA_SUBCHUNKS = 3
GLA_SUB = 8

RET_HEADS = 4
RET_CHUNK = 192
ROPE_BASE = 10000.0


def _cparams(*sem):
    return pltpu.CompilerParams(dimension_semantics=sem, vmem_limit_bytes=VMEM_LIMIT)


def _row_tile(rows, limit=MAX_ROW_TILE):
    for t in range(min(rows, limit) // BF16_SUBLANES * BF16_SUBLANES, 0, -BF16_SUBLANES):
        if rows % t == 0:
            return t
    raise ValueError(rows)


def _dot(a, b):
    return jnp.dot(a.astype(BF16), b.astype(BF16), preferred_element_type=F32)


def _dot_nt(a, b):
    return lax.dot_general(a.astype(BF16), b.astype(BF16), (((1,), (1,)), ((), ())),
                           preferred_element_type=F32)


def _dot_tn(a, b):
    return lax.dot_general(a.astype(BF16), b.astype(BF16), (((0,), (0,)), ((), ())),
                           preferred_element_type=F32)


def _split3(x):
    hi = x.astype(BF16)
    r1 = x - hi.astype(F32)
    mid = r1.astype(BF16)
    return hi, mid, (r1 - mid.astype(F32)).astype(BF16)


def _cumsum_rows(x):
    C = x.shape[0]
    r = lax.broadcasted_iota(jnp.int32, (C, 3 * C), 0)
    c = lax.broadcasted_iota(jnp.int32, (C, 3 * C), 1)
    tri3 = (r >= lax.rem(c, C)).astype(BF16)
    return jnp.dot(tri3, jnp.concatenate(_split3(x), axis=0), preferred_element_type=F32)


def _cumsum_lanes(x):
    C = x.shape[1]
    r = lax.broadcasted_iota(jnp.int32, (3 * C, C), 0)
    c = lax.broadcasted_iota(jnp.int32, (3 * C, C), 1)
    tri3 = (lax.rem(r, C) <= c).astype(BF16)
    return jnp.dot(jnp.concatenate(_split3(x), axis=1), tri3, preferred_element_type=F32)


def _rms_rows(x, w):
    return x * lax.rsqrt(jnp.mean(x * x, axis=-1, keepdims=True) + NORM_EPS) * w


def _tri(n, strict=False, upper=False):
    r = lax.broadcasted_iota(jnp.int32, (n, n), 0)
    c = lax.broadcasted_iota(jnp.int32, (n, n), 1)
    if upper:
        r, c = c, r
    return (r > c) if strict else (r >= c)


def _silu(x):
    half = 0.5 * x
    return half + half * jnp.tanh(half)


def _tree_sum(xs):
    while len(xs) > 1:
        xs = [xs[i] + xs[i + 1] if i + 1 < len(xs) else xs[i] for i in range(0, len(xs), 2)]
    return xs[0]


def _norm_linear_kernel(h_ref, nw_ref, w_ref, o_ref, u_ref):
    @pl.when(pl.program_id(1) == 0)
    def _():
        u_ref[...] = _rms_rows(h_ref[...], nw_ref[...]).astype(BF16)

    o_ref[...] = jnp.dot(u_ref[...], w_ref[...], preferred_element_type=F32).astype(o_ref.dtype)


def norm_linear(h, nw, w, tn, out_dtype):
    M, D = h.shape
    N = w.shape[1]
    tm = _row_tile(M, MAX_WIDE_ROW_TILE)
    return pl.pallas_call(
        _norm_linear_kernel,
        grid=(M // tm, N // tn),
        in_specs=[pl.BlockSpec((tm, D), lambda i, j: (i, 0)),
                  pl.BlockSpec((1, D), lambda i, j: (0, 0)),
                  pl.BlockSpec((D, tn), lambda i, j: (0, j))],
        out_specs=pl.BlockSpec((tm, tn), lambda i, j: (i, j)),
        out_shape=jax.ShapeDtypeStruct((M, N), out_dtype),
        scratch_shapes=[pltpu.VMEM((tm, D), BF16)],
        compiler_params=_cparams("parallel", "arbitrary"),
        name="norm_linear",
    )(h, nw.reshape(1, D), w)


def _linear_res_kernel(a_ref, w_ref, h_ref, o_ref):
    o_ref[...] = h_ref[...] + jnp.dot(a_ref[...], w_ref[...], preferred_element_type=F32)


def linear_res(a, w, h):
    M, K = a.shape
    D = w.shape[1]
    tm = _row_tile(M, MAX_WIDE_ROW_TILE)
    return pl.pallas_call(
        _linear_res_kernel,
        grid=(M // tm,),
        in_specs=[pl.BlockSpec((tm, K), lambda i: (i, 0)),
                  pl.BlockSpec((K, D), lambda i: (0, 0)),
                  pl.BlockSpec((tm, D), lambda i: (i, 0))],
        out_specs=pl.BlockSpec((tm, D), lambda i: (i, 0)),
        out_shape=jax.ShapeDtypeStruct((M, D), F32),
        compiler_params=_cparams("parallel"),
        name="linear_res",
    )(a, w, h)


def _mlp_kernel(h_ref, nw_ref, w1_ref, w2_ref, fnw_ref, o_ref, u_ref, *, final):
    f = pl.program_id(1)

    @pl.when(f == 0)
    def _():
        h = h_ref[...]
        u_ref[...] = _rms_rows(h, nw_ref[...]).astype(BF16)
        o_ref[...] = h

    hid = jnp.dot(u_ref[...], w1_ref[...], preferred_element_type=F32)
    hid = jnp.square(jnp.maximum(hid, 0.0))
    o_ref[...] += _dot(hid, w2_ref[...])

    if final:
        @pl.when(f == pl.num_programs(1) - 1)
        def _():
            o_ref[...] = _rms_rows(o_ref[...], fnw_ref[...])


def mlp_res(h, nw, w1, w2, fnw, final, tf=2048):
    M, D = h.shape
    F = w1.shape[1]
    tm = _row_tile(M, MAX_WIDE_ROW_TILE)
    row = pl.BlockSpec((tm, D), lambda i, f: (i, 0))
    vec = pl.BlockSpec((1, D), lambda i, f: (0, 0))
    return pl.pallas_call(
        functools.partial(_mlp_kernel, final=final),
        grid=(M // tm, F // tf),
        in_specs=[row, vec,
                  pl.BlockSpec((D, tf), lambda i, f: (0, f)),
                  pl.BlockSpec((tf, D), lambda i, f: (f, 0)), vec],
        out_specs=row,
        out_shape=jax.ShapeDtypeStruct((M, D), F32),
        scratch_shapes=[pltpu.VMEM((tm, D), BF16)],
        compiler_params=_cparams("parallel", "arbitrary"),
        name="mlp_res",
    )(h, nw.reshape(1, D), w1, w2, fnw.reshape(1, D))


def _rwkv_proj_kernel(h_ref, halo_ref, nw_ref, mu_ref, wr_ref, wk_ref, wv_ref,
                      wla_ref, wlb_ref, ala_ref, alb_ref, gla_ref, glb_ref,
                      w0_ref, a0_ref,
                      r_ref, k_ref, v_ref, lw_ref, a_ref, g_ref, *, tiles_per_batch):
    nw = nw_ref[...]
    u = _rms_rows(h_ref[...], nw)
    first = pl.program_id(0) % tiles_per_batch == 0
    prev_last = _rms_rows(halo_ref[F32_SUBLANES - 1:F32_SUBLANES, :], nw)
    prev_last = jnp.where(first, 0.0, prev_last)
    row = lax.broadcasted_iota(jnp.int32, (u.shape[0], 1), 0)
    xx = jnp.where(row == 0, prev_last, pltpu.roll(u, 1, axis=0)) - u
    mu = mu_ref[...]
    mix = lambda i: (u + xx * mu[i:i + 1, :]).astype(BF16)
    r_ref[...] = jnp.dot(mix(0), wr_ref[...], preferred_element_type=F32).astype(r_ref.dtype)
    k_ref[...] = jnp.dot(mix(2), wk_ref[...], preferred_element_type=F32).astype(k_ref.dtype)
    v_ref[...] = jnp.dot(mix(3), wv_ref[...], preferred_element_type=F32).astype(v_ref.dtype)
    hw = jnp.tanh(jnp.dot(mix(1), wla_ref[...], preferred_element_type=F32))
    w = -jax.nn.softplus(-(w0_ref[...] + _dot(hw, wlb_ref[...]))) - 0.5
    lw_ref[...] = -jnp.exp(w)
    ha = jnp.dot(mix(4), ala_ref[...], preferred_element_type=F32)
    a_ref[...] = jax.nn.sigmoid(a0_ref[...] + _dot(ha, alb_ref[...]))
    hg = jax.nn.sigmoid(jnp.dot(mix(5), gla_ref[...], preferred_element_type=F32))
    g_ref[...] = _dot(hg, glb_ref[...]).astype(g_ref.dtype)


def rwkv_proj(h, rows_per_batch, nw, mu, wr, wk, wv, wla, wlb, ala, alb, gla, glb, w0, a0):
    M, D = h.shape
    tm = _row_tile(rows_per_batch)
    row = pl.BlockSpec((tm, D), lambda i: (i, 0))
    halo = pl.BlockSpec((F32_SUBLANES, D),
                        lambda i: (jnp.maximum(i * (tm // F32_SUBLANES) - 1, 0), 0))
    full = lambda x: pl.BlockSpec(x.shape, lambda i: (0, 0))
    consts = (nw.reshape(1, D), mu, wr, wk, wv, wla, wlb, ala, alb, gla, glb,
              w0.reshape(1, D), a0.reshape(1, D))
    sds = lambda dt: jax.ShapeDtypeStruct((M, D), dt)
    return pl.pallas_call(
        functools.partial(_rwkv_proj_kernel, tiles_per_batch=rows_per_batch // tm),
        grid=(M // tm,),
        in_specs=[row, halo] + [full(c) for c in consts],
        out_specs=[row] * 6,
        out_shape=[sds(BF16), sds(BF16), sds(BF16), sds(F32), sds(F32), sds(BF16)],
        compiler_params=_cparams("parallel"),
        name="rwkv_proj",
    )(h, h, *consts)


def _rwkv_scan_kernel(r_ref, k_ref, v_ref, lw_ref, a_ref, g_ref,
                      kk_ref, ka_ref, rk_ref, lnw_ref, lnb_ref, o_ref, s_ref):
    C = RWKV_CHUNK
    N = RWKV_HEAD
    assert C == N

    @pl.when(pl.program_id(1) == 0)
    def _():
        s_ref[...] = jnp.zeros_like(s_ref)

    W = 2 * N
    lane = lax.broadcasted_iota(jnp.int32, (1, W), 1)
    left = lane < N
    row_c = lax.broadcasted_iota(jnp.int32, (C, 1), 0)
    strict = row_c > (lane & (N - 1))
    incl = row_c >= (lane & (N - 1))
    mask2 = jnp.concatenate([strict, incl], axis=0)
    row_w = lax.broadcasted_iota(jnp.int32, (W, 1), 0)
    bdiag = (row_w < N) == left
    eye = (row_w == lane).astype(F32)
    split = lambda x: jnp.concatenate([jnp.where(left, x, 0.0), jnp.where(left, 0.0, x)], axis=0)

    def head_sum(x):
        sa = jnp.sum(jnp.where(left, x, 0.0), axis=-1, keepdims=True)
        sb = jnp.sum(jnp.where(left, 0.0, x), axis=-1, keepdims=True)
        return jnp.where(left, sa, sb)

    n_pairs = r_ref.shape[-1] // W
    n_sub = r_ref.shape[0] // C
    sls = [slice(p * W, (p + 1) * W) for p in range(n_pairs)]
    items = [(j, p) for j in range(n_sub) for p in range(n_pairs)]
    rows = [slice(j * C, (j + 1) * C) for j in range(n_sub)]
    lws = [lw_ref[rw, :] for rw in rows]
    cums = [_cumsum_rows(lw) for lw in lws]
    pre = []
    for j, p in items:
        rw, sl = rows[j], sls[p]
        r = r_ref[rw, sl].astype(F32)
        k = k_ref[rw, sl].astype(F32)
        a = a_ref[rw, sl]
        cum = cums[j][:, sl]
        kkr = k * kk_ref[:, sl]
        kk = kkr * lax.rsqrt(jnp.maximum(head_sum(kkr * kkr), 1e-24))
        km = k * (1.0 + (a - 1.0) * ka_ref[:, sl])
        beta = kk * a
        e_neg = jnp.exp(-cum)
        lhs = jnp.concatenate([-kk * jnp.exp(cum - lws[j][:, sl]), r * jnp.exp(cum)], axis=0)
        pre.append((r, km, beta, cum, lhs, beta * e_neg, km * e_neg))
    x_as = [_dot_nt(jnp.where(left, q[4], 0.0), jnp.concatenate([q[5], q[6]], axis=0)) for q in pre]
    x_bs = [_dot_nt(jnp.where(left, 0.0, q[4]), jnp.concatenate([q[6], q[5]], axis=0)) for q in pre]
    xbeta = [jnp.where(left, xa, xb) for xa, xb in zip(x_as, x_bs)]
    xkey = [jnp.where(mask2, jnp.where(left, xb, xa), 0.0) for xa, xb in zip(x_as, x_bs)]
    ps = [split(jnp.where(strict, x[:C], 0.0)) for x in xbeta]
    ts = [eye + q for q in ps]
    n_fac = int(math.log2(C))
    for i in range(1, n_fac):
        if i == 1:
            ps = [_dot(q, q) for q in ps]
        zs = [_dot(q, jnp.concatenate([q, t], axis=-1)) if i < n_fac - 1 else _dot(q, t)
              for q, t in zip(ps, ts)]
        ts = [t + z[:, -W:] for t, z in zip(ts, zs)]
        ps = [z[:, :W] for z in zs]
    states = [s_ref[p] for p in range(n_pairs)]
    for j in range(n_sub):
        rw = rows[j]
        base = j * n_pairs
        vs = [v_ref[rw, sl].astype(F32) for sl in sls]
        vx = [jnp.concatenate([jnp.where(left, 0.0, v), jnp.where(left, v, 0.0)], axis=0) for v in vs]
        hss = [_dot_nt(pre[base + p][4], states[p]) + _dot(xkey[base + p], vx[p])
               for p in range(n_pairs)]
        ubd = [_dot(ts[base + p], split(hss[p][:C])) for p in range(n_pairs)]
        ys = [hss[p][C:] + _dot(jnp.where(incl, xbeta[base + p][C:], 0.0), ubd[p]) for p in range(n_pairs)]
        new_states = []
        for p in range(n_pairs):
            r, km, beta, cum, _, _, _ = pre[base + p]
            last = cum[C - 1:C, :]
            e_last = jnp.exp(last - cum)
            upd = _dot_tn(jnp.concatenate([vs[p], ubd[p][:C] + ubd[p][C:]], axis=0),
                          jnp.concatenate([km * e_last, beta * e_last], axis=0))
            new_states.append(states[p] * jnp.exp(last) + jnp.where(bdiag, upd, 0.0))
        states = new_states
        outs = []
        for p in range(n_pairs):
            sl = sls[p]
            r, km = pre[base + p][0], pre[base + p][1]
            y = ys[p]
            mu_y = head_sum(y) * (1.0 / N)
            var = head_sum(jnp.square(y - mu_y)) * (1.0 / N)
            yn = (y - mu_y) * lax.rsqrt(var + RWKV_GN_EPS) * lnw_ref[:, sl] + lnb_ref[:, sl]
            bonus = head_sum(r * km * rk_ref[:, sl]) * vs[p]
            outs.append((yn + bonus) * g_ref[rw, sl].astype(F32))
        o_ref[rw, :] = jnp.concatenate(outs, axis=-1).astype(o_ref.dtype)
    for p in range(n_pairs):
        s_ref[p] = states[p]


def rwkv_scan(r, k, v, lw, a, g, k_k, k_a, r_k, ln_w, ln_b):
    B, L, D = r.shape
    rows = RWKV_CHUNK * RWKV_SUBCHUNKS
    seq = pl.BlockSpec((None, rows, D), lambda b, c: (b, c, 0))
    par = pl.BlockSpec((1, D), lambda b, c: (0, 0))
    return pl.pallas_call(
        _rwkv_scan_kernel,
        grid=(B, L // rows),
        in_specs=[seq] * 6 + [par] * 5,
        out_specs=seq,
        out_shape=jax.ShapeDtypeStruct((B, L, D), BF16),
        scratch_shapes=[pltpu.VMEM((D // (2 * RWKV_HEAD), 2 * RWKV_HEAD, 2 * RWKV_HEAD), F32)],
        compiler_params=_cparams("parallel", "arbitrary"),
        name="rwkv_scan",
    )(r, k, v, lw, a, g, *(p.reshape(1, D) for p in (k_k, k_a, r_k, ln_w, ln_b)))


def rwkv_layer(h, nw, p):
    B, L, D = h.shape
    bf = lambda x: x.astype(BF16)
    h2 = h.reshape(B * L, D)
    outs = rwkv_proj(h2, L, nw, p['mu'],
                     bf(p['w_r']), bf(p['w_k']), bf(p['w_v']),
                     bf(p['w_lora_a']), bf(p['w_lora_b']), bf(p['a_lora_a']), bf(p['a_lora_b']),
                     bf(p['g_lora_a']), bf(p['g_lora_b']), p['w0'], p['a0'])
    r, k, v, lw, a, g = (o.reshape(B, L, D) for o in outs)
    o = rwkv_scan(r, k, v, lw, a, g, p['k_k'], p['k_a'], p['r_k'].reshape(D), p['ln_w'], p['ln_b'])
    return o.reshape(B * L, D), bf(p['w_o'])


def _sdecay_groups(qs, ks, vss, ccss, crss, s_ref, P):
    C = qs[0].shape[0]
    causal = _tri(C)
    groups = range(len(qs))
    s0s = [s_ref[g] for g in groups]
    scores = [_dot_nt(qs[g], ks[g]) for g in groups]
    qss = [_dot(qs[g], s0s[g]) for g in groups]
    yss = []
    for g in groups:
        ys = []
        for r, (v, cc, cr) in enumerate(zip(vss[g], ccss[g], crss[g])):
            dec = jnp.exp(jnp.where(causal, cc - cr, -jnp.inf))
            ys.append(_dot(scores[g] * dec, v) + jnp.exp(cc) * qss[g][:, r * P:(r + 1) * P])
        yss.append(ys)
    for g in groups:
        lasts = [cc[C - 1:C, :] for cc in ccss[g]]
        vw = jnp.concatenate([v * jnp.exp(last - cc)
                              for v, cc, last in zip(vss[g], ccss[g], lasts)], axis=-1)
        keep = jnp.concatenate([jnp.broadcast_to(jnp.exp(last), (1, P)) for last in lasts], axis=-1)
        s_ref[g] = keep * s0s[g] + _dot_tn(ks[g], vw)
    return yss


def _m2_scan_kernel(shift_ref, expand_ref, z_ref, xs_ref, xsh_ref, b_ref, bh_ref, c_ref, ch_ref,
                    wx_ref, wb_ref, wc_ref, bx_ref, bb_ref, bc_ref,
                    dtc_ref, dtr_ref, pc_ref, pr_ref, dx_ref, nw_ref, o_ref, s_ref, *, GS, R, P, N, C):
    first = pl.program_id(2) == 0

    @pl.when(first)
    def _():
        s_ref[...] = jnp.zeros_like(s_ref)

    shift = shift_ref[...]

    def conv_silu(x_ref, prev_ref, w_ref, bias_ref):
        x = x_ref[...]
        prev = prev_ref[...]
        xf = jnp.concatenate([jnp.where(first, jnp.zeros_like(prev), prev), x], axis=0)
        taps = jnp.dot(shift, xf, preferred_element_type=F32)
        w = w_ref[...]
        acc = bias_ref[...] + w[M2_CONV - 1:M2_CONV] * x.astype(F32)
        for i in range(M2_CONV - 1):
            acc = acc + w[i:i + 1] * taps[i * C:(i + 1) * C]
        return _silu(acc)

    GP = R * P
    xs = conv_silu(xs_ref, xsh_ref, wx_ref, bx_ref)
    k = conv_silu(b_ref, bh_ref, wb_ref, bb_ref)
    q = conv_silu(c_ref, ch_ref, wc_ref, bc_ref)
    pc = pc_ref[...]
    pr = pr_ref[...]
    dt_c = jax.nn.softplus(dtc_ref[...] + pc[0:1])
    dt_r = jax.nn.softplus(dtr_ref[...] + pr[:, 0:1])
    cc = _cumsum_rows(dt_c * -jnp.exp(pc[1:2]))
    cr = _cumsum_lanes(dt_r * -jnp.exp(pr[:, 1:2]))
    expand = expand_ref[...]

    def widen(x):
        return jnp.dot(jnp.concatenate(_split3(x), axis=-1), expand, preferred_element_type=F32)

    last = cc[C - 1:C]
    e_cum = widen(jnp.exp(cc))
    v = xs * widen(dt_c)
    vw = v * widen(jnp.exp(last - cc))
    keep = e_cum[C - 1:C]
    causal = _tri(C)
    head_of_lane = lax.broadcasted_iota(jnp.int32, (1, GP), 1) // P
    groups = range(GS)
    gsl = [slice(g * GP, (g + 1) * GP) for g in groups]
    qg = [q[:, g * N:(g + 1) * N] for g in groups]
    kg = [k[:, g * N:(g + 1) * N] for g in groups]
    s0s = [s_ref[g] for g in groups]
    scores = [_dot_nt(qg[g], kg[g]) for g in groups]
    qss = [_dot(qg[g], s0s[g]) for g in groups]
    pvs = []
    for g in groups:
        ps = [scores[g] * jnp.exp(jnp.where(causal, cc[:, j:j + 1] - cr[j:j + 1], -jnp.inf))
              for j in range(g * R, (g + 1) * R)]
        vbd = jnp.concatenate([jnp.where(head_of_lane == r, v[:, gsl[g]], 0.0) for r in range(R)], axis=0)
        pvs.append(_dot(jnp.concatenate(ps, axis=-1), vbd))
    for g in groups:
        s_ref[g] = keep[:, gsl[g]] * s0s[g] + _dot_tn(kg[g], vw[:, gsl[g]])
    z = z_ref[...].astype(F32)
    dx = dx_ref[...]
    outs = []
    for g in groups:
        y = pvs[g] + e_cum[:, gsl[g]] * qss[g] + dx[:, gsl[g]] * xs[:, gsl[g]]
        y = y * _silu(z[:, gsl[g]])
        outs.append(y * lax.rsqrt(jnp.mean(y * y, axis=-1, keepdims=True) + NORM_EPS))
    o_ref[...] = (jnp.concatenate(outs, axis=-1) * nw_ref[...]).astype(o_ref.dtype)


def m2_scan(zx, dt, conv_w, conv_b, dt_bias, a_log, d_skip, norm_w, *, d_inner, heads):
    B, L, _ = zx.shape
    G, N, C, GS = M2_GROUPS, M2_STATE, M2_CHUNK, M2_GROUPS_PER_STEP
    R = heads // G
    GP = d_inner // G
    NS = G // GS
    WX, WN, WR = GS * GP, GS * N, GS * R
    x0 = d_inner // WX
    b0 = 2 * d_inner // WN
    c0 = b0 + NS
    wb0 = d_inner // WN
    wc0 = wb0 + NS
    cur = lambda w, off: pl.BlockSpec((None, C, w), lambda b, g, c: (b, c, off + g))
    HB = BF16_SUBLANES
    halo = lambda w, off: pl.BlockSpec(
        (None, HB, w), lambda b, g, c: (b, jnp.maximum(c * (C // HB) - 1, 0), off + g))
    par = lambda rows, w, off: pl.BlockSpec((rows, w), lambda b, g, c: (0, off + g))
    tap_row = jnp.arange((M2_CONV - 1) * C)
    shift = (jnp.arange(HB + C)[None, :] ==
             (HB + tap_row % C - (M2_CONV - 1) + tap_row // C)[:, None]).astype(BF16)
    dt4 = dt.reshape(B, L, NS, WR)
    pcol = jnp.stack([dt_bias, a_log]).reshape(2, NS, WR).transpose(1, 0, 2)
    prow = pcol.transpose(0, 2, 1)
    expand = jnp.tile(jnp.arange(WX)[None, :] // (GP // R) == jnp.arange(WR)[:, None], (3, 1)).astype(BF16)
    cb = conv_b.reshape(1, -1)
    const = lambda x: pl.BlockSpec(x.shape, lambda b, g, c: (0, 0))
    return pl.pallas_call(
        functools.partial(_m2_scan_kernel, GS=GS, R=R, P=GP // R, N=N, C=C),
        grid=(B, NS, L // C),
        in_specs=[const(shift), const(expand),
                  cur(WX, 0), cur(WX, x0), halo(WX, x0), cur(WN, b0), halo(WN, b0), cur(WN, c0), halo(WN, c0),
                  par(M2_CONV, WX, 0), par(M2_CONV, WN, wb0), par(M2_CONV, WN, wc0),
                  par(1, WX, 0), par(1, WN, wb0), par(1, WN, wc0),
                  pl.BlockSpec((None, None, C, WR), lambda b, g, c: (b, g, c, 0)),
                  pl.BlockSpec((None, None, None, WR, C), lambda b, g, c: (b, g, c, 0, 0)),
                  pl.BlockSpec((None, 2, WR), lambda b, g, c: (g, 0, 0)),
                  pl.BlockSpec((None, WR, 2), lambda b, g, c: (g, 0, 0)),
                  par(1, WX, 0), par(1, WX, 0)],
        out_specs=cur(WX, 0),
        out_shape=jax.ShapeDtypeStruct((B, L, d_inner), BF16),
        scratch_shapes=[pltpu.VMEM((GS, N, GP), F32)],
        compiler_params=_cparams("parallel", "parallel", "arbitrary"),
        name="m2_scan",
    )(shift, expand, zx, zx, zx, zx, zx, zx, zx, conv_w, conv_w, conv_w, cb, cb, cb,
      dt4.transpose(0, 2, 1, 3), dt4.reshape(B, L // C, C, NS, WR).transpose(0, 3, 1, 4, 2), pcol, prow,
      jnp.repeat(d_skip, GP // R).reshape(1, -1), norm_w.reshape(1, -1))


def mamba2_layer(h, nw, p):
    B, L, D = h.shape
    d_inner = p['norm_w'].shape[0]
    heads = p['a_log'].shape[0]
    w_in = p['in_proj']
    n_main = w_in.shape[1] - heads
    w_dt = jnp.pad(w_in[:, n_main:], ((0, 0), (0, 128 - heads)))
    h2 = h.reshape(B * L, D)
    zx = norm_linear(h2, nw, w_in[:, :n_main].astype(BF16), 3072, BF16).reshape(B, L, n_main)
    dt = norm_linear(h2, nw, w_dt.astype(BF16), 128, F32)[:, :heads].reshape(B, L, heads)
    y = m2_scan(zx, dt, p['conv_w'], p['conv_b'], p['dt_bias'], p['a_log'], p['d'], p['norm_w'],
                d_inner=d_inner, heads=heads)
    return y.reshape(B * L, d_inner), p['out_proj'].astype(BF16)


def _ret_scan_kernel(q_ref, k_ref, v_ref, g_ref, cos_ref, sin_ref, lg_ref, o_ref, s_ref,
                     *, H, dk, dv, C, scale):
    @pl.when(pl.program_id(1) == 0)
    def _():
        s_ref[...] = jnp.zeros_like(s_ref)

    cos = cos_ref[...]
    sin = sin_ref[...]
    half = dk // 2

    def rotary(x_ref, hd):
        x1 = x_ref[:, hd * dk:hd * dk + half].astype(F32)
        x2 = x_ref[:, hd * dk + half:(hd + 1) * dk].astype(F32)
        return jnp.concatenate([x1 * cos - x2 * sin, x1 * sin + x2 * cos], axis=-1)

    lg = lg_ref[...]
    steps_c = (lax.broadcasted_iota(jnp.int32, (C, 1), 0) + 1).astype(F32)
    steps_r = (lax.broadcasted_iota(jnp.int32, (1, C), 1) + 1).astype(F32)
    yss = _sdecay_groups(
        [rotary(q_ref, hd) for hd in range(H)], [rotary(k_ref, hd) * scale for hd in range(H)],
        [[v_ref[:, hd * dv:(hd + 1) * dv]] for hd in range(H)],
        [[steps_c * lg[:, hd:hd + 1]] for hd in range(H)],
        [[steps_r * lg[:, hd:hd + 1]] for hd in range(H)], s_ref, dv)
    outs = []
    for hd in range(H):
        y = yss[hd][0]
        mu = jnp.mean(y, axis=-1, keepdims=True)
        var = jnp.mean(jnp.square(y - mu), axis=-1, keepdims=True)
        outs.append(_silu(g_ref[:, hd * dv:(hd + 1) * dv].astype(F32)) * ((y - mu) * lax.rsqrt(var + NORM_EPS)))
    o_ref[...] = jnp.concatenate(outs, axis=-1).astype(o_ref.dtype)


def ret_scan(proj, cos, sin, log_gamma, *, dk, dv):
    B, L, _ = proj.shape
    H, C = RET_HEADS, RET_CHUNK
    qk = lambda j: pl.BlockSpec((None, C, H * dk), lambda b, c: (b, c, j))
    vg = lambda j: pl.BlockSpec((None, C, H * dv), lambda b, c: (b, c, j))
    tab = pl.BlockSpec((C, dk // 2), lambda b, c: (c, 0))
    return pl.pallas_call(
        functools.partial(_ret_scan_kernel, H=H, dk=dk, dv=dv, C=C, scale=dk ** -0.5),
        grid=(B, L // C),
        in_specs=[qk(0), qk(1), vg(1), vg(2), tab, tab, pl.BlockSpec((1, H), lambda b, c: (0, 0))],
        out_specs=vg(0),
        out_shape=jax.ShapeDtypeStruct((B, L, H * dv), BF16),
        scratch_shapes=[pltpu.VMEM((H, dk, dv), F32)],
        compiler_params=_cparams("parallel", "arbitrary"),
        name="ret_scan",
    )(proj, proj, proj, proj, cos, sin, log_gamma.reshape(1, H))


def retnet_layer(h, nw, p):
    B, L, D = h.shape
    w_in = p['in_proj']
    vd = p['out_proj'].shape[0]
    dv = vd // RET_HEADS
    dk = (w_in.shape[1] - 2 * vd) // (2 * RET_HEADS)
    assert 2 * RET_HEADS * dk == vd
    h2 = h.reshape(B * L, D)
    proj = norm_linear(h2, nw, w_in.astype(BF16), 3072, BF16).reshape(B, L, -1)
    half = dk // 2
    inv_freq = 1.0 / (ROPE_BASE ** jnp.linspace(0.0, 1.0, half, dtype=F32))
    ang = jnp.arange(L, dtype=F32)[:, None] * inv_freq[None, :]
    log_gamma = jnp.log1p(-jnp.exp2(-5.0 - jnp.arange(RET_HEADS, dtype=F32)))
    y = ret_scan(proj, jnp.cos(ang), jnp.sin(ang), log_gamma, dk=dk, dv=dv)
    return y.reshape(B * L, vd), p['out_proj'].astype(BF16)


def _gla_kernel(q_ref, k_ref, v_ref, r_ref, glr_ref, gup_ref, gb_ref, nw_ref, o_ref, s_ref,
                *, H, dk, dv, scale):
    C = GLA_CHUNK
    SB = GLA_SUB

    @pl.when(pl.program_id(1) == 0)
    def _():
        s_ref[...] = jnp.zeros_like(s_ref)

    n_sub = q_ref.shape[0] // C
    rows = [slice(j * C, (j + 1) * C) for j in range(n_sub)]
    gup, gb = gup_ref[...], gb_ref[...]
    gc_alls = [_cumsum_rows(jax.nn.log_sigmoid(_dot(glr_ref[rw, :], gup) + gb) / GLA_TAU) for rw in rows]
    row_id = lax.broadcasted_iota(jnp.int32, (C, 1), 0)
    sub_row = lax.broadcasted_iota(jnp.int32, (SB, 1), 0)
    col_id = lax.broadcasted_iota(jnp.int32, (1, C), 1)
    units = [(j, hd) for j in range(n_sub) for hd in range(H)]
    qs = [q_ref[rows[j], hd * dk:(hd + 1) * dk].astype(F32) * scale for j, hd in units]
    ks = [k_ref[rows[j], hd * dk:(hd + 1) * dk].astype(F32) for j, hd in units]
    gcs = [gc_alls[j][:, hd * dk:(hd + 1) * dk] for j, hd in units]
    vs = [v_ref[rows[j], hd * dv:(hd + 1) * dv] for j, hd in units]
    a_rows = [[] for _ in units]
    for i in range(C // SB):
        lo = i * SB
        for u in range(len(units)):
            qi = qs[u][lo:lo + SB]
            gi = gcs[u][lo:lo + SB]
            ki = ks[u][lo:lo + SB]
            cols = []
            for j in range(SB):
                d = jnp.where(sub_row >= j, gi - gi[j:j + 1], -jnp.inf)
                col = jnp.sum(qi * ki[j:j + 1] * jnp.exp(d), axis=-1, keepdims=True)
                cols.append(jnp.where(col_id == lo + j, col, 0.0))
            blk = _tree_sum(cols)
            if i > 0:
                ref = gcs[u][lo - 1:lo]
                qt = qi * jnp.exp(gi - ref)
                kt = ks[u] * jnp.exp(jnp.where(row_id < lo, ref - gcs[u], -jnp.inf))
                blk = blk + jnp.where(col_id < lo, _dot_nt(qt, kt), 0.0)
            a_rows[u].append(blk)
    intra = [_dot(jnp.concatenate(a_rows[u], axis=0), vs[u]) for u in range(len(units))]
    lasts = [gc[C - 1:C] for gc in gcs]
    kvs = [_dot_tn(vs[u], ks[u] * jnp.exp(lasts[u] - gcs[u])) for u in range(len(units))]
    states = [s_ref[hd] for hd in range(H)]
    nw = nw_ref[...]
    for j in range(n_sub):
        outs = []
        for hd in range(H):
            u = j * H + hd
            o = intra[u] + _dot_nt(qs[u] * jnp.exp(gcs[u]), states[hd])
            states[hd] = states[hd] * jnp.exp(lasts[u]) + kvs[u]
            o = o * lax.rsqrt(jnp.mean(o * o, axis=-1, keepdims=True) + NORM_EPS) * nw
            outs.append(o * _silu(r_ref[rows[j], hd * dv:(hd + 1) * dv].astype(F32)))
        o_ref[rows[j], :] = jnp.concatenate(outs, axis=-1).astype(o_ref.dtype)
    for hd in range(H):
        s_ref[hd] = states[hd]


def gla_scan(proj, gate_up, gate_bias, norm_w, *, dk, dv, lora_pad):
    B, L, _ = proj.shape
    H, C = GLA_HEADS, GLA_CHUNK * GLA_SUBCHUNKS
    qk = lambda j: pl.BlockSpec((None, C, H * dk), lambda b, c: (b, c, j))
    vr = lambda j: pl.BlockSpec((None, C, H * dv), lambda b, c: (b, c, j))
    g0 = (2 * H * dk + 2 * H * dv) // lora_pad
    full = lambda x: pl.BlockSpec(x.shape, lambda b, c: (0, 0))
    consts = (gate_up, gate_bias.reshape(1, -1), norm_w.reshape(1, -1))
    return pl.pallas_call(
        functools.partial(_gla_kernel, H=H, dk=dk, dv=dv, scale=dk ** -0.5),
        grid=(B, L // C),
        in_specs=[qk(0), qk(1), vr(1), vr(2),
                  pl.BlockSpec((None, C, lora_pad), lambda b, c: (b, c, g0))] + [full(x) for x in consts],
        out_specs=vr(0),
        out_shape=jax.ShapeDtypeStruct((B, L, H * dv), BF16),
        scratch_shapes=[pltpu.VMEM((H, dv, dk), F32)],
        compiler_params=_cparams("parallel", "arbitrary"),
        name="gla_scan",
    )(proj, proj, proj, proj, proj, *consts)


def gla_layer(h, nw, p):
    B, L, D = h.shape
    lora, qk = p['gate_up'].shape
    vd = p['out_proj'].shape[0]
    assert 2 * qk == vd
    lora_pad = 128
    w_in = jnp.pad(p['in_proj'], ((0, 0), (0, lora_pad - lora)))
    n_all = w_in.shape[1]
    gate_up = jnp.pad(p['gate_up'], ((0, lora_pad - lora), (0, 0))).astype(BF16)
    h2 = h.reshape(B * L, D)
    proj = norm_linear(h2, nw, w_in.astype(BF16), n_all, BF16).reshape(B, L, n_all)
    o = gla_scan(proj, gate_up, p['gate_bias'], p['norm_w'],
                 dk=qk // GLA_HEADS, dv=vd // GLA_HEADS, lora_pad=lora_pad)
    return o.reshape(B * L, vd), p['out_proj'].astype(BF16)


def kernel(x, meta_tokens, norm_mix, norm_mlp, norm_final, mlp_w_in, mlp_w_out, rwkv_mu, rwkv_w_r, rwkv_w_k, rwkv_w_v, rwkv_w0, rwkv_w_lora_a, rwkv_w_lora_b, rwkv_a0, rwkv_a_lora_a, rwkv_a_lora_b, rwkv_g_lora_a, rwkv_g_lora_b, rwkv_k_k, rwkv_k_a, rwkv_r_k, rwkv_ln_w, rwkv_ln_b, rwkv_w_o, m2_in_proj, m2_conv_w, m2_conv_b, m2_dt_bias, m2_a_log, m2_d, m2_norm_w, m2_out_proj, gla_in_proj, gla_gate_up, gla_gate_bias, gla_norm_w, gla_out_proj, ret_in_proj, ret_out_proj):
    B, S, D = x.shape
    depth = norm_mix.shape[0]
    n_tok = N_META + S
    L = -(-n_tok // SEQ_ALIGN) * SEQ_ALIGN
    meta = jnp.broadcast_to(meta_tokens[None].astype(x.dtype), (B, N_META, D))
    h = jnp.concatenate([meta, x, jnp.zeros((B, L - n_tok, D), x.dtype)], axis=1)
    for i in range(depth):
        m, j = i % 4, i // 4
        if m == 0:
            a, wo = rwkv_layer(h, norm_mix[i], dict(
                mu=rwkv_mu[j], w_r=rwkv_w_r[j], w_k=rwkv_w_k[j], w_v=rwkv_w_v[j], w0=rwkv_w0[j],
                w_lora_a=rwkv_w_lora_a[j], w_lora_b=rwkv_w_lora_b[j], a0=rwkv_a0[j],
                a_lora_a=rwkv_a_lora_a[j], a_lora_b=rwkv_a_lora_b[j],
                g_lora_a=rwkv_g_lora_a[j], g_lora_b=rwkv_g_lora_b[j],
                k_k=rwkv_k_k[j], k_a=rwkv_k_a[j], r_k=rwkv_r_k[j],
                ln_w=rwkv_ln_w[j], ln_b=rwkv_ln_b[j], w_o=rwkv_w_o[j]))
        elif m == 1:
            a, wo = mamba2_layer(h, norm_mix[i], dict(
                in_proj=m2_in_proj[j], conv_w=m2_conv_w[j], conv_b=m2_conv_b[j],
                dt_bias=m2_dt_bias[j], a_log=m2_a_log[j], d=m2_d[j], norm_w=m2_norm_w[j],
                out_proj=m2_out_proj[j]))
        elif m == 2:
            a, wo = gla_layer(h, norm_mix[i], dict(
                in_proj=gla_in_proj[j], gate_up=gla_gate_up[j], gate_bias=gla_gate_bias[j],
                norm_w=gla_norm_w[j], out_proj=gla_out_proj[j]))
        else:
            a, wo = retnet_layer(h, norm_mix[i], dict(in_proj=ret_in_proj[j], out_proj=ret_out_proj[j]))
        h = mlp_res(linear_res(a, wo, h.reshape(B * L, D)), norm_mlp[i], mlp_w_in[i].astype(BF16),
                    mlp_w_out[i].astype(BF16), norm_final, final=(i == depth - 1)).reshape(B, L, D)
    return h[:, N_META:n_tok]
```

```python
import functools
import math

import jax
import jax.numpy as jnp
from jax import lax
from jax.experimental import pallas as pl
from jax.experimental.pallas import tpu as pltpu

F32 = jnp.float32
BF16 = jnp.bfloat16

N_META = 16
NORM_EPS = 1e-5
SEQ_ALIGN = 192
MAX_ROW_TILE = 640
MAX_WIDE_ROW_TILE = 1152
BF16_SUBLANES = 16
F32_SUBLANES = 8
VMEM_LIMIT = 56 * 1024 * 1024

RWKV_HEAD = 64
RWKV_GN_EPS = 64e-5
RWKV_CHUNK = 64
RWKV_SUBCHUNKS = 3

M2_HEAD = 64
M2_GROUPS = 8
M2_STATE = 128
M2_CONV = 4
M2_CHUNK = 192
M2_GROUPS_PER_STEP = 8

GLA_HEADS = 4
GLA_TAU = 16.0
GLA_CHUNK = 64
GLA_SUBCHUNKS = 3
GLA_SUB = 8

RET_HEADS = 4
RET_CHUNK = 192
ROPE_BASE = 10000.0


def _cparams(*sem):
    return pltpu.CompilerParams(dimension_semantics=sem, vmem_limit_bytes=VMEM_LIMIT)


def _row_tile(rows, limit=MAX_ROW_TILE):
    for t in range(min(rows, limit) // BF16_SUBLANES * BF16_SUBLANES, 0, -BF16_SUBLANES):
        if rows % t == 0:
            return t
    raise ValueError(rows)


def _dot(a, b):
    return jnp.dot(a.astype(BF16), b.astype(BF16), preferred_element_type=F32)


def _dot_nt(a, b):
    return lax.dot_general(a.astype(BF16), b.astype(BF16), (((1,), (1,)), ((), ())),
                           preferred_element_type=F32)


def _dot_tn(a, b):
    return lax.dot_general(a.astype(BF16), b.astype(BF16), (((0,), (0,)), ((), ())),
                           preferred_element_type=F32)


def _split3(x):
    hi = x.astype(BF16)
    r1 = x - hi.astype(F32)
    mid = r1.astype(BF16)
    return hi, mid, (r1 - mid.astype(F32)).astype(BF16)


def _cumsum_rows(x):
    C = x.shape[0]
    r = lax.broadcasted_iota(jnp.int32, (C, 3 * C), 0)
    c = lax.broadcasted_iota(jnp.int32, (C, 3 * C), 1)
    tri3 = (r >= lax.rem(c, C)).astype(BF16)
    return jnp.dot(tri3, jnp.concatenate(_split3(x), axis=0), preferred_element_type=F32)


def _cumsum_lanes(x):
    C = x.shape[1]
    r = lax.broadcasted_iota(jnp.int32, (3 * C, C), 0)
    c = lax.broadcasted_iota(jnp.int32, (3 * C, C), 1)
    tri3 = (lax.rem(r, C) <= c).astype(BF16)
    return jnp.dot(jnp.concatenate(_split3(x), axis=1), tri3, preferred_element_type=F32)


def _rms_rows(x, w):
    return x * lax.rsqrt(jnp.mean(x * x, axis=-1, keepdims=True) + NORM_EPS) * w


def _tri(n, strict=False, upper=False):
    r = lax.broadcasted_iota(jnp.int32, (n, n), 0)
    c = lax.broadcasted_iota(jnp.int32, (n, n), 1)
    if upper:
        r, c = c, r
    return (r > c) if strict else (r >= c)


def _silu(x):
    half = 0.5 * x
    return half + half * jnp.tanh(half)


def _tree_sum(xs):
    while len(xs) > 1:
        xs = [xs[i] + xs[i + 1] if i + 1 < len(xs) else xs[i] for i in range(0, len(xs), 2)]
    return xs[0]


def _norm_linear_kernel(h_ref, nw_ref, w_ref, o_ref, u_ref):
    @pl.when(pl.program_id(1) == 0)
    def _():
        u_ref[...] = _rms_rows(h_ref[...], nw_ref[...]).astype(BF16)

    o_ref[...] = jnp.dot(u_ref[...], w_ref[...], preferred_element_type=F32).astype(o_ref.dtype)


def norm_linear(h, nw, w, tn, out_dtype):
    M, D = h.shape
    N = w.shape[1]
    tm = _row_tile(M, MAX_WIDE_ROW_TILE)
    return pl.pallas_call(
        _norm_linear_kernel,
        grid=(M // tm, N // tn),
        in_specs=[pl.BlockSpec((tm, D), lambda i, j: (i, 0)),
                  pl.BlockSpec((1, D), lambda i, j: (0, 0)),
                  pl.BlockSpec((D, tn), lambda i, j: (0, j))],
        out_specs=pl.BlockSpec((tm, tn), lambda i, j: (i, j)),
        out_shape=jax.ShapeDtypeStruct((M, N), out_dtype),
        scratch_shapes=[pltpu.VMEM((tm, D), BF16)],
        compiler_params=_cparams("parallel", "arbitrary"),
        name="norm_linear",
    )(h, nw.reshape(1, D), w)


def _linear_res_kernel(a_ref, w_ref, h_ref, o_ref):
    o_ref[...] = h_ref[...] + jnp.dot(a_ref[...], w_ref[...], preferred_element_type=F32)


def linear_res(a, w, h):
    M, K = a.shape
    D = w.shape[1]
    tm = _row_tile(M, MAX_WIDE_ROW_TILE)
    return pl.pallas_call(
        _linear_res_kernel,
        grid=(M // tm,),
        in_specs=[pl.BlockSpec((tm, K), lambda i: (i, 0)),
                  pl.BlockSpec((K, D), lambda i: (0, 0)),
                  pl.BlockSpec((tm, D), lambda i: (i, 0))],
        out_specs=pl.BlockSpec((tm, D), lambda i: (i, 0)),
        out_shape=jax.ShapeDtypeStruct((M, D), F32),
        compiler_params=_cparams("parallel"),
        name="linear_res",
    )(a, w, h)


def _mlp_kernel(h_ref, nw_ref, w1_ref, w2_ref, fnw_ref, o_ref, u_ref, *, final):
    f = pl.program_id(1)

    @pl.when(f == 0)
    def _():
        h = h_ref[...]
        u_ref[...] = _rms_rows(h, nw_ref[...]).astype(BF16)
        o_ref[...] = h

    hid = jnp.dot(u_ref[...], w1_ref[...], preferred_element_type=F32)
    hid = jnp.square(jnp.maximum(hid, 0.0))
    o_ref[...] += _dot(hid, w2_ref[...])

    if final:
        @pl.when(f == pl.num_programs(1) - 1)
        def _():
            o_ref[...] = _rms_rows(o_ref[...], fnw_ref[...])


def mlp_res(h, nw, w1, w2, fnw, final, tf=2048):
    M, D = h.shape
    F = w1.shape[1]
    tm = _row_tile(M, MAX_WIDE_ROW_TILE)
    row = pl.BlockSpec((tm, D), lambda i, f: (i, 0))
    vec = pl.BlockSpec((1, D), lambda i, f: (0, 0))
    return pl.pallas_call(
        functools.partial(_mlp_kernel, final=final),
        grid=(M // tm, F // tf),
        in_specs=[row, vec,
                  pl.BlockSpec((D, tf), lambda i, f: (0, f)),
                  pl.BlockSpec((tf, D), lambda i, f: (f, 0)), vec],
        out_specs=row,
        out_shape=jax.ShapeDtypeStruct((M, D), F32),
        scratch_shapes=[pltpu.VMEM((tm, D), BF16)],
        compiler_params=_cparams("parallel", "arbitrary"),
        name="mlp_res",
    )(h, nw.reshape(1, D), w1, w2, fnw.reshape(1, D))


def _rwkv_proj_kernel(h_ref, halo_ref, nw_ref, mu_ref, wr_ref, wk_ref, wv_ref,
                      wla_ref, wlb_ref, ala_ref, alb_ref, gla_ref, glb_ref,
                      w0_ref, a0_ref,
                      r_ref, k_ref, v_ref, lw_ref, a_ref, g_ref, *, tiles_per_batch):
    nw = nw_ref[...]
    u = _rms_rows(h_ref[...], nw)
    first = pl.program_id(0) % tiles_per_batch == 0
    prev_last = _rms_rows(halo_ref[F32_SUBLANES - 1:F32_SUBLANES, :], nw)
    prev_last = jnp.where(first, 0.0, prev_last)
    row = lax.broadcasted_iota(jnp.int32, (u.shape[0], 1), 0)
    xx = jnp.where(row == 0, prev_last, pltpu.roll(u, 1, axis=0)) - u
    mu = mu_ref[...]
    mix = lambda i: (u + xx * mu[i:i + 1, :]).astype(BF16)
    r_ref[...] = jnp.dot(mix(0), wr_ref[...], preferred_element_type=F32).astype(r_ref.dtype)
    k_ref[...] = jnp.dot(mix(2), wk_ref[...], preferred_element_type=F32).astype(k_ref.dtype)
    v_ref[...] = jnp.dot(mix(3), wv_ref[...], preferred_element_type=F32).astype(v_ref.dtype)
    hw = jnp.tanh(jnp.dot(mix(1), wla_ref[...], preferred_element_type=F32))
    w = -jax.nn.softplus(-(w0_ref[...] + _dot(hw, wlb_ref[...]))) - 0.5
    lw_ref[...] = -jnp.exp(w)
    ha = jnp.dot(mix(4), ala_ref[...], preferred_element_type=F32)
    a_ref[...] = jax.nn.sigmoid(a0_ref[...] + _dot(ha, alb_ref[...]))
    hg = jax.nn.sigmoid(jnp.dot(mix(5), gla_ref[...], preferred_element_type=F32))
    g_ref[...] = _dot(hg, glb_ref[...]).astype(g_ref.dtype)


def rwkv_proj(h, rows_per_batch, nw, mu, wr, wk, wv, wla, wlb, ala, alb, gla, glb, w0, a0):
    M, D = h.shape
    tm = _row_tile(rows_per_batch)
    row = pl.BlockSpec((tm, D), lambda i: (i, 0))
    halo = pl.BlockSpec((F32_SUBLANES, D),
                        lambda i: (jnp.maximum(i * (tm // F32_SUBLANES) - 1, 0), 0))
    full = lambda x: pl.BlockSpec(x.shape, lambda i: (0, 0))
    consts = (nw.reshape(1, D), mu, wr, wk, wv, wla, wlb, ala, alb, gla, glb,
              w0.reshape(1, D), a0.reshape(1, D))
    sds = lambda dt: jax.ShapeDtypeStruct((M, D), dt)
    return pl.pallas_call(
        functools.partial(_rwkv_proj_kernel, tiles_per_batch=rows_per_batch // tm),
        grid=(M // tm,),
        in_specs=[row, halo] + [full(c) for c in consts],
        out_specs=[row] * 6,
        out_shape=[sds(BF16), sds(BF16), sds(BF16), sds(F32), sds(F32), sds(BF16)],
        compiler_params=_cparams("parallel"),
        name="rwkv_proj",
    )(h, h, *consts)


def _rwkv_scan_kernel(r_ref, k_ref, v_ref, lw_ref, a_ref, g_ref,
                      kk_ref, ka_ref, rk_ref, lnw_ref, lnb_ref, o_ref, s_ref):
    C = RWKV_CHUNK
    N = RWKV_HEAD
    assert C == N

    @pl.when(pl.program_id(1) == 0)
    def _():
        s_ref[...] = jnp.zeros_like(s_ref)

    W = 2 * N
    lane = lax.broadcasted_iota(jnp.int32, (1, W), 1)
    left = lane < N
    row_c = lax.broadcasted_iota(jnp.int32, (C, 1), 0)
    strict = row_c > (lane & (N - 1))
    incl = row_c >= (lane & (N - 1))
    mask2 = jnp.concatenate([strict, incl], axis=0)
    row_w = lax.broadcasted_iota(jnp.int32, (W, 1), 0)
    bdiag = (row_w < N) == left
    eye = (row_w == lane).astype(F32)
    split = lambda x: jnp.concatenate([jnp.where(left, x, 0.0), jnp.where(left, 0.0, x)], axis=0)

    def head_sum(x):
        sa = jnp.sum(jnp.where(left, x, 0.0), axis=-1, keepdims=True)
        sb = jnp.sum(jnp.where(left, 0.0, x), axis=-1, keepdims=True)
        return jnp.where(left, sa, sb)

    n_pairs = r_ref.shape[-1] // W
    n_sub = r_ref.shape[0] // C
    sls = [slice(p * W, (p + 1) * W) for p in range(n_pairs)]
    items = [(j, p) for j in range(n_sub) for p in range(n_pairs)]
    rows = [slice(j * C, (j + 1) * C) for j in range(n_sub)]
    lws = [lw_ref[rw, :] for rw in rows]
    cums = [_cumsum_rows(lw) for lw in lws]
    pre = []
    for j, p in items:
        rw, sl = rows[j], sls[p]
        r = r_ref[rw, sl].astype(F32)
        k = k_ref[rw, sl].astype(F32)
        a = a_ref[rw, sl]
        cum = cums[j][:, sl]
        kkr = k * kk_ref[:, sl]
        kk = kkr * lax.rsqrt(jnp.maximum(head_sum(kkr * kkr), 1e-24))
        km = k * (1.0 + (a - 1.0) * ka_ref[:, sl])
        beta = kk * a
        e_neg = jnp.exp(-cum)
        lhs = jnp.concatenate([-kk * jnp.exp(cum - lws[j][:, sl]), r * jnp.exp(cum)], axis=0)
        pre.append((r, km, beta, cum, lhs, beta * e_neg, km * e_neg))
    x_as = [_dot_nt(jnp.where(left, q[4], 0.0), jnp.concatenate([q[5], q[6]], axis=0)) for q in pre]
    x_bs = [_dot_nt(jnp.where(left, 0.0, q[4]), jnp.concatenate([q[6], q[5]], axis=0)) for q in pre]
    xbeta = [jnp.where(left, xa, xb) for xa, xb in zip(x_as, x_bs)]
    xkey = [jnp.where(mask2, jnp.where(left, xb, xa), 0.0) for xa, xb in zip(x_as, x_bs)]
    ps = [split(jnp.where(strict, x[:C], 0.0)) for x in xbeta]
    ts = [eye + q for q in ps]
    n_fac = int(math.log2(C))
    for i in range(1, n_fac):
        if i == 1:
            ps = [_dot(q, q) for q in ps]
        zs = [_dot(q, jnp.concatenate([q, t], axis=-1)) if i < n_fac - 1 else _dot(q, t)
              for q, t in zip(ps, ts)]
        ts = [t + z[:, -W:] for t, z in zip(ts, zs)]
        ps = [z[:, :W] for z in zs]
    states = [s_ref[p] for p in range(n_pairs)]
    for j in range(n_sub):
        rw = rows[j]
        base = j * n_pairs
        vs = [v_ref[rw, sl].astype(F32) for sl in sls]
        vx = [jnp.concatenate([jnp.where(left, 0.0, v), jnp.where(left, v, 0.0)], axis=0) for v in vs]
        hss = [_dot_nt(pre[base + p][4], states[p]) + _dot(xkey[base + p], vx[p])
               for p in range(n_pairs)]
        ubd = [_dot(ts[base + p], split(hss[p][:C])) for p in range(n_pairs)]
        ys = [hss[p][C:] + _dot(jnp.where(incl, xbeta[base + p][C:], 0.0), ubd[p]) for p in range(n_pairs)]
        new_states = []
        for p in range(n_pairs):
            r, km, beta, cum, _, _, _ = pre[base + p]
            last = cum[C - 1:C, :]
            e_last = jnp.exp(last - cum)
            upd = _dot_tn(jnp.concatenate([vs[p], ubd[p][:C] + ubd[p][C:]], axis=0),
                          jnp.concatenate([km * e_last, beta * e_last], axis=0))
            new_states.append(states[p] * jnp.exp(last) + jnp.where(bdiag, upd, 0.0))
        states = new_states
        outs = []
        for p in range(n_pairs):
            sl = sls[p]
            r, km = pre[base + p][0], pre[base + p][1]
            y = ys[p]
            mu_y = head_sum(y) * (1.0 / N)
            var = head_sum(jnp.square(y - mu_y)) * (1.0 / N)
            yn = (y - mu_y) * lax.rsqrt(var + RWKV_GN_EPS) * lnw_ref[:, sl] + lnb_ref[:, sl]
            bonus = head_sum(r * km * rk_ref[:, sl]) * vs[p]
            outs.append((yn + bonus) * g_ref[rw, sl].astype(F32))
        o_ref[rw, :] = jnp.concatenate(outs, axis=-1).astype(o_ref.dtype)
    for p in range(n_pairs):
        s_ref[p] = states[p]


def rwkv_scan(r, k, v, lw, a, g, k_k, k_a, r_k, ln_w, ln_b):
    B, L, D = r.shape
    rows = RWKV_CHUNK * RWKV_SUBCHUNKS
    seq = pl.BlockSpec((None, rows, D), lambda b, c: (b, c, 0))
    par = pl.BlockSpec((1, D), lambda b, c: (0, 0))
    return pl.pallas_call(
        _rwkv_scan_kernel,
        grid=(B, L // rows),
        in_specs=[seq] * 6 + [par] * 5,
        out_specs=seq,
        out_shape=jax.ShapeDtypeStruct((B, L, D), BF16),
        scratch_shapes=[pltpu.VMEM((D // (2 * RWKV_HEAD), 2 * RWKV_HEAD, 2 * RWKV_HEAD), F32)],
        compiler_params=_cparams("parallel", "arbitrary"),
        name="rwkv_scan",
    )(r, k, v, lw, a, g, *(p.reshape(1, D) for p in (k_k, k_a, r_k, ln_w, ln_b)))


def rwkv_layer(h, nw, p):
    B, L, D = h.shape
    bf = lambda x: x.astype(BF16)
    h2 = h.reshape(B * L, D)
    outs = rwkv_proj(h2, L, nw, p['mu'],
                     bf(p['w_r']), bf(p['w_k']), bf(p['w_v']),
                     bf(p['w_lora_a']), bf(p['w_lora_b']), bf(p['a_lora_a']), bf(p['a_lora_b']),
                     bf(p['g_lora_a']), bf(p['g_lora_b']), p['w0'], p['a0'])
    r, k, v, lw, a, g = (o.reshape(B, L, D) for o in outs)
    o = rwkv_scan(r, k, v, lw, a, g, p['k_k'], p['k_a'], p['r_k'].reshape(D), p['ln_w'], p['ln_b'])
    return o.reshape(B * L, D), bf(p['w_o'])


def _sdecay_groups(qs, ks, vss, ccss, crss, s_ref, P):
    C = qs[0].shape[0]
    causal = _tri(C)
    groups = range(len(qs))
    s0s = [s_ref[g] for g in groups]
    scores = [_dot_nt(qs[g], ks[g]) for g in groups]
    qss = [_dot(qs[g], s0s[g]) for g in groups]
    yss = []
    for g in groups:
        ys = []
        for r, (v, cc, cr) in enumerate(zip(vss[g], ccss[g], crss[g])):
            dec = jnp.exp(jnp.where(causal, cc - cr, -jnp.inf))
            ys.append(_dot(scores[g] * dec, v) + jnp.exp(cc) * qss[g][:, r * P:(r + 1) * P])
        yss.append(ys)
    for g in groups:
        lasts = [cc[C - 1:C, :] for cc in ccss[g]]
        vw = jnp.concatenate([v * jnp.exp(last - cc)
                              for v, cc, last in zip(vss[g], ccss[g], lasts)], axis=-1)
        keep = jnp.concatenate([jnp.broadcast_to(jnp.exp(last), (1, P)) for last in lasts], axis=-1)
        s_ref[g] = keep * s0s[g] + _dot_tn(ks[g], vw)
    return yss


def _m2_scan_kernel(shift_ref, expand_ref, z_ref, xs_ref, xsh_ref, b_ref, bh_ref, c_ref, ch_ref,
                    wx_ref, wb_ref, wc_ref, bx_ref, bb_ref, bc_ref,
                    dtc_ref, dtr_ref, pc_ref, pr_ref, dx_ref, nw_ref, o_ref, s_ref, *, GS, R, P, N, C):
    first = pl.program_id(2) == 0

    @pl.when(first)
    def _():
        s_ref[...] = jnp.zeros_like(s_ref)

    shift = shift_ref[...]

    def conv_silu(x_ref, prev_ref, w_ref, bias_ref):
        x = x_ref[...]
        prev = prev_ref[...]
        xf = jnp.concatenate([jnp.where(first, jnp.zeros_like(prev), prev), x], axis=0)
        taps = jnp.dot(shift, xf, preferred_element_type=F32)
        w = w_ref[...]
        acc = bias_ref[...] + w[M2_CONV - 1:M2_CONV] * x.astype(F32)
        for i in range(M2_CONV - 1):
            acc = acc + w[i:i + 1] * taps[i * C:(i + 1) * C]
        return _silu(acc)

    GP = R * P
    xs = conv_silu(xs_ref, xsh_ref, wx_ref, bx_ref)
    k = conv_silu(b_ref, bh_ref, wb_ref, bb_ref)
    q = conv_silu(c_ref, ch_ref, wc_ref, bc_ref)
    pc = pc_ref[...]
    pr = pr_ref[...]
    dt_c = jax.nn.softplus(dtc_ref[...] + pc[0:1])
    dt_r = jax.nn.softplus(dtr_ref[...] + pr[:, 0:1])
    cc = _cumsum_rows(dt_c * -jnp.exp(pc[1:2]))
    cr = _cumsum_lanes(dt_r * -jnp.exp(pr[:, 1:2]))
    expand = expand_ref[...]

    def widen(x):
        return jnp.dot(jnp.concatenate(_split3(x), axis=-1), expand, preferred_element_type=F32)

    last = cc[C - 1:C]
    e_cum = widen(jnp.exp(cc))
    v = xs * widen(dt_c)
    vw = v * widen(jnp.exp(last - cc))
    keep = e_cum[C - 1:C]
    causal = _tri(C)
    head_of_lane = lax.broadcasted_iota(jnp.int32, (1, GP), 1) // P
    groups = range(GS)
    gsl = [slice(g * GP, (g + 1) * GP) for g in groups]
    qg = [q[:, g * N:(g + 1) * N] for g in groups]
    kg = [k[:, g * N:(g + 1) * N] for g in groups]
    s0s = [s_ref[g] for g in groups]
    scores = [_dot_nt(qg[g], kg[g]) for g in groups]
    qss = [_dot(qg[g], s0s[g]) for g in groups]
    pvs = []
    for g in groups:
        ps = [scores[g] * jnp.exp(jnp.where(causal, cc[:, j:j + 1] - cr[j:j + 1], -jnp.inf))
              for j in range(g * R, (g + 1) * R)]
        vg = v[:, gsl[g]].astype(BF16)
        vbd = jnp.concatenate([jnp.where(head_of_lane == r, vg, jnp.zeros_like(vg)) for r in range(R)], axis=0)
        pvs.append(_dot(jnp.concatenate(ps, axis=-1), vbd))
    for g in groups:
        s_ref[g] = keep[:, gsl[g]] * s0s[g] + _dot_tn(kg[g], vw[:, gsl[g]])
    z = z_ref[...].astype(F32)
    dx = dx_ref[...]
    outs = []
    for g in groups:
        y = pvs[g] + e_cum[:, gsl[g]] * qss[g] + dx[:, gsl[g]] * xs[:, gsl[g]]
        y = y * _silu(z[:, gsl[g]])
        outs.append(y * lax.rsqrt(jnp.mean(y * y, axis=-1, keepdims=True) + NORM_EPS))
    o_ref[...] = (jnp.concatenate(outs, axis=-1) * nw_ref[...]).astype(o_ref.dtype)


def m2_scan(zx, dt, conv_w, conv_b, dt_bias, a_log, d_skip, norm_w, *, d_inner, heads):
    B, L, _ = zx.shape
    G, N, C, GS = M2_GROUPS, M2_STATE, M2_CHUNK, M2_GROUPS_PER_STEP
    R = heads // G
    GP = d_inner // G
    NS = G // GS
    WX, WN, WR = GS * GP, GS * N, GS * R
    x0 = d_inner // WX
    b0 = 2 * d_inner // WN
    c0 = b0 + NS
    wb0 = d_inner // WN
    wc0 = wb0 + NS
    cur = lambda w, off: pl.BlockSpec((None, C, w), lambda b, g, c: (b, c, off + g))
    HB = BF16_SUBLANES
    halo = lambda w, off: pl.BlockSpec(
        (None, HB, w), lambda b, g, c: (b, jnp.maximum(c * (C // HB) - 1, 0), off + g))
    par = lambda rows, w, off: pl.BlockSpec((rows, w), lambda b, g, c: (0, off + g))
    tap_row = jnp.arange((M2_CONV - 1) * C)
    shift = (jnp.arange(HB + C)[None, :] ==
             (HB + tap_row % C - (M2_CONV - 1) + tap_row // C)[:, None]).astype(BF16)
    dt4 = dt.reshape(B, L, NS, WR)
    pcol = jnp.stack([dt_bias, a_log]).reshape(2, NS, WR).transpose(1, 0, 2)
    prow = pcol.transpose(0, 2, 1)
    expand = jnp.tile(jnp.arange(WX)[None, :] // (GP // R) == jnp.arange(WR)[:, None], (3, 1)).astype(BF16)
    cb = conv_b.reshape(1, -1)
    const = lambda x: pl.BlockSpec(x.shape, lambda b, g, c: (0, 0))
    return pl.pallas_call(
        functools.partial(_m2_scan_kernel, GS=GS, R=R, P=GP // R, N=N, C=C),
        grid=(B, NS, L // C),
        in_specs=[const(shift), const(expand),
                  cur(WX, 0), cur(WX, x0), halo(WX, x0), cur(WN, b0), halo(WN, b0), cur(WN, c0), halo(WN, c0),
                  par(M2_CONV, WX, 0), par(M2_CONV, WN, wb0), par(M2_CONV, WN, wc0),
                  par(1, WX, 0), par(1, WN, wb0), par(1, WN, wc0),
                  pl.BlockSpec((None, None, C, WR), lambda b, g, c: (b, g, c, 0)),
                  pl.BlockSpec((None, None, None, WR, C), lambda b, g, c: (b, g, c, 0, 0)),
                  pl.BlockSpec((None, 2, WR), lambda b, g, c: (g, 0, 0)),
                  pl.BlockSpec((None, WR, 2), lambda b, g, c: (g, 0, 0)),
                  par(1, WX, 0), par(1, WX, 0)],
        out_specs=cur(WX, 0),
        out_shape=jax.ShapeDtypeStruct((B, L, d_inner), BF16),
        scratch_shapes=[pltpu.VMEM((GS, N, GP), F32)],
        compiler_params=_cparams("parallel", "parallel", "arbitrary"),
        name="m2_scan",
    )(shift, expand, zx, zx, zx, zx, zx, zx, zx, conv_w, conv_w, conv_w, cb, cb, cb,
      dt4.transpose(0, 2, 1, 3), dt4.reshape(B, L // C, C, NS, WR).transpose(0, 3, 1, 4, 2), pcol, prow,
      jnp.repeat(d_skip, GP // R).reshape(1, -1), norm_w.reshape(1, -1))


def mamba2_layer(h, nw, p):
    B, L, D = h.shape
    d_inner = p['norm_w'].shape[0]
    heads = p['a_log'].shape[0]
    w_in = p['in_proj']
    n_main = w_in.shape[1] - heads
    w_dt = jnp.pad(w_in[:, n_main:], ((0, 0), (0, 128 - heads)))
    h2 = h.reshape(B * L, D)
    zx = norm_linear(h2, nw, w_in[:, :n_main].astype(BF16), 3072, BF16).reshape(B, L, n_main)
    dt = norm_linear(h2, nw, w_dt.astype(BF16), 128, F32)[:, :heads].reshape(B, L, heads)
    y = m2_scan(zx, dt, p['conv_w'], p['conv_b'], p['dt_bias'], p['a_log'], p['d'], p['norm_w'],
                d_inner=d_inner, heads=heads)
    return y.reshape(B * L, d_inner), p['out_proj'].astype(BF16)


def _ret_scan_kernel(q_ref, k_ref, v_ref, g_ref, cos_ref, sin_ref, lg_ref, o_ref, s_ref,
                     *, H, dk, dv, C, scale):
    @pl.when(pl.program_id(1) == 0)
    def _():
        s_ref[...] = jnp.zeros_like(s_ref)

    cos = cos_ref[...]
    sin = sin_ref[...]
    half = dk // 2

    def rotary(x_ref, hd):
        x1 = x_ref[:, hd * dk:hd * dk + half].astype(F32)
        x2 = x_ref[:, hd * dk + half:(hd + 1) * dk].astype(F32)
        return jnp.concatenate([x1 * cos - x2 * sin, x1 * sin + x2 * cos], axis=-1)

    lg = lg_ref[...]
    steps_c = (lax.broadcasted_iota(jnp.int32, (C, 1), 0) + 1).astype(F32)
    steps_r = (lax.broadcasted_iota(jnp.int32, (1, C), 1) + 1).astype(F32)
    yss = _sdecay_groups(
        [rotary(q_ref, hd) for hd in range(H)], [rotary(k_ref, hd) * scale for hd in range(H)],
        [[v_ref[:, hd * dv:(hd + 1) * dv]] for hd in range(H)],
        [[steps_c * lg[:, hd:hd + 1]] for hd in range(H)],
        [[steps_r * lg[:, hd:hd + 1]] for hd in range(H)], s_ref, dv)
    outs = []
    for hd in range(H):
        y = yss[hd][0]
        mu = jnp.mean(y, axis=-1, keepdims=True)
        var = jnp.mean(jnp.square(y - mu), axis=-1, keepdims=True)
        outs.append(_silu(g_ref[:, hd * dv:(hd + 1) * dv].astype(F32)) * ((y - mu) * lax.rsqrt(var + NORM_EPS)))
    o_ref[...] = jnp.concatenate(outs, axis=-1).astype(o_ref.dtype)


def ret_scan(proj, cos, sin, log_gamma, *, dk, dv):
    B, L, _ = proj.shape
    H, C = RET_HEADS, RET_CHUNK
    qk = lambda j: pl.BlockSpec((None, C, H * dk), lambda b, c: (b, c, j))
    vg = lambda j: pl.BlockSpec((None, C, H * dv), lambda b, c: (b, c, j))
    tab = pl.BlockSpec((C, dk // 2), lambda b, c: (c, 0))
    return pl.pallas_call(
        functools.partial(_ret_scan_kernel, H=H, dk=dk, dv=dv, C=C, scale=dk ** -0.5),
        grid=(B, L // C),
        in_specs=[qk(0), qk(1), vg(1), vg(2), tab, tab, pl.BlockSpec((1, H), lambda b, c: (0, 0))],
        out_specs=vg(0),
        out_shape=jax.ShapeDtypeStruct((B, L, H * dv), BF16),
        scratch_shapes=[pltpu.VMEM((H, dk, dv), F32)],
        compiler_params=_cparams("parallel", "arbitrary"),
        name="ret_scan",
    )(proj, proj, proj, proj, cos, sin, log_gamma.reshape(1, H))


def retnet_layer(h, nw, p):
    B, L, D = h.shape
    w_in = p['in_proj']
    vd = p['out_proj'].shape[0]
    dv = vd // RET_HEADS
    dk = (w_in.shape[1] - 2 * vd) // (2 * RET_HEADS)
    assert 2 * RET_HEADS * dk == vd
    h2 = h.reshape(B * L, D)
    proj = norm_linear(h2, nw, w_in.astype(BF16), 3072, BF16).reshape(B, L, -1)
    half = dk // 2
    inv_freq = 1.0 / (ROPE_BASE ** jnp.linspace(0.0, 1.0, half, dtype=F32))
    ang = jnp.arange(L, dtype=F32)[:, None] * inv_freq[None, :]
    log_gamma = jnp.log1p(-jnp.exp2(-5.0 - jnp.arange(RET_HEADS, dtype=F32)))
    y = ret_scan(proj, jnp.cos(ang), jnp.sin(ang), log_gamma, dk=dk, dv=dv)
    return y.reshape(B * L, vd), p['out_proj'].astype(BF16)


def _gla_kernel(q_ref, k_ref, v_ref, r_ref, glr_ref, gup_ref, gb_ref, nw_ref, o_ref, s_ref,
                *, H, dk, dv, scale):
    C = GLA_CHUNK
    SB = GLA_SUB

    @pl.when(pl.program_id(1) == 0)
    def _():
        s_ref[...] = jnp.zeros_like(s_ref)

    n_sub = q_ref.shape[0] // C
    rows = [slice(j * C, (j + 1) * C) for j in range(n_sub)]
    gup, gb = gup_ref[...], gb_ref[...]
    gc_alls = [_cumsum_rows(jax.nn.log_sigmoid(_dot(glr_ref[rw, :], gup) + gb) / GLA_TAU) for rw in rows]
    row_id = lax.broadcasted_iota(jnp.int32, (C, 1), 0)
    sub_row = lax.broadcasted_iota(jnp.int32, (SB, 1), 0)
    col_id = lax.broadcasted_iota(jnp.int32, (1, C), 1)
    units = [(j, hd) for j in range(n_sub) for hd in range(H)]
    qs = [q_ref[rows[j], hd * dk:(hd + 1) * dk].astype(F32) * scale for j, hd in units]
    ks = [k_ref[rows[j], hd * dk:(hd + 1) * dk].astype(F32) for j, hd in units]
    gcs = [gc_alls[j][:, hd * dk:(hd + 1) * dk] for j, hd in units]
    vs = [v_ref[rows[j], hd * dv:(hd + 1) * dv] for j, hd in units]
    a_rows = [[] for _ in units]
    for i in range(C // SB):
        lo = i * SB
        for u in range(len(units)):
            qi = qs[u][lo:lo + SB]
            gi = gcs[u][lo:lo + SB]
            ki = ks[u][lo:lo + SB]
            cols = []
            for j in range(SB):
                d = jnp.where(sub_row >= j, gi - gi[j:j + 1], -jnp.inf)
                col = jnp.sum(qi * ki[j:j + 1] * jnp.exp(d), axis=-1, keepdims=True)
                cols.append(jnp.where(col_id == lo + j, col, 0.0))
            blk = _tree_sum(cols)
            if i > 0:
                ref = gcs[u][lo - 1:lo]
                qt = qi * jnp.exp(gi - ref)
                kt = ks[u] * jnp.exp(jnp.where(row_id < lo, ref - gcs[u], -jnp.inf))
                blk = blk + jnp.where(col_id < lo, _dot_nt(qt, kt), 0.0)
            a_rows[u].append(blk)
    intra = [_dot(jnp.concatenate(a_rows[u], axis=0), vs[u]) for u in range(len(units))]
    lasts = [gc[C - 1:C] for gc in gcs]
    kvs = [_dot_tn(vs[u], ks[u] * jnp.exp(lasts[u] - gcs[u])) for u in range(len(units))]
    states = [s_ref[hd] for hd in range(H)]
    nw = nw_ref[...]
    for j in range(n_sub):
        outs = []
        for hd in range(H):
            u = j * H + hd
            o = intra[u] + _dot_nt(qs[u] * jnp.exp(gcs[u]), states[hd])
            states[hd] = states[hd] * jnp.exp(lasts[u]) + kvs[u]
            o = o * lax.rsqrt(jnp.mean(o * o, axis=-1, keepdims=True) + NORM_EPS) * nw
            outs.append(o * _silu(r_ref[rows[j], hd * dv:(hd + 1) * dv].astype(F32)))
        o_ref[rows[j], :] = jnp.concatenate(outs, axis=-1).astype(o_ref.dtype)
    for hd in range(H):
        s_ref[hd] = states[hd]


def gla_scan(proj, gate_up, gate_bias, norm_w, *, dk, dv, lora_pad):
    B, L, _ = proj.shape
    H, C = GLA_HEADS, GLA_CHUNK * GLA_SUBCHUNKS
    qk = lambda j: pl.BlockSpec((None, C, H * dk), lambda b, c: (b, c, j))
    vr = lambda j: pl.BlockSpec((None, C, H * dv), lambda b, c: (b, c, j))
    g0 = (2 * H * dk + 2 * H * dv) // lora_pad
    full = lambda x: pl.BlockSpec(x.shape, lambda b, c: (0, 0))
    consts = (gate_up, gate_bias.reshape(1, -1), norm_w.reshape(1, -1))
    return pl.pallas_call(
        functools.partial(_gla_kernel, H=H, dk=dk, dv=dv, scale=dk ** -0.5),
        grid=(B, L // C),
        in_specs=[qk(0), qk(1), vr(1), vr(2),
                  pl.BlockSpec((None, C, lora_pad), lambda b, c: (b, c, g0))] + [full(x) for x in consts],
        out_specs=vr(0),
        out_shape=jax.ShapeDtypeStruct((B, L, H * dv), BF16),
        scratch_shapes=[pltpu.VMEM((H, dv, dk), F32)],
        compiler_params=_cparams("parallel", "arbitrary"),
        name="gla_scan",
    )(proj, proj, proj, proj, proj, *consts)


def gla_layer(h, nw, p):
    B, L, D = h.shape
    lora, qk = p['gate_up'].shape
    vd = p['out_proj'].shape[0]
    assert 2 * qk == vd
    lora_pad = 128
    w_in = jnp.pad(p['in_proj'], ((0, 0), (0, lora_pad - lora)))
    n_all = w_in.shape[1]
    gate_up = jnp.pad(p['gate_up'], ((0, lora_pad - lora), (0, 0))).astype(BF16)
    h2 = h.reshape(B * L, D)
    proj = norm_linear(h2, nw, w_in.astype(BF16), n_all, BF16).reshape(B, L, n_all)
    o = gla_scan(proj, gate_up, p['gate_bias'], p['norm_w'],
                 dk=qk // GLA_HEADS, dv=vd // GLA_HEADS, lora_pad=lora_pad)
    return o.reshape(B * L, vd), p['out_proj'].astype(BF16)


def kernel(x, meta_tokens, norm_mix, norm_mlp, norm_final, mlp_w_in, mlp_w_out, rwkv_mu, rwkv_w_r, rwkv_w_k, rwkv_w_v, rwkv_w0, rwkv_w_lora_a, rwkv_w_lora_b, rwkv_a0, rwkv_a_lora_a, rwkv_a_lora_b, rwkv_g_lora_a, rwkv_g_lora_b, rwkv_k_k, rwkv_k_a, rwkv_r_k, rwkv_ln_w, rwkv_ln_b, rwkv_w_o, m2_in_proj, m2_conv_w, m2_conv_b, m2_dt_bias, m2_a_log, m2_d, m2_norm_w, m2_out_proj, gla_in_proj, gla_gate_up, gla_gate_bias, gla_norm_w, gla_out_proj, ret_in_proj, ret_out_proj):
    B, S, D = x.shape
    depth = norm_mix.shape[0]
    n_tok = N_META + S
    L = -(-n_tok // SEQ_ALIGN) * SEQ_ALIGN
    meta = jnp.broadcast_to(meta_tokens[None].astype(x.dtype), (B, N_META, D))
    h = jnp.concatenate([meta, x, jnp.zeros((B, L - n_tok, D), x.dtype)], axis=1)
    for i in range(depth):
        m, j = i % 4, i // 4
        if m == 0:
            a, wo = rwkv_layer(h, norm_mix[i], dict(
                mu=rwkv_mu[j], w_r=rwkv_w_r[j], w_k=rwkv_w_k[j], w_v=rwkv_w_v[j], w0=rwkv_w0[j],
                w_lora_a=rwkv_w_lora_a[j], w_lora_b=rwkv_w_lora_b[j], a0=rwkv_a0[j],
                a_lora_a=rwkv_a_lora_a[j], a_lora_b=rwkv_a_lora_b[j],
                g_lora_a=rwkv_g_lora_a[j], g_lora_b=rwkv_g_lora_b[j],
                k_k=rwkv_k_k[j], k_a=rwkv_k_a[j], r_k=rwkv_r_k[j],
                ln_w=rwkv_ln_w[j], ln_b=rwkv_ln_b[j], w_o=rwkv_w_o[j]))
        elif m == 1:
            a, wo = mamba2_layer(h, norm_mix[i], dict(
                in_proj=m2_in_proj[j], conv_w=m2_conv_w[j], conv_b=m2_conv_b[j],
                dt_bias=m2_dt_bias[j], a_log=m2_a_log[j], d=m2_d[j], norm_w=m2_norm_w[j],
                out_proj=m2_out_proj[j]))
        elif m == 2:
            a, wo = gla_layer(h, norm_mix[i], dict(
                in_proj=gla_in_proj[j], gate_up=gla_gate_up[j], gate_bias=gla_gate_bias[j],
                norm_w=gla_norm_w[j], out_proj=gla_out_proj[j]))
        else:
            a, wo = retnet_layer(h, norm_mix[i], dict(in_proj=ret_in_proj[j], out_proj=ret_out_proj[j]))
        h = mlp_res(linear_res(a, wo, h.reshape(B * L, D)), norm_mlp[i], mlp_w_in[i].astype(BF16),
                    mlp_w_out[i].astype(BF16), norm_final, final=(i == depth - 1)).reshape(B, L, D)
    return h[:, N_META:n_tok]
```
